```python
import math
import jax, jax.numpy as jnp
from jax import lax
import numpy as np

D_MODEL = 2048
BATCH = 8
SEQ = 2048
DEPTH = 4

N_EVEN = (DEPTH + 1) // 2
N_ODD = DEPTH // 2
BLOCK = 128
ROPE_THETA = 10000.0
NORM_EPS = 1e-6

A_HEAD_DIM = 128
A_HEADS = D_MODEL // 2 // A_HEAD_DIM
A_WIDTH = A_HEADS * A_HEAD_DIM
A_PATTERNS = ((128, 1), (512, 4), (2048, 16))
B_WIDTH = D_MODEL // 2
B_BLOCKS = 8
B_BLOCK_DIM = B_WIDTH // B_BLOCKS
B_CONV = 4
LRU_C = 8.0
EVEN_IN = 3 * A_WIDTH + 2 * B_WIDTH
EVEN_MIX = A_WIDTH + B_WIDTH

C_HEAD_DIM = 64
C_HEADS = D_MODEL // 2 // C_HEAD_DIM
C_KV_HEADS = C_HEADS // 8
C_GROUP = C_HEADS // C_KV_HEADS
C_WIDTH = C_HEADS * C_HEAD_DIM
C_KV_WIDTH = C_KV_HEADS * C_HEAD_DIM
C_WINDOW = 128
D_WIDTH = D_MODEL // 2
D_GROUP_DIM = 16
D_GROUPS = D_WIDTH // D_GROUP_DIM
D_STATE = 64
ODD_IN = C_WIDTH + 2 * C_KV_WIDTH + D_WIDTH
ODD_MIX = C_WIDTH + D_WIDTH

D_FF = ((8 * D_MODEL // 3 + 127) // 128) * 128
FFN_CONV = 3

kernel_name = 'hybrid_dilated_lru_swa_s5_trunk'

F32 = jnp.float32


def rmsnorm(x, g):
    x32 = x.astype(F32)
    y = x32 * lax.rsqrt(jnp.mean(x32 * x32, axis=-1, keepdims=True) + NORM_EPS)
    return (y * g.astype(F32)).astype(x.dtype)


def modulate(h, shift, scale):
    return (h.astype(F32) * (1.0 + scale[:, None]) + shift[:, None]).astype(h.dtype)


def rope(x, positions):
    half = x.shape[-1] // 2
    inv = ROPE_THETA ** (-jnp.arange(half, dtype=F32) / half)
    ang = positions.astype(F32)[..., None] * inv
    cos, sin = jnp.cos(ang)[:, :, None, :], jnp.sin(ang)[:, :, None, :]
    x1, x2 = x[..., :half].astype(F32), x[..., half:].astype(F32)
    return jnp.concatenate([x1 * cos - x2 * sin, x2 * cos + x1 * sin], axis=-1)


def causal_dwconv(x, w, b):
    k, s = w.shape[0], x.shape[1]
    xp = jnp.pad(x, ((0, 0), (k - 1, 0), (0, 0)))
    out = b
    for i in range(k):
        out = out + w[i] * xp[:, i:i + s]
    return out


def linear_scan(a, b):
    def combine(l, r):
        al, bl = l
        ar, br = r
        return ar * al, ar * bl + br
    _, h = lax.associative_scan(combine, (a, b), axis=1)
    return h


def banded_window_attention(q, k, v, max_dist):
    n, r, l, dh = q.shape
    blk = min(BLOCK, l)
    nb = l // blk
    pad = -(-max_dist // blk) * blk
    span = pad + blk
    kp = jnp.pad(k.astype(F32), ((0, 0), (pad, 0), (0, 0)))
    vp = jnp.pad(v.astype(F32), ((0, 0), (pad, 0), (0, 0)))
    idx = jnp.arange(nb)[:, None] * blk + jnp.arange(span)[None, :]
    kb, vb = kp[:, idx], vp[:, idx]
    qb = q.astype(F32).reshape(n, r, nb, blk, dh)
    s = jnp.einsum('nrbqd,nbkd->nrbqk', qb, kb) * (dh ** -0.5)
    qi = jnp.arange(blk)[:, None]
    kj = jnp.arange(span)[None, :]
    dist = qi + pad - kj
    kpos = jnp.arange(nb)[:, None, None] * blk + kj[None] - pad
    valid = (dist >= 0) & (dist <= max_dist) & (kpos >= 0)
    s = jnp.where(valid, s, -jnp.inf)
    m = jnp.max(s, axis=-1, keepdims=True)
    p = jnp.exp(s - m)
    den = jnp.sum(p, axis=-1)
    o = jnp.einsum('nrbqk,nbkd->nrbqd', p, vb) / den[..., None]
    lse = m[..., 0] + jnp.log(den)
    return o.reshape(n, r, l, dh), lse.reshape(n, r, l)


def dilated_window_attention(q, k, v):
    b, s, h, dh = q.shape
    outs, lses = [], []
    for window, dil in A_PATTERNS:
        l = s // dil
        def to_sub(t):
            return t.reshape(b, l, dil, h, dh).transpose(0, 2, 3, 1, 4).reshape(b * dil * h, l, dh)
        o, lse = banded_window_attention(to_sub(q)[:, None], to_sub(k), to_sub(v), window // dil)
        outs.append(o.reshape(b, dil, h, l, dh).transpose(0, 3, 1, 2, 4).reshape(b, s, h, dh))
        lses.append(lse.reshape(b, dil, h, l).transpose(0, 3, 1, 2).reshape(b, s, h))
    w = jax.nn.softmax(jnp.stack(lses, axis=0), axis=0)
    return jnp.einsum('pbsh,pbshd->bshd', w, jnp.stack(outs, axis=0))


def rg_lru(xb, conv_w, conv_b, ga_w, ga_b, gx_w, gx_b, lam):
    b, s, _ = xb.shape
    xc = causal_dwconv(xb.astype(F32), conv_w.astype(F32), conv_b.astype(F32))
    xh = xc.reshape(b, s, B_BLOCKS, B_BLOCK_DIM)
    r = jax.nn.sigmoid(jnp.einsum('bshi,hij->bshj', xh, ga_w.astype(F32)).reshape(b, s, B_WIDTH) + ga_b.astype(F32))
    i = jax.nn.sigmoid(jnp.einsum('bshi,hij->bshj', xh, gx_w.astype(F32)).reshape(b, s, B_WIDTH) + gx_b.astype(F32))
    log_a = -LRU_C * r * jax.nn.softplus(-lam.astype(F32))
    a = jnp.exp(log_a)
    mult = jnp.sqrt(-jnp.expm1(2.0 * log_a))
    return linear_scan(a, mult * (i * xc))


def sink_window_attention(q, k, v, sinks):
    b, s, _, dh = q.shape
    qg = q.reshape(b, s, C_KV_HEADS, C_GROUP, dh).transpose(0, 2, 3, 1, 4).reshape(b * C_KV_HEADS, C_GROUP, s, dh)
    kg = k.transpose(0, 2, 1, 3).reshape(b * C_KV_HEADS, s, dh)
    vg = v.transpose(0, 2, 1, 3).reshape(b * C_KV_HEADS, s, dh)
    o, lse = banded_window_attention(qg, kg, vg, C_WINDOW - 1)
    sink = sinks.astype(F32).reshape(1, C_KV_HEADS, C_GROUP, 1)
    keep = jax.nn.sigmoid(lse.reshape(b, C_KV_HEADS, C_GROUP, s) - sink)
    o = o.reshape(b, C_KV_HEADS, C_GROUP, s, dh) * keep[..., None]
    return o.transpose(0, 3, 1, 2, 4).reshape(b, s, C_WIDTH)


def s5_ssm(u, a_re, a_im, b_re, b_im, c_re, c_im, d_skip, log_dt, glu_w, glu_b):
    b, s, _ = u.shape
    u32 = u.astype(F32).reshape(b, s, D_GROUPS, D_GROUP_DIM)
    lam = lax.complex(a_re.astype(F32), a_im.astype(F32))
    dt = jnp.exp(log_dt.astype(F32))[:, None]
    a_bar = jnp.exp(lam * dt)
    b_mat = lax.complex(b_re.astype(F32), b_im.astype(F32))
    b_bar = ((a_bar - 1.0) / lam)[..., None] * b_mat
    bu = jnp.einsum('bsgc,gpc->bsgp', u32.astype(jnp.complex64), b_bar)
    state = linear_scan(jnp.broadcast_to(a_bar, bu.shape), bu)
    c_mat = lax.complex(c_re.astype(F32), c_im.astype(F32))
    y = jnp.einsum('bsgp,gcp->bsgc', state, c_mat).real + d_skip.astype(F32).reshape(D_GROUPS, D_GROUP_DIM) * u32
    z = jax.nn.gelu(y.reshape(b, s, D_WIDTH))
    return z * jax.nn.sigmoid(z @ glu_w.astype(F32) + glu_b.astype(F32))


def even_mixer(h, positions, w_in, conv_w, conv_b, ga_w, ga_b, gx_w, gx_b, lam, w_out):
    b, s, _ = h.shape
    proj = h @ w_in
    q, k, v, xb, yb = jnp.split(proj, [A_WIDTH, 2 * A_WIDTH, 3 * A_WIDTH, 3 * A_WIDTH + B_WIDTH], axis=-1)
    q = rope(q.reshape(b, s, A_HEADS, A_HEAD_DIM), positions)
    k = rope(k.reshape(b, s, A_HEADS, A_HEAD_DIM), positions)
    attn = dilated_window_attention(q, k, v.reshape(b, s, A_HEADS, A_HEAD_DIM)).reshape(b, s, A_WIDTH)
    lru = rg_lru(xb, conv_w, conv_b, ga_w, ga_b, gx_w, gx_b, lam) * jax.nn.gelu(yb.astype(F32))
    return jnp.concatenate([attn, lru], axis=-1).astype(h.dtype) @ w_out


def odd_mixer(h, positions, w_in, sinks, a_re, a_im, b_re, b_im, c_re, c_im, d_skip, log_dt, glu_w, glu_b, w_out):
    b, s, _ = h.shape
    proj = h @ w_in
    q, k, v, u = jnp.split(proj, [C_WIDTH, C_WIDTH + C_KV_WIDTH, C_WIDTH + 2 * C_KV_WIDTH], axis=-1)
    q = rope(q.reshape(b, s, C_HEADS, C_HEAD_DIM), positions)
    k = rope(k.reshape(b, s, C_KV_HEADS, C_HEAD_DIM), positions)
    attn = sink_window_attention(q, k, v.reshape(b, s, C_KV_HEADS, C_HEAD_DIM), sinks)
    ssm = s5_ssm(u, a_re, a_im, b_re, b_im, c_re, c_im, d_skip, log_dt, glu_w, glu_b)
    return jnp.concatenate([attn, ssm], axis=-1).astype(h.dtype) @ w_out


def conv_ffn(h, w_in, conv_w, conv_b, w_out):
    u = causal_dwconv((h @ w_in).astype(F32), conv_w.astype(F32), conv_b.astype(F32))
    g, v = jnp.split(u, 2, axis=-1)
    return (jax.nn.gelu(g) * v).astype(h.dtype) @ w_out


def _fwd_setup_inputs(seed: int = 0) -> dict:
    key = jax.random.key(seed)
    ks = iter(jax.random.split(key, 48))

    def nrm(shape, scale):
        return scale * jax.random.normal(next(ks), shape, F32)

    def unif(shape, lo, hi):
        return jax.random.uniform(next(ks), shape, F32, lo, hi)

    x = nrm((BATCH, SEQ, D_MODEL), 1.0)
    c = nrm((BATCH, D_MODEL), 1.0)
    positions = (jax.random.randint(next(ks), (BATCH, 1), 0, 1024, dtype=jnp.int32)
                 + jnp.arange(SEQ, dtype=jnp.int32)[None, :])
    gate_offset = jnp.repeat(jnp.array([0.0, 0.0, 1.0, 0.0, 0.0, 1.0], F32), D_MODEL)
    ada_w = nrm((DEPTH, D_MODEL, 6 * D_MODEL), 0.1 * D_MODEL ** -0.5)
    ada_b = nrm((DEPTH, 6 * D_MODEL), 0.02) + gate_offset
    norm_mix = 1.0 + nrm((DEPTH, D_MODEL), 0.05)
    norm_ffn = 1.0 + nrm((DEPTH, D_MODEL), 0.05)
    norm_final = 1.0 + nrm((D_MODEL,), 0.05)

    ev_w_in = nrm((N_EVEN, D_MODEL, EVEN_IN), D_MODEL ** -0.5)
    ev_conv_w = nrm((N_EVEN, B_CONV, B_WIDTH), B_CONV ** -0.5)
    ev_conv_b = nrm((N_EVEN, B_WIDTH), 0.02)
    ev_gate_a_w = nrm((N_EVEN, B_BLOCKS, B_BLOCK_DIM, B_BLOCK_DIM), B_BLOCK_DIM ** -0.5)
    ev_gate_a_b = nrm((N_EVEN, B_WIDTH), 0.02)
    ev_gate_x_w = nrm((N_EVEN, B_BLOCKS, B_BLOCK_DIM, B_BLOCK_DIM), B_BLOCK_DIM ** -0.5)
    ev_gate_x_b = nrm((N_EVEN, B_WIDTH), 0.02)
    a_pow_c = unif((N_EVEN, B_WIDTH), 0.9, 0.999)
    a_base = a_pow_c ** (1.0 / LRU_C)
    ev_lambda = jnp.log(a_base) - jnp.log1p(-a_base)
    ev_w_out = nrm((N_EVEN, EVEN_MIX, D_MODEL), EVEN_MIX ** -0.5)

    od_w_in = nrm((N_ODD, D_MODEL, ODD_IN), D_MODEL ** -0.5)
    od_sinks = 3.0 + nrm((N_ODD, C_HEADS), 1.0)
    od_a_re = -0.5 + nrm((N_ODD, D_GROUPS, D_STATE), 0.01)
    od_a_im = math.pi * jnp.arange(D_STATE, dtype=F32) + nrm((N_ODD, D_GROUPS, D_STATE), 0.01)
    od_b_re = nrm((N_ODD, D_GROUPS, D_STATE, D_GROUP_DIM), (2.0 * D_GROUP_DIM) ** -0.5)
    od_b_im = nrm((N_ODD, D_GROUPS, D_STATE, D_GROUP_DIM), (2.0 * D_GROUP_DIM) ** -0.5)
    od_c_re = nrm((N_ODD, D_GROUPS, D_GROUP_DIM, D_STATE), (2.0 * D_STATE) ** -0.5)
    od_c_im = nrm((N_ODD, D_GROUPS, D_GROUP_DIM, D_STATE), (2.0 * D_STATE) ** -0.5)
    od_d = nrm((N_ODD, D_WIDTH), 0.5)
    od_log_dt = unif((N_ODD, D_GROUPS), math.log(1e-3), math.log(1e-1))
    od_glu_w = nrm((N_ODD, D_WIDTH, D_WIDTH), D_WIDTH ** -0.5)
    od_glu_b = nrm((N_ODD, D_WIDTH), 0.02)
    od_w_out = nrm((N_ODD, ODD_MIX, D_MODEL), ODD_MIX ** -0.5)

    ffn_w_in = nrm((DEPTH, D_MODEL, 2 * D_FF), D_MODEL ** -0.5)
    ffn_conv_w = nrm((DEPTH, FFN_CONV, 2 * D_FF), FFN_CONV ** -0.5)
    ffn_conv_b = nrm((DEPTH, 2 * D_FF), 0.02)
    ffn_w_out = nrm((DEPTH, D_FF, D_MODEL), D_FF ** -0.5)

    return {'x': x, 'c': c, 'positions': positions,
            'ada_w': ada_w, 'ada_b': ada_b, 'norm_mix': norm_mix, 'norm_ffn': norm_ffn, 'norm_final': norm_final,
            'ev_w_in': ev_w_in, 'ev_conv_w': ev_conv_w, 'ev_conv_b': ev_conv_b,
            'ev_gate_a_w': ev_gate_a_w, 'ev_gate_a_b': ev_gate_a_b, 'ev_gate_x_w': ev_gate_x_w, 'ev_gate_x_b': ev_gate_x_b,
            'ev_lambda': ev_lambda, 'ev_w_out': ev_w_out,
            'od_w_in': od_w_in, 'od_sinks': od_sinks, 'od_a_re': od_a_re, 'od_a_im': od_a_im,
            'od_b_re': od_b_re, 'od_b_im': od_b_im, 'od_c_re': od_c_re, 'od_c_im': od_c_im,
            'od_d': od_d, 'od_log_dt': od_log_dt, 'od_glu_w': od_glu_w, 'od_glu_b': od_glu_b, 'od_w_out': od_w_out,
            'ffn_w_in': ffn_w_in, 'ffn_conv_w': ffn_conv_w, 'ffn_conv_b': ffn_conv_b, 'ffn_w_out': ffn_w_out}


def _fwd_reference(x, c, positions, ada_w, ada_b, norm_mix, norm_ffn, norm_final,
              ev_w_in, ev_conv_w, ev_conv_b, ev_gate_a_w, ev_gate_a_b, ev_gate_x_w, ev_gate_x_b, ev_lambda, ev_w_out,
              od_w_in, od_sinks, od_a_re, od_a_im, od_b_re, od_b_im, od_c_re, od_c_im, od_d, od_log_dt,
              od_glu_w, od_glu_b, od_w_out,
              ffn_w_in, ffn_conv_w, ffn_conv_b, ffn_w_out):
    cond = jax.nn.silu(c.astype(F32))
    for layer in range(DEPTH):
        mod = cond @ ada_w[layer].astype(F32) + ada_b[layer].astype(F32)
        sh1, sc1, g1, sh2, sc2, g2 = jnp.split(mod, 6, axis=-1)
        h = modulate(rmsnorm(x, norm_mix[layer]), sh1, sc1)
        if layer % 2 == 0:
            e = layer // 2
            y = even_mixer(h, positions, ev_w_in[e], ev_conv_w[e], ev_conv_b[e], ev_gate_a_w[e], ev_gate_a_b[e],
                           ev_gate_x_w[e], ev_gate_x_b[e], ev_lambda[e], ev_w_out[e])
        else:
            o = layer // 2
            y = odd_mixer(h, positions, od_w_in[o], od_sinks[o], od_a_re[o], od_a_im[o], od_b_re[o], od_b_im[o],
                          od_c_re[o], od_c_im[o], od_d[o], od_log_dt[o], od_glu_w[o], od_glu_b[o], od_w_out[o])
        x = x + (g1[:, None] * y.astype(F32)).astype(x.dtype)
        h = modulate(rmsnorm(x, norm_ffn[layer]), sh2, sc2)
        f = conv_ffn(h, ffn_w_in[layer], ffn_conv_w[layer], ffn_conv_b[layer], ffn_w_out[layer])
        x = x + (g2[:, None] * f.astype(F32)).astype(x.dtype)
    return rmsnorm(x, norm_final)


import jax as _jax
import jax.numpy as _jnp

TWIN_FORMAT = 'train_step'
FWD_PARAMS = ['x', 'c', 'positions', 'ada_w', 'ada_b', 'norm_mix', 'norm_ffn', 'norm_final', 'ev_w_in', 'ev_conv_w', 'ev_conv_b', 'ev_gate_a_w', 'ev_gate_a_b', 'ev_gate_x_w', 'ev_gate_x_b', 'ev_lambda', 'ev_w_out', 'od_w_in', 'od_sinks', 'od_a_re', 'od_a_im', 'od_b_re', 'od_b_im', 'od_c_re', 'od_c_im', 'od_d', 'od_log_dt', 'od_glu_w', 'od_glu_b', 'od_w_out', 'ffn_w_in', 'ffn_conv_w', 'ffn_conv_b', 'ffn_w_out']
TWIN_WEIGHTS = ['ada_w', 'ada_b', 'norm_mix', 'norm_ffn', 'norm_final', 'ev_w_in', 'ev_conv_w', 'ev_conv_b', 'ev_gate_a_w', 'ev_gate_a_b', 'ev_gate_x_w', 'ev_gate_x_b', 'ev_lambda', 'ev_w_out', 'od_w_in', 'od_sinks', 'od_a_re', 'od_a_im', 'od_b_re', 'od_b_im', 'od_c_re', 'od_c_im', 'od_d', 'od_log_dt', 'od_glu_w', 'od_glu_b', 'od_w_out', 'ffn_w_in', 'ffn_conv_w', 'ffn_conv_b', 'ffn_w_out']
TWIN_DIFF_INPUT = 'x'
TWIN_INPUTS = ['x', 'c', 'positions', 'ada_w', 'ada_b', 'norm_mix', 'norm_ffn', 'norm_final', 'ev_w_in', 'ev_conv_w', 'ev_conv_b', 'ev_gate_a_w', 'ev_gate_a_b', 'ev_gate_x_w', 'ev_gate_x_b', 'ev_lambda', 'ev_w_out', 'od_w_in', 'od_sinks', 'od_a_re', 'od_a_im', 'od_b_re', 'od_b_im', 'od_c_re', 'od_c_im', 'od_d', 'od_log_dt', 'od_glu_w', 'od_glu_b', 'od_w_out', 'ffn_w_in', 'ffn_conv_w', 'ffn_conv_b', 'ffn_w_out', 'loss_target', 'm_ada_w', 'm_ada_b', 'm_norm_mix', 'm_norm_ffn', 'm_norm_final', 'm_ev_w_in', 'm_ev_conv_w', 'm_ev_conv_b', 'm_ev_gate_a_w', 'm_ev_gate_a_b', 'm_ev_gate_x_w', 'm_ev_gate_x_b', 'm_ev_lambda', 'm_ev_w_out', 'm_od_w_in', 'm_od_sinks', 'm_od_a_re', 'm_od_a_im', 'm_od_b_re', 'm_od_b_im', 'm_od_c_re', 'm_od_c_im', 'm_od_d', 'm_od_log_dt', 'm_od_glu_w', 'm_od_glu_b', 'm_od_w_out', 'm_ffn_w_in', 'm_ffn_conv_w', 'm_ffn_conv_b', 'm_ffn_w_out', 'v_ada_w', 'v_ada_b', 'v_norm_mix', 'v_norm_ffn', 'v_norm_final', 'v_ev_w_in', 'v_ev_conv_w', 'v_ev_conv_b', 'v_ev_gate_a_w', 'v_ev_gate_a_b', 'v_ev_gate_x_w', 'v_ev_gate_x_b', 'v_ev_lambda', 'v_ev_w_out', 'v_od_w_in', 'v_od_sinks', 'v_od_a_re', 'v_od_a_im', 'v_od_b_re', 'v_od_b_im', 'v_od_c_re', 'v_od_c_im', 'v_od_d', 'v_od_log_dt', 'v_od_glu_w', 'v_od_glu_b', 'v_od_w_out', 'v_ffn_w_in', 'v_ffn_conv_w', 'v_ffn_conv_b', 'v_ffn_w_out']
TWIN_OUTPUTS = ['loss', 'grad_x', 'grad_ada_w', 'grad_ada_b', 'grad_norm_mix', 'grad_norm_ffn', 'grad_norm_final', 'grad_ev_w_in', 'grad_ev_conv_w', 'grad_ev_conv_b', 'grad_ev_gate_a_w', 'grad_ev_gate_a_b', 'grad_ev_gate_x_w', 'grad_ev_gate_x_b', 'grad_ev_lambda', 'grad_ev_w_out', 'grad_od_w_in', 'grad_od_sinks', 'grad_od_a_re', 'grad_od_a_im', 'grad_od_b_re', 'grad_od_b_im', 'grad_od_c_re', 'grad_od_c_im', 'grad_od_d', 'grad_od_log_dt', 'grad_od_glu_w', 'grad_od_glu_b', 'grad_od_w_out', 'grad_ffn_w_in', 'grad_ffn_conv_w', 'grad_ffn_conv_b', 'grad_ffn_w_out', 'delta_ada_w', 'delta_ada_b', 'delta_norm_mix', 'delta_norm_ffn', 'delta_norm_final', 'delta_ev_w_in', 'delta_ev_conv_w', 'delta_ev_conv_b', 'delta_ev_gate_a_w', 'delta_ev_gate_a_b', 'delta_ev_gate_x_w', 'delta_ev_gate_x_b', 'delta_ev_lambda', 'delta_ev_w_out', 'delta_od_w_in', 'delta_od_sinks', 'delta_od_a_re', 'delta_od_a_im', 'delta_od_b_re', 'delta_od_b_im', 'delta_od_c_re', 'delta_od_c_im', 'delta_od_d', 'delta_od_log_dt', 'delta_od_glu_w', 'delta_od_glu_b', 'delta_od_w_out', 'delta_ffn_w_in', 'delta_ffn_conv_w', 'delta_ffn_conv_b', 'delta_ffn_w_out', 'new_m_ada_w', 'new_m_ada_b', 'new_m_norm_mix', 'new_m_norm_ffn', 'new_m_norm_final', 'new_m_ev_w_in', 'new_m_ev_conv_w', 'new_m_ev_conv_b', 'new_m_ev_gate_a_w', 'new_m_ev_gate_a_b', 'new_m_ev_gate_x_w', 'new_m_ev_gate_x_b', 'new_m_ev_lambda', 'new_m_ev_w_out', 'new_m_od_w_in', 'new_m_od_sinks', 'new_m_od_a_re', 'new_m_od_a_im', 'new_m_od_b_re', 'new_m_od_b_im', 'new_m_od_c_re', 'new_m_od_c_im', 'new_m_od_d', 'new_m_od_log_dt', 'new_m_od_glu_w', 'new_m_od_glu_b', 'new_m_od_w_out', 'new_m_ffn_w_in', 'new_m_ffn_conv_w', 'new_m_ffn_conv_b', 'new_m_ffn_w_out', 'new_v_ada_w', 'new_v_ada_b', 'new_v_norm_mix', 'new_v_norm_ffn', 'new_v_norm_final', 'new_v_ev_w_in', 'new_v_ev_conv_w', 'new_v_ev_conv_b', 'new_v_ev_gate_a_w', 'new_v_ev_gate_a_b', 'new_v_ev_gate_x_w', 'new_v_ev_gate_x_b', 'new_v_ev_lambda', 'new_v_ev_w_out', 'new_v_od_w_in', 'new_v_od_sinks', 'new_v_od_a_re', 'new_v_od_a_im', 'new_v_od_b_re', 'new_v_od_b_im', 'new_v_od_c_re', 'new_v_od_c_im', 'new_v_od_d', 'new_v_od_log_dt', 'new_v_od_glu_w', 'new_v_od_glu_b', 'new_v_od_w_out', 'new_v_ffn_w_in', 'new_v_ffn_conv_w', 'new_v_ffn_conv_b', 'new_v_ffn_w_out']
TWIN_LEAF_KINDS = {'loss': 'loss', 'grad_x': 'grad_x', 'grad_ada_w': 'grad_w', 'grad_ada_b': 'grad_w', 'grad_norm_mix': 'grad_w', 'grad_norm_ffn': 'grad_w', 'grad_norm_final': 'grad_w', 'grad_ev_w_in': 'grad_w', 'grad_ev_conv_w': 'grad_w', 'grad_ev_conv_b': 'grad_w', 'grad_ev_gate_a_w': 'grad_w', 'grad_ev_gate_a_b': 'grad_w', 'grad_ev_gate_x_w': 'grad_w', 'grad_ev_gate_x_b': 'grad_w', 'grad_ev_lambda': 'grad_w', 'grad_ev_w_out': 'grad_w', 'grad_od_w_in': 'grad_w', 'grad_od_sinks': 'grad_w', 'grad_od_a_re': 'grad_w', 'grad_od_a_im': 'grad_w', 'grad_od_b_re': 'grad_w', 'grad_od_b_im': 'grad_w', 'grad_od_c_re': 'grad_w', 'grad_od_c_im': 'grad_w', 'grad_od_d': 'grad_w', 'grad_od_log_dt': 'grad_w', 'grad_od_glu_w': 'grad_w', 'grad_od_glu_b': 'grad_w', 'grad_od_w_out': 'grad_w', 'grad_ffn_w_in': 'grad_w', 'grad_ffn_conv_w': 'grad_w', 'grad_ffn_conv_b': 'grad_w', 'grad_ffn_w_out': 'grad_w', 'delta_ada_w': 'delta_w', 'delta_ada_b': 'delta_w', 'delta_norm_mix': 'delta_w', 'delta_norm_ffn': 'delta_w', 'delta_norm_final': 'delta_w', 'delta_ev_w_in': 'delta_w', 'delta_ev_conv_w': 'delta_w', 'delta_ev_conv_b': 'delta_w', 'delta_ev_gate_a_w': 'delta_w', 'delta_ev_gate_a_b': 'delta_w', 'delta_ev_gate_x_w': 'delta_w', 'delta_ev_gate_x_b': 'delta_w', 'delta_ev_lambda': 'delta_w', 'delta_ev_w_out': 'delta_w', 'delta_od_w_in': 'delta_w', 'delta_od_sinks': 'delta_w', 'delta_od_a_re': 'delta_w', 'delta_od_a_im': 'delta_w', 'delta_od_b_re': 'delta_w', 'delta_od_b_im': 'delta_w', 'delta_od_c_re': 'delta_w', 'delta_od_c_im': 'delta_w', 'delta_od_d': 'delta_w', 'delta_od_log_dt': 'delta_w', 'delta_od_glu_w': 'delta_w', 'delta_od_glu_b': 'delta_w', 'delta_od_w_out': 'delta_w', 'delta_ffn_w_in': 'delta_w', 'delta_ffn_conv_w': 'delta_w', 'delta_ffn_conv_b': 'delta_w', 'delta_ffn_w_out': 'delta_w', 'new_m_ada_w': 'new_m', 'new_m_ada_b': 'new_m', 'new_m_norm_mix': 'new_m', 'new_m_norm_ffn': 'new_m', 'new_m_norm_final': 'new_m', 'new_m_ev_w_in': 'new_m', 'new_m_ev_conv_w': 'new_m', 'new_m_ev_conv_b': 'new_m', 'new_m_ev_gate_a_w': 'new_m', 'new_m_ev_gate_a_b': 'new_m', 'new_m_ev_gate_x_w': 'new_m', 'new_m_ev_gate_x_b': 'new_m', 'new_m_ev_lambda': 'new_m', 'new_m_ev_w_out': 'new_m', 'new_m_od_w_in': 'new_m', 'new_m_od_sinks': 'new_m', 'new_m_od_a_re': 'new_m', 'new_m_od_a_im': 'new_m', 'new_m_od_b_re': 'new_m', 'new_m_od_b_im': 'new_m', 'new_m_od_c_re': 'new_m', 'new_m_od_c_im': 'new_m', 'new_m_od_d': 'new_m', 'new_m_od_log_dt': 'new_m', 'new_m_od_glu_w': 'new_m', 'new_m_od_glu_b': 'new_m', 'new_m_od_w_out': 'new_m', 'new_m_ffn_w_in': 'new_m', 'new_m_ffn_conv_w': 'new_m', 'new_m_ffn_conv_b': 'new_m', 'new_m_ffn_w_out': 'new_m', 'new_v_ada_w': 'new_v', 'new_v_ada_b': 'new_v', 'new_v_norm_mix': 'new_v', 'new_v_norm_ffn': 'new_v', 'new_v_norm_final': 'new_v', 'new_v_ev_w_in': 'new_v', 'new_v_ev_conv_w': 'new_v', 'new_v_ev_conv_b': 'new_v', 'new_v_ev_gate_a_w': 'new_v', 'new_v_ev_gate_a_b': 'new_v', 'new_v_ev_gate_x_w': 'new_v', 'new_v_ev_gate_x_b': 'new_v', 'new_v_ev_lambda': 'new_v', 'new_v_ev_w_out': 'new_v', 'new_v_od_w_in': 'new_v', 'new_v_od_sinks': 'new_v', 'new_v_od_a_re': 'new_v', 'new_v_od_a_im': 'new_v', 'new_v_od_b_re': 'new_v', 'new_v_od_b_im': 'new_v', 'new_v_od_c_re': 'new_v', 'new_v_od_c_im': 'new_v', 'new_v_od_d': 'new_v', 'new_v_od_log_dt': 'new_v', 'new_v_od_glu_w': 'new_v', 'new_v_od_glu_b': 'new_v', 'new_v_od_w_out': 'new_v', 'new_v_ffn_w_in': 'new_v', 'new_v_ffn_conv_w': 'new_v', 'new_v_ffn_conv_b': 'new_v', 'new_v_ffn_w_out': 'new_v'}


def _forward(args):
    return _fwd_reference(*[args[k] for k in FWD_PARAMS])


def _output_shape():
    out = _jax.eval_shape(lambda: _forward(_fwd_setup_inputs(0)))
    return out.shape, out.dtype

N_MICROBATCH = 1
ADAM_LR = 0.001
ADAM_B1 = 0.9
ADAM_B2 = 0.999
ADAM_EPS = 1e-08
ADAM_WD = 0.01
ADAM_STEP = 10
PER_EXAMPLE_BATCH_AXIS = {'x': 0, 'c': 0, 'positions': 0, 'loss_target': 0}
SHARED_INPUTS = []
_WEIGHT_DTYPES = {'ada_w': _jnp.float32, 'ada_b': _jnp.float32, 'norm_mix': _jnp.float32, 'norm_ffn': _jnp.float32, 'norm_final': _jnp.float32, 'ev_w_in': _jnp.float32, 'ev_conv_w': _jnp.float32, 'ev_conv_b': _jnp.float32, 'ev_gate_a_w': _jnp.float32, 'ev_gate_a_b': _jnp.float32, 'ev_gate_x_w': _jnp.float32, 'ev_gate_x_b': _jnp.float32, 'ev_lambda': _jnp.float32, 'ev_w_out': _jnp.float32, 'od_w_in': _jnp.float32, 'od_sinks': _jnp.float32, 'od_a_re': _jnp.float32, 'od_a_im': _jnp.float32, 'od_b_re': _jnp.float32, 'od_b_im': _jnp.float32, 'od_c_re': _jnp.float32, 'od_c_im': _jnp.float32, 'od_d': _jnp.float32, 'od_log_dt': _jnp.float32, 'od_glu_w': _jnp.float32, 'od_glu_b': _jnp.float32, 'od_w_out': _jnp.float32, 'ffn_w_in': _jnp.float32, 'ffn_conv_w': _jnp.float32, 'ffn_conv_b': _jnp.float32, 'ffn_w_out': _jnp.float32}
MOMENT_SCALE = {'ada_w': 5.634696e-02, 'ada_b': 1.035340e-01, 'norm_mix': 2.911300e-02, 'norm_ffn': 4.849622e-02, 'norm_final': 8.006866e+00, 'ev_w_in': 2.645107e-02, 'ev_conv_w': 4.297637e-02, 'ev_conv_b': 4.020632e-01, 'ev_gate_a_w': 1.172480e-02, 'ev_gate_a_b': 1.096127e-02, 'ev_gate_x_w': 2.090341e-02, 'ev_gate_x_b': 1.523426e-02, 'ev_lambda': 2.293414e-02, 'ev_w_out': 3.110427e-02, 'od_w_in': 1.243020e-02, 'od_sinks': 3.059389e-02, 'od_a_re': 9.801774e-04, 'od_a_im': 9.594592e-04, 'od_b_re': 6.248059e-04, 'od_b_im': 6.314498e-04, 'od_c_re': 1.255591e-03, 'od_c_im': 1.270426e-03, 'od_d': 2.269501e-02, 'od_log_dt': 5.626653e-01, 'od_glu_w': 1.436796e-03, 'od_glu_b': 5.364922e-03, 'od_w_out': 1.223899e-02, 'ffn_w_in': 2.095455e-02, 'ffn_conv_w': 2.102170e-02, 'ffn_conv_b': 2.406348e-02, 'ffn_w_out': 3.390977e-02}


def _to_microbatches(a, axis):
    t = _jnp.moveaxis(a, axis, 0)
    t = t.reshape((N_MICROBATCH, t.shape[0] // N_MICROBATCH) + t.shape[1:])
    return _jnp.moveaxis(t, 1, axis + 1)


def setup_inputs(seed: int = 0) -> dict:
    inp = _fwd_setup_inputs(seed)
    key = _jax.random.fold_in(_jax.random.key(seed), 7919)
    shape, _ = _output_shape()
    out = dict(inp)
    out["loss_target"] = _jax.random.normal(_jax.random.fold_in(key, 0), shape, _jnp.float32)
    for i, name in enumerate(TWIN_WEIGHTS):
        w = inp[name].astype(_jnp.float32)
        if MOMENT_SCALE is None:
            s = _jnp.sqrt(_jnp.mean(_jnp.square(w)) + 1e-30)
        else:
            s = MOMENT_SCALE[name]
        km, kv = _jax.random.split(_jax.random.fold_in(key, i + 1))
        out[name] = w
        out["m_" + name] = s * _jax.random.normal(km, w.shape, _jnp.float32)
        out["v_" + name] = (s * s) * _jax.random.uniform(kv, w.shape, _jnp.float32, 0.5, 1.5)
    if N_MICROBATCH > 1:
        for name, axis in PER_EXAMPLE_BATCH_AXIS.items():
            out[name] = _to_microbatches(out[name], axis)
    return {'x': out['x'], 'c': out['c'], 'positions': out['positions'], 'ada_w': out['ada_w'], 'ada_b': out['ada_b'], 'norm_mix': out['norm_mix'], 'norm_ffn': out['norm_ffn'], 'norm_final': out['norm_final'], 'ev_w_in': out['ev_w_in'], 'ev_conv_w': out['ev_conv_w'], 'ev_conv_b': out['ev_conv_b'], 'ev_gate_a_w': out['ev_gate_a_w'], 'ev_gate_a_b': out['ev_gate_a_b'], 'ev_gate_x_w': out['ev_gate_x_w'], 'ev_gate_x_b': out['ev_gate_x_b'], 'ev_lambda': out['ev_lambda'], 'ev_w_out': out['ev_w_out'], 'od_w_in': out['od_w_in'], 'od_sinks': out['od_sinks'], 'od_a_re': out['od_a_re'], 'od_a_im': out['od_a_im'], 'od_b_re': out['od_b_re'], 'od_b_im': out['od_b_im'], 'od_c_re': out['od_c_re'], 'od_c_im': out['od_c_im'], 'od_d': out['od_d'], 'od_log_dt': out['od_log_dt'], 'od_glu_w': out['od_glu_w'], 'od_glu_b': out['od_glu_b'], 'od_w_out': out['od_w_out'], 'ffn_w_in': out['ffn_w_in'], 'ffn_conv_w': out['ffn_conv_w'], 'ffn_conv_b': out['ffn_conv_b'], 'ffn_w_out': out['ffn_w_out'], 'loss_target': out['loss_target'], 'm_ada_w': out['m_ada_w'], 'm_ada_b': out['m_ada_b'], 'm_norm_mix': out['m_norm_mix'], 'm_norm_ffn': out['m_norm_ffn'], 'm_norm_final': out['m_norm_final'], 'm_ev_w_in': out['m_ev_w_in'], 'm_ev_conv_w': out['m_ev_conv_w'], 'm_ev_conv_b': out['m_ev_conv_b'], 'm_ev_gate_a_w': out['m_ev_gate_a_w'], 'm_ev_gate_a_b': out['m_ev_gate_a_b'], 'm_ev_gate_x_w': out['m_ev_gate_x_w'], 'm_ev_gate_x_b': out['m_ev_gate_x_b'], 'm_ev_lambda': out['m_ev_lambda'], 'm_ev_w_out': out['m_ev_w_out'], 'm_od_w_in': out['m_od_w_in'], 'm_od_sinks': out['m_od_sinks'], 'm_od_a_re': out['m_od_a_re'], 'm_od_a_im': out['m_od_a_im'], 'm_od_b_re': out['m_od_b_re'], 'm_od_b_im': out['m_od_b_im'], 'm_od_c_re': out['m_od_c_re'], 'm_od_c_im': out['m_od_c_im'], 'm_od_d': out['m_od_d'], 'm_od_log_dt': out['m_od_log_dt'], 'm_od_glu_w': out['m_od_glu_w'], 'm_od_glu_b': out['m_od_glu_b'], 'm_od_w_out': out['m_od_w_out'], 'm_ffn_w_in': out['m_ffn_w_in'], 'm_ffn_conv_w': out['m_ffn_conv_w'], 'm_ffn_conv_b': out['m_ffn_conv_b'], 'm_ffn_w_out': out['m_ffn_w_out'], 'v_ada_w': out['v_ada_w'], 'v_ada_b': out['v_ada_b'], 'v_norm_mix': out['v_norm_mix'], 'v_norm_ffn': out['v_norm_ffn'], 'v_norm_final': out['v_norm_final'], 'v_ev_w_in': out['v_ev_w_in'], 'v_ev_conv_w': out['v_ev_conv_w'], 'v_ev_conv_b': out['v_ev_conv_b'], 'v_ev_gate_a_w': out['v_ev_gate_a_w'], 'v_ev_gate_a_b': out['v_ev_gate_a_b'], 'v_ev_gate_x_w': out['v_ev_gate_x_w'], 'v_ev_gate_x_b': out['v_ev_gate_x_b'], 'v_ev_lambda': out['v_ev_lambda'], 'v_ev_w_out': out['v_ev_w_out'], 'v_od_w_in': out['v_od_w_in'], 'v_od_sinks': out['v_od_sinks'], 'v_od_a_re': out['v_od_a_re'], 'v_od_a_im': out['v_od_a_im'], 'v_od_b_re': out['v_od_b_re'], 'v_od_b_im': out['v_od_b_im'], 'v_od_c_re': out['v_od_c_re'], 'v_od_c_im': out['v_od_c_im'], 'v_od_d': out['v_od_d'], 'v_od_log_dt': out['v_od_log_dt'], 'v_od_glu_w': out['v_od_glu_w'], 'v_od_glu_b': out['v_od_glu_b'], 'v_od_w_out': out['v_od_w_out'], 'v_ffn_w_in': out['v_ffn_w_in'], 'v_ffn_conv_w': out['v_ffn_conv_w'], 'v_ffn_conv_b': out['v_ffn_conv_b'], 'v_ffn_w_out': out['v_ffn_w_out']}


def _loss(weights, diff, rest, loss_target):
    with _jax.named_scope("forward"):
        args = {**rest, TWIN_DIFF_INPUT: diff, **{k: w.astype(_WEIGHT_DTYPES[k]) for k, w in weights.items()}}
        y = _forward(args)
    with _jax.named_scope("loss_head"):
        err = _jnp.square(y.astype(_jnp.float32) - loss_target)
        return 0.5 * _jnp.sum(_jnp.mean(err, axis=-1)) if err.ndim else 0.5 * err


def _adamw(w, g, m, v):
    m = ADAM_B1 * m + (1.0 - ADAM_B1) * g
    v = ADAM_B2 * v + (1.0 - ADAM_B2) * _jnp.square(g)
    m_hat = m / (1.0 - ADAM_B1 ** ADAM_STEP)
    v_hat = v / (1.0 - ADAM_B2 ** ADAM_STEP)
    delta = -ADAM_LR * (m_hat / (_jnp.sqrt(v_hat) + ADAM_EPS) + ADAM_WD * w)
    return delta, m, v


def reference(x, c, positions, ada_w, ada_b, norm_mix, norm_ffn, norm_final, ev_w_in, ev_conv_w, ev_conv_b, ev_gate_a_w, ev_gate_a_b, ev_gate_x_w, ev_gate_x_b, ev_lambda, ev_w_out, od_w_in, od_sinks, od_a_re, od_a_im, od_b_re, od_b_im, od_c_re, od_c_im, od_d, od_log_dt, od_glu_w, od_glu_b, od_w_out, ffn_w_in, ffn_conv_w, ffn_conv_b, ffn_w_out, loss_target, m_ada_w, m_ada_b, m_norm_mix, m_norm_ffn, m_norm_final, m_ev_w_in, m_ev_conv_w, m_ev_conv_b, m_ev_gate_a_w, m_ev_gate_a_b, m_ev_gate_x_w, m_ev_gate_x_b, m_ev_lambda, m_ev_w_out, m_od_w_in, m_od_sinks, m_od_a_re, m_od_a_im, m_od_b_re, m_od_b_im, m_od_c_re, m_od_c_im, m_od_d, m_od_log_dt, m_od_glu_w, m_od_glu_b, m_od_w_out, m_ffn_w_in, m_ffn_conv_w, m_ffn_conv_b, m_ffn_w_out, v_ada_w, v_ada_b, v_norm_mix, v_norm_ffn, v_norm_final, v_ev_w_in, v_ev_conv_w, v_ev_conv_b, v_ev_gate_a_w, v_ev_gate_a_b, v_ev_gate_x_w, v_ev_gate_x_b, v_ev_lambda, v_ev_w_out, v_od_w_in, v_od_sinks, v_od_a_re, v_od_a_im, v_od_b_re, v_od_b_im, v_od_c_re, v_od_c_im, v_od_d, v_od_log_dt, v_od_glu_w, v_od_glu_b, v_od_w_out, v_ffn_w_in, v_ffn_conv_w, v_ffn_conv_b, v_ffn_w_out):
    given = dict(x=x, c=c, positions=positions, ada_w=ada_w, ada_b=ada_b, norm_mix=norm_mix, norm_ffn=norm_ffn, norm_final=norm_final, ev_w_in=ev_w_in, ev_conv_w=ev_conv_w, ev_conv_b=ev_conv_b, ev_gate_a_w=ev_gate_a_w, ev_gate_a_b=ev_gate_a_b, ev_gate_x_w=ev_gate_x_w, ev_gate_x_b=ev_gate_x_b, ev_lambda=ev_lambda, ev_w_out=ev_w_out, od_w_in=od_w_in, od_sinks=od_sinks, od_a_re=od_a_re, od_a_im=od_a_im, od_b_re=od_b_re, od_b_im=od_b_im, od_c_re=od_c_re, od_c_im=od_c_im, od_d=od_d, od_log_dt=od_log_dt, od_glu_w=od_glu_w, od_glu_b=od_glu_b, od_w_out=od_w_out, ffn_w_in=ffn_w_in, ffn_conv_w=ffn_conv_w, ffn_conv_b=ffn_conv_b, ffn_w_out=ffn_w_out, loss_target=loss_target, m_ada_w=m_ada_w, m_ada_b=m_ada_b, m_norm_mix=m_norm_mix, m_norm_ffn=m_norm_ffn, m_norm_final=m_norm_final, m_ev_w_in=m_ev_w_in, m_ev_conv_w=m_ev_conv_w, m_ev_conv_b=m_ev_conv_b, m_ev_gate_a_w=m_ev_gate_a_w, m_ev_gate_a_b=m_ev_gate_a_b, m_ev_gate_x_w=m_ev_gate_x_w, m_ev_gate_x_b=m_ev_gate_x_b, m_ev_lambda=m_ev_lambda, m_ev_w_out=m_ev_w_out, m_od_w_in=m_od_w_in, m_od_sinks=m_od_sinks, m_od_a_re=m_od_a_re, m_od_a_im=m_od_a_im, m_od_b_re=m_od_b_re, m_od_b_im=m_od_b_im, m_od_c_re=m_od_c_re, m_od_c_im=m_od_c_im, m_od_d=m_od_d, m_od_log_dt=m_od_log_dt, m_od_glu_w=m_od_glu_w, m_od_glu_b=m_od_glu_b, m_od_w_out=m_od_w_out, m_ffn_w_in=m_ffn_w_in, m_ffn_conv_w=m_ffn_conv_w, m_ffn_conv_b=m_ffn_conv_b, m_ffn_w_out=m_ffn_w_out, v_ada_w=v_ada_w, v_ada_b=v_ada_b, v_norm_mix=v_norm_mix, v_norm_ffn=v_norm_ffn, v_norm_final=v_norm_final, v_ev_w_in=v_ev_w_in, v_ev_conv_w=v_ev_conv_w, v_ev_conv_b=v_ev_conv_b, v_ev_gate_a_w=v_ev_gate_a_w, v_ev_gate_a_b=v_ev_gate_a_b, v_ev_gate_x_w=v_ev_gate_x_w, v_ev_gate_x_b=v_ev_gate_x_b, v_ev_lambda=v_ev_lambda, v_ev_w_out=v_ev_w_out, v_od_w_in=v_od_w_in, v_od_sinks=v_od_sinks, v_od_a_re=v_od_a_re, v_od_a_im=v_od_a_im, v_od_b_re=v_od_b_re, v_od_b_im=v_od_b_im, v_od_c_re=v_od_c_re, v_od_c_im=v_od_c_im, v_od_d=v_od_d, v_od_log_dt=v_od_log_dt, v_od_glu_w=v_od_glu_w, v_od_glu_b=v_od_glu_b, v_od_w_out=v_od_w_out, v_ffn_w_in=v_ffn_w_in, v_ffn_conv_w=v_ffn_conv_w, v_ffn_conv_b=v_ffn_conv_b, v_ffn_w_out=v_ffn_w_out)
    weights = {n: given[n] for n in TWIN_WEIGHTS}
    shared = {n: given[n] for n in SHARED_INPUTS}
    per_example = {n: given[n] for n in ['x', 'c', 'positions']}
    grad_fn = _jax.value_and_grad(_loss, argnums=(0, 1))

    def one_microbatch(ex, loss_target):
        ex = dict(ex)
        diff = ex.pop(TWIN_DIFF_INPUT)
        return grad_fn(weights, diff, {**shared, **ex}, loss_target)

    if N_MICROBATCH == 1:
        loss, (grad_w, grad_x) = one_microbatch(per_example, given["loss_target"])
    else:
        def body(carry, xs):
            loss_sum, grad_sum = carry
            l_k, (gw_k, gx_k) = one_microbatch(xs[0], xs[1])
            with _jax.named_scope("update"):
                return (loss_sum + l_k, _jax.tree.map(_jnp.add, grad_sum, gw_k)), gx_k

        init = (_jnp.zeros((), _jnp.float32), _jax.tree.map(_jnp.zeros_like, weights))
        (loss, grad_w), grad_x = _jax.lax.scan(body, init, (per_example, given["loss_target"]))
    with _jax.named_scope("update"):
        delta_w, new_m, new_v = {}, {}, {}
        for n in TWIN_WEIGHTS:
            delta_w[n], new_m[n], new_v[n] = _adamw(weights[n], grad_w[n], given["m_" + n], given["v_" + n])
    return (loss, grad_x, *[grad_w[n] for n in TWIN_WEIGHTS], *[delta_w[n] for n in TWIN_WEIGHTS],
            *[new_m[n] for n in TWIN_WEIGHTS], *[new_v[n] for n in TWIN_WEIGHTS])
```

```python
import functools
import math

import jax
import jax.numpy as jnp
from jax import lax
from jax.experimental import pallas as pl
from jax.experimental.pallas import tpu as pltpu

F32 = jnp.float32
BF16 = jnp.bfloat16
MESH = pl.DeviceIdType.MESH

D_MODEL = 2048
SEQ = 2048
DEPTH = 4
N_DEV = 8
N_CHIP = 4
BLK = 128
LANE = 128
V7X_VMEM_LIMIT = 56 * 1024 * 1024
NORM_EPS = 1e-6
ROPE_THETA = 10000.0
LRU_C = 8.0
D_FF = 5504
A_PATTERNS = ((128, 1), (512, 4), (2048, 16))
ADAM_LR, ADAM_B1, ADAM_B2, ADAM_EPS, ADAM_WD, ADAM_STEP = 0.001, 0.9, 0.999, 1e-08, 0.01, 10
NEG = -1e30


def _params(n_grid):
    return pltpu.CompilerParams(dimension_semantics=("arbitrary",) * n_grid, vmem_limit_bytes=V7X_VMEM_LIMIT)


def _pick(dim, pref):
    best = None
    for t in range(LANE, min(dim, pref) + 1, LANE):
        if dim % t == 0:
            best = t
    return best or dim


def _sigmoid(x):
    return 1.0 / (1.0 + jnp.exp(-x))


_GELU_C = math.sqrt(2.0 / math.pi)


def _gelu(x):
    t = jnp.tanh(_GELU_C * (x + 0.044715 * (x * x * x)))
    return 0.5 * x * (1.0 + t)


def _gelu_grad(x):
    t = jnp.tanh(_GELU_C * (x + 0.044715 * (x * x * x)))
    return 0.5 * (1.0 + t) + 0.5 * x * (1.0 - t * t) * (_GELU_C * (1.0 + 3.0 * 0.044715 * (x * x)))


def _softplus(x):
    return jnp.maximum(x, 0.0) + jnp.log(1.0 + jnp.exp(-jnp.abs(x)))


def _neg_expm1(x):
    series = -x * (1.0 + x * (0.5 + x * (1.0 / 6.0 + x * (1.0 / 24.0))))
    return jnp.where(x > -0.03, series, 1.0 - jnp.exp(x))


def _shift_down(x, k):
    if k == 0:
        return x
    row = lax.broadcasted_iota(jnp.int32, x.shape, 0)
    return jnp.where(row >= k, pltpu.roll(x, k, 0), 0.0)


def _shift_up(x, k):
    if k == 0:
        return x
    n = x.shape[0]
    row = lax.broadcasted_iota(jnp.int32, x.shape, 0)
    return jnp.where(row < n - k, pltpu.roll(x, n - k, 0), 0.0)


def _dot(a, b, ca=1, cb=0):
    return lax.dot_general(a.astype(BF16), b.astype(BF16), (((ca,), (cb,)), ((), ())), preferred_element_type=F32)


def _rope(x, cos, sin_signed, half):
    c = x.shape[1]
    reps = c // cos.shape[1]
    cos_c = jnp.tile(cos, (1, reps)) if reps > 1 else cos
    sin_c = jnp.tile(sin_signed, (1, reps)) if reps > 1 else sin_signed
    lane = lax.broadcasted_iota(jnp.int32, x.shape, 1)
    first = (lane % (2 * half)) < half
    partner = jnp.where(first, pltpu.roll(x, c - half, 1), pltpu.roll(x, half, 1))
    return x * cos_c + partner * sin_c


def _rope_t(dy, cos, sin_signed, half):
    c = dy.shape[1]
    reps = c // cos.shape[1]
    cos_c = jnp.tile(cos, (1, reps)) if reps > 1 else cos
    sin_c = jnp.tile(sin_signed, (1, reps)) if reps > 1 else sin_signed
    lane = lax.broadcasted_iota(jnp.int32, dy.shape, 1)
    first = (lane % (2 * half)) < half
    ys = dy * sin_c
    partner = jnp.where(first, pltpu.roll(ys, c - half, 1), pltpu.roll(ys, half, 1))
    return dy * cos_c + partner


def _tile_call(name, fn, grid, ins, outs, acc_axis=None):
    n_in = len(ins)
    accs = [o[4] for o in outs]

    def body(*refs):
        vals = fn(*[r[...] for r in refs[:n_in]])
        if not isinstance(vals, (tuple, list)):
            vals = (vals,)
        for r, v, acc in zip(refs[n_in:], vals, accs):
            if acc:
                first = pl.program_id(acc_axis) == 0

                @pl.when(first)
                def _():
                    r[...] = v.astype(r.dtype)

                @pl.when(jnp.logical_not(first))
                def _():
                    r[...] += v.astype(r.dtype)
            else:
                r[...] = v.astype(r.dtype)

    res = pl.pallas_call(
        body, name=name, grid=grid,
        in_specs=[pl.BlockSpec(b, im) for _, b, im in ins],
        out_specs=[pl.BlockSpec(o[2], o[3]) for o in outs],
        out_shape=[jax.ShapeDtypeStruct(o[0], o[1]) for o in outs],
        compiler_params=_params(len(grid)),
    )(*[a for a, _, _ in ins])
    return res


def _row_tile(cols, n_arrays, rows):
    budget = 24 * 1024 * 1024 // (2 * 4 * max(n_arrays, 1) * cols)
    t = 8
    while t * 2 <= budget and rows % (t * 2) == 0 and t * 2 <= 1024:
        t *= 2
    return t


def _rows_call(name, fn, rows, tiled, full, outs_tiled, outs_acc=()):
    cols = max([a.shape[1] for a in tiled] + [c for c, _ in outs_tiled])
    tt = _row_tile(cols, len(tiled) + len(outs_tiled), rows)
    ins = [(a, (tt, a.shape[1]), lambda i: (i, 0)) for a in tiled]
    ins += [(a, a.shape, (lambda nd: (lambda i: (0,) * nd))(a.ndim)) for a in full]
    outs = [((rows, c), dt, (tt, c), lambda i: (i, 0), False) for c, dt in outs_tiled]
    outs += [(s, dt, s, (lambda nd: (lambda i: (0,) * nd))(len(s)), True) for s, dt in outs_acc]
    return _tile_call(name, fn, (rows // tt,), ins, outs, acc_axis=0)


def mm(a, b, *, ta=False, tb=False, out_dtype=F32, tm=None, tn=None, tk=None, name):
    m, k = (a.shape[1], a.shape[0]) if ta else a.shape
    n = b.shape[0] if tb else b.shape[1]
    tm = tm or _pick(m, 1024)
    tn = tn or _pick(n, 1024)
    tk = tk or k
    assert m % tm == 0 and n % tn == 0 and k % tk == 0, (name, m, n, k, tm, tn, tk)
    nk = k // tk
    a_spec = pl.BlockSpec((tk, tm), lambda i, j, kk: (kk, i)) if ta else pl.BlockSpec((tm, tk), lambda i, j, kk: (i, kk))
    b_spec = pl.BlockSpec((tn, tk), lambda i, j, kk: (j, kk)) if tb else pl.BlockSpec((tk, tn), lambda i, j, kk: (kk, j))
    ca, cb = (0 if ta else 1), (1 if tb else 0)

    def body(a_ref, b_ref, o_ref, *scratch):
        p = _dot(a_ref[...], b_ref[...], ca, cb)
        if nk == 1:
            o_ref[...] = p.astype(o_ref.dtype)
        else:
            acc = scratch[0]
            kk = pl.program_id(2)

            @pl.when(kk == 0)
            def _():
                acc[...] = p

            @pl.when(kk > 0)
            def _():
                acc[...] += p

            @pl.when(kk == nk - 1)
            def _():
                o_ref[...] = acc[...].astype(o_ref.dtype)

    return pl.pallas_call(
        body, name=name, grid=(m // tm, n // tn, nk),
        in_specs=[a_spec, b_spec],
        out_specs=pl.BlockSpec((tm, tn), lambda i, j, kk: (i, j)),
        out_shape=jax.ShapeDtypeStruct((m, n), out_dtype),
        scratch_shapes=[pltpu.VMEM((tm, tn), F32)] if nk > 1 else [],
        compiler_params=_params(3),
    )(a, b)


def _rstd(x):
    return lax.rsqrt(jnp.mean(x * x, axis=-1, keepdims=True) + NORM_EPS)


def norm_mod(x, gamma, sc, sh, name):
    def fn(x, gamma, sc, sh):
        return (x * _rstd(x)) * gamma * (1.0 + sc) + sh
    return _rows_call(name, fn, x.shape[0], [x], [gamma, sc, sh], [(x.shape[1], BF16)])[0]


def resid_norm_mod(x, y, g, gamma, sc, sh, name):
    def fn(x, y, g, gamma, sc, sh):
        x2 = x + g * y
        return x2, (x2 * _rstd(x2)) * gamma * (1.0 + sc) + sh
    return _rows_call(name, fn, x.shape[0], [x, y], [g, gamma, sc, sh], [(x.shape[1], F32), (x.shape[1], BF16)])


def resid_add(x, y, g, name):
    def fn(x, y, g):
        return x + g * y
    return _rows_call(name, fn, x.shape[0], [x, y], [g], [(x.shape[1], F32)])[0]


def resid_bwd(dxo, f, g, name):
    def fn(dxo, f, g):
        return dxo * g, jnp.sum(dxo * f, axis=0, keepdims=True)
    d = dxo.shape[1]
    return _rows_call(name, fn, dxo.shape[0], [dxo, f], [g], [(d, BF16)], [((1, d), F32)])


def norm_mod_bwd(dh, x, dres, gamma, sc, name):
    def fn(dh, x, dres, gamma, sc):
        rstd = _rstd(x)
        xhat = x * rstd
        dxhat = dh * (gamma * (1.0 + sc))
        dx = rstd * (dxhat - xhat * jnp.mean(dxhat * xhat, axis=-1, keepdims=True))
        dhx = dh * xhat
        return (dres + dx, jnp.sum(dh, axis=0, keepdims=True), jnp.sum(dhx * gamma, axis=0, keepdims=True),
                jnp.sum(dhx * (1.0 + sc), axis=0, keepdims=True))
    d = x.shape[1]
    return _rows_call(name, fn, x.shape[0], [dh, x, dres], [gamma, sc], [(d, F32)], [((1, d), F32)] * 3)


def final_loss(x, gamma, target, name):
    d = x.shape[1]

    def fn(x, target, gamma):
        rstd = _rstd(x)
        xhat = x * rstd
        e = xhat * gamma - target
        part = 0.5 * jnp.sum(jnp.sum(e * e, axis=-1, keepdims=True) / d, axis=0, keepdims=True)
        dy = e / d
        dxhat = dy * gamma
        dx = rstd * (dxhat - xhat * jnp.mean(dxhat * xhat, axis=-1, keepdims=True))
        return dx, jnp.broadcast_to(part, (8, LANE)), jnp.sum(dy * xhat, axis=0, keepdims=True)
    return _rows_call(name, fn, x.shape[0], [x, target], [gamma], [(d, F32)], [((8, LANE), F32), ((1, d), F32)])


def _conv3(u, w, b):
    return b + w[2:3] * u + w[1:2] * _shift_down(u, 1) + w[0:1] * _shift_down(u, 2)


def ffn_act(u, cw, cb, name):
    t = u.shape[0]
    nb = D_FF // LANE

    def fn(ug, uv, wg, wv, bg, bv):
        return _gelu(_conv3(ug, wg, bg)) * _conv3(uv, wv, bv)
    ins = [(u, (t, LANE), lambda j: (0, j)), (u, (t, LANE), lambda j: (0, j + nb)),
           (cw, (3, LANE), lambda j: (0, j)), (cw, (3, LANE), lambda j: (0, j + nb)),
           (cb, (1, LANE), lambda j: (0, j)), (cb, (1, LANE), lambda j: (0, j + nb))]
    return _tile_call(name, fn, (nb,), ins, [((t, D_FF), BF16, (t, LANE), lambda j: (0, j), False)])[0]


def ffn_act_bwd(u, dact, cw, cb, name):
    t = u.shape[0]
    nb = D_FF // LANE

    def fn(ug, uv, wg, wv, bg, bv, da, side):
        g = _conv3(ug, wg, bg)
        v = _conv3(uv, wv, bv)
        is_g = side[0:1, 0:1] == 0
        duc = jnp.where(is_g, da * v * _gelu_grad(g), da * _gelu(g))
        w = jnp.where(is_g, wg, wv)
        us = jnp.where(is_g, ug, uv)
        du = w[2:3] * duc + w[1:2] * _shift_up(duc, 1) + w[0:1] * _shift_up(duc, 2)
        dw = jnp.concatenate([jnp.sum(duc * _shift_down(us, 2), axis=0, keepdims=True),
                              jnp.sum(duc * _shift_down(us, 1), axis=0, keepdims=True),
                              jnp.sum(duc * us, axis=0, keepdims=True)], axis=0)
        return du, dw, jnp.sum(duc, axis=0, keepdims=True)
    side = jnp.concatenate([jnp.zeros((nb, 8, LANE), jnp.int32), jnp.ones((nb, 8, LANE), jnp.int32)], axis=0)
    ins = [(u, (t, LANE), lambda j: (0, j % nb)), (u, (t, LANE), lambda j: (0, j % nb + nb)),
           (cw, (3, LANE), lambda j: (0, j % nb)), (cw, (3, LANE), lambda j: (0, j % nb + nb)),
           (cb, (1, LANE), lambda j: (0, j % nb)), (cb, (1, LANE), lambda j: (0, j % nb + nb)),
           (dact, (t, LANE), lambda j: (0, j % nb)), (side, (None, 8, LANE), lambda j: (j, 0, 0))]
    outs = [((t, 2 * D_FF), BF16, (t, LANE), lambda j: (0, j), False),
            ((3, 2 * D_FF), F32, (3, LANE), lambda j: (0, j), False),
            ((1, 2 * D_FF), F32, (1, LANE), lambda j: (0, j), False)]
    return _tile_call(name, fn, (2 * nb,), ins, outs)


def attn_fwd(q, k, v, max_dist, scale, name):
    n, r, l, dh = q.shape
    nb = l // BLK
    m_rows = r * BLK

    def body(q_ref, kc_ref, kp_ref, vc_ref, vp_ref, o_ref, lse_ref):
        b = pl.program_id(1)
        qv = q_ref[...].reshape(m_rows, dh)
        s_c = _dot(qv, kc_ref[...], 1, 1) * scale
        s_p = _dot(qv, kp_ref[...], 1, 1) * scale
        qi = lax.broadcasted_iota(jnp.int32, (m_rows, BLK), 0) % BLK
        kj = lax.broadcasted_iota(jnp.int32, (m_rows, BLK), 1)
        s_c = jnp.where(kj <= qi, s_c, NEG)
        s_p = jnp.where((kj >= qi + (BLK - max_dist)) & (b > 0), s_p, NEG)
        mx = jnp.maximum(jnp.max(s_c, axis=1, keepdims=True), jnp.max(s_p, axis=1, keepdims=True))
        p_c = jnp.exp(s_c - mx)
        p_p = jnp.exp(s_p - mx)
        den = jnp.sum(p_c, axis=1, keepdims=True) + jnp.sum(p_p, axis=1, keepdims=True)
        o = (_dot(p_c, vc_ref[...]) + _dot(p_p, vp_ref[...])) / den
        o_ref[...] = o.reshape(r, BLK, dh)
        lse_ref[...] = jnp.broadcast_to(mx + jnp.log(den), (m_rows, dh)).reshape(r, BLK, dh)

    qspec = pl.BlockSpec((None, r, BLK, dh), lambda i, b: (i, 0, b, 0))
    cur = pl.BlockSpec((None, BLK, dh), lambda i, b: (i, b, 0))
    prev = pl.BlockSpec((None, BLK, dh), lambda i, b: (i, jnp.maximum(b - 1, 0), 0))
    return pl.pallas_call(
        body, name=name, grid=(n, nb),
        in_specs=[qspec, cur, prev, cur, prev],
        out_specs=[qspec, qspec],
        out_shape=[jax.ShapeDtypeStruct((n, r, l, dh), F32)] * 2,
        compiler_params=_params(2),
    )(q, k, k, v, v)


def attn_bwd(q, k, v, do, lse, dvec, max_dist, scale, name):
    n, r, l, dh = q.shape
    nb = l // BLK
    m_rows = r * BLK

    def body(qc_ref, qn_ref, kc_ref, kp_ref, vc_ref, vp_ref, doc_ref, don_ref, lc_ref, ln_ref, dc_ref, dn_ref,
             dq_ref, dk_ref, dv_ref):
        b = pl.program_id(1)
        qi = lax.broadcasted_iota(jnp.int32, (m_rows, BLK), 0) % BLK
        kj = lax.broadcasted_iota(jnp.int32, (m_rows, BLK), 1)
        m_cur = kj <= qi
        m_prev = kj >= qi + (BLK - max_dist)

        def pair(q_ref, do_ref, l_ref, d_ref, k_ref, v_ref, mask):
            qv = q_ref[...].reshape(m_rows, dh)
            dov = do_ref[...].reshape(m_rows, dh)
            lrow = l_ref[...].reshape(m_rows, dh)[:, 0:1]
            drow = d_ref[...].reshape(m_rows, dh)[:, 0:1]
            s = _dot(qv, k_ref[...], 1, 1) * scale
            p = jnp.where(mask, jnp.exp(jnp.where(mask, s, NEG) - lrow), 0.0)
            dp = _dot(dov, v_ref[...], 1, 1)
            ds = p * (dp - drow) * scale
            return qv, dov, p, ds

        q_a, do_a, p_a, ds_a = pair(qc_ref, doc_ref, lc_ref, dc_ref, kc_ref, vc_ref, m_cur)
        _, _, _, ds_b = pair(qc_ref, doc_ref, lc_ref, dc_ref, kp_ref, vp_ref, m_prev & (b > 0))
        q_c, do_c, p_c, ds_c = pair(qn_ref, don_ref, ln_ref, dn_ref, kc_ref, vc_ref, m_prev & (b < nb - 1))
        dq = _dot(ds_a, kc_ref[...]) + _dot(ds_b, kp_ref[...])
        dq_ref[...] = dq.reshape(r, BLK, dh)
        dk_ref[...] = _dot(ds_a, q_a, 0, 0) + _dot(ds_c, q_c, 0, 0)
        dv_ref[...] = _dot(p_a, do_a, 0, 0) + _dot(p_c, do_c, 0, 0)

    qcur = pl.BlockSpec((None, r, BLK, dh), lambda i, b: (i, 0, b, 0))
    qnext = pl.BlockSpec((None, r, BLK, dh), lambda i, b: (i, 0, jnp.minimum(b + 1, nb - 1), 0))
    cur = pl.BlockSpec((None, BLK, dh), lambda i, b: (i, b, 0))
    prev = pl.BlockSpec((None, BLK, dh), lambda i, b: (i, jnp.maximum(b - 1, 0), 0))
    return pl.pallas_call(
        body, name=name, grid=(n, nb),
        in_specs=[qcur, qnext, cur, prev, cur, prev, qcur, qnext, qcur, qnext, qcur, qnext],
        out_specs=[qcur, cur, cur],
        out_shape=[jax.ShapeDtypeStruct((n, r, l, dh), F32), jax.ShapeDtypeStruct((n, l, dh), F32),
                   jax.ShapeDtypeStruct((n, l, dh), F32)],
        compiler_params=_params(2),
    )(q, q, k, k, v, v, do, do, lse, lse, dvec, dvec)


def _scan_rows(t_len, step, init, reverse=False):
    n_chunks = t_len // 8

    def chunk(ci, carry):
        c = (n_chunks - 1 - ci) if reverse else ci
        base = pl.multiple_of(c * 8, 8)
        order = range(7, -1, -1) if reverse else range(8)
        return step(base, order, carry)
    return lax.fori_loop(0, n_chunks, chunk, init)


def _put_row(acc, i, row):
    rid = lax.broadcasted_iota(jnp.int32, acc.shape, 0)
    return jnp.where(rid == i, row, acc)


def _real_scan(a_ref, b_ref, h_ref, t_len, reverse=False):
    c = a_ref.shape[1]

    def step(base, order, h):
        a8 = a_ref[pl.ds(base, 8), :]
        b8 = b_ref[pl.ds(base, 8), :]
        out = jnp.zeros((8, c), F32)
        for i in order:
            h = a8[i:i + 1, :] * h + b8[i:i + 1, :]
            out = _put_row(out, i, h)
        h_ref[pl.ds(base, 8), :] = out
        return h
    _scan_rows(t_len, step, jnp.zeros((1, c), F32), reverse)


def _complex_scan(ar, ai, br_ref, bi_ref, sr_ref, si_ref, t_len, reverse=False):
    c = br_ref.shape[1]

    def step(base, order, carry):
        sr, si = carry
        br8 = br_ref[pl.ds(base, 8), :]
        bi8 = bi_ref[pl.ds(base, 8), :]
        outr = jnp.zeros((8, c), F32)
        outi = jnp.zeros((8, c), F32)
        for i in order:
            nr = ar * sr - ai * si + br8[i:i + 1, :]
            ni = ar * si + ai * sr + bi8[i:i + 1, :]
            sr, si = nr, ni
            outr = _put_row(outr, i, sr)
            outi = _put_row(outi, i, si)
        sr_ref[pl.ds(base, 8), :] = outr
        si_ref[pl.ds(base, 8), :] = outi
        return sr, si
    _scan_rows(t_len, step, (jnp.zeros((1, c), F32), jnp.zeros((1, c), F32)), reverse)


def _rglru_pre(xb, cw, cb, gaw, gab, gxw, gxb, lam):
    xc = cb + cw[3:4] * xb + cw[2:3] * _shift_down(xb, 1) + cw[1:2] * _shift_down(xb, 2) + cw[0:1] * _shift_down(xb, 3)
    r = _sigmoid(_dot(xc, gaw) + gab)
    ig = _sigmoid(_dot(xc, gxw) + gxb)
    sp = _softplus(-lam)
    log_a = -LRU_C * r * sp
    a = jnp.exp(log_a)
    mult = jnp.sqrt(_neg_expm1(2.0 * log_a))
    return xc, r, ig, sp, a, mult


def rglru_fwd(proj, cw, cb, gaw, gab, gxw, gxb, lam, xb_col, yb_col, name):
    t = proj.shape[0]
    nh = cw.shape[1] // LANE

    def body(xb_ref, yb_ref, cw_ref, cb_ref, gaw_ref, gab_ref, gxw_ref, gxb_ref, lam_ref, out_ref, h_ref, a_scr, b_scr):
        xc, r, ig, sp, a, mult = _rglru_pre(xb_ref[...], cw_ref[...], cb_ref[...], gaw_ref[...], gab_ref[...],
                                            gxw_ref[...], gxb_ref[...], lam_ref[...])
        a_scr[...] = a
        b_scr[...] = mult * (ig * xc)
        _real_scan(a_scr, b_scr, h_ref, t)
        out_ref[...] = (h_ref[...] * _gelu(yb_ref[...])).astype(out_ref.dtype)

    col = lambda off: pl.BlockSpec((t, LANE), lambda h: (0, off + h))
    vec = lambda rows: pl.BlockSpec((rows, LANE), lambda h: (0, h))
    wsp = pl.BlockSpec((None, LANE, LANE), lambda h: (h, 0, 0))
    return pl.pallas_call(
        body, name=name, grid=(nh,),
        in_specs=[col(xb_col), col(yb_col), vec(4), vec(1), wsp, vec(1), wsp, vec(1), vec(1)],
        out_specs=[col(0), col(0)],
        out_shape=[jax.ShapeDtypeStruct((t, nh * LANE), BF16), jax.ShapeDtypeStruct((t, nh * LANE), F32)],
        scratch_shapes=[pltpu.VMEM((t, LANE), F32)] * 2,
        compiler_params=_params(1),
    )(proj, proj, cw, cb, gaw, gab, gxw, gxb, lam)


def rglru_bwd(proj, hs, dlru, dlru_col, cw, cb, gaw, gab, gxw, gxb, lam, xb_col, yb_col, name):
    t = proj.shape[0]
    nh = cw.shape[1] // LANE

    def body(xb_ref, yb_ref, h_ref, dl_ref, cw_ref, cb_ref, gaw_ref, gab_ref, gxw_ref, gxb_ref, lam_ref,
             dxb_ref, dyb_ref, dcw_ref, dcb_ref, dgaw_ref, dgab_ref, dgxw_ref, dgxb_ref, dlam_ref, an_scr, dh_scr, gh_scr):
        xb = xb_ref[...]
        yb = yb_ref[...]
        cwv = cw_ref[...]
        lam = lam_ref[...]
        xc, r, ig, sp, a, mult = _rglru_pre(xb, cwv, cb_ref[...], gaw_ref[...], gab_ref[...], gxw_ref[...], gxb_ref[...], lam)
        h = h_ref[...]
        dl = dl_ref[...]
        dyb_ref[...] = (dl * h * _gelu_grad(yb)).astype(dyb_ref.dtype)
        dh_scr[...] = dl * _gelu(yb)
        an_scr[...] = _shift_up(a, 1)
        _real_scan(an_scr, dh_scr, gh_scr, t, reverse=True)
        gh = gh_scr[...]
        da = gh * _shift_down(h, 1)
        dmult = gh * ig * xc
        dig = gh * mult * xc
        dxc = gh * mult * ig
        dla = (da - dmult * a / mult) * a
        dr = dla * (-LRU_C * sp)
        dsp = jnp.sum(dla * (-LRU_C * r), axis=0, keepdims=True)
        dlam_ref[...] = dsp * (-_sigmoid(-lam))
        dpr = dr * r * (1.0 - r)
        dpi = dig * ig * (1.0 - ig)
        dgab_ref[...] = jnp.sum(dpr, axis=0, keepdims=True)
        dgxb_ref[...] = jnp.sum(dpi, axis=0, keepdims=True)
        dgaw_ref[...] = _dot(xc, dpr, 0, 0)
        dgxw_ref[...] = _dot(xc, dpi, 0, 0)
        dxc = dxc + _dot(dpr, gaw_ref[...], 1, 1) + _dot(dpi, gxw_ref[...], 1, 1)
        dxb = cwv[3:4] * dxc + cwv[2:3] * _shift_up(dxc, 1) + cwv[1:2] * _shift_up(dxc, 2) + cwv[0:1] * _shift_up(dxc, 3)
        dxb_ref[...] = dxb.astype(dxb_ref.dtype)
        dcw_ref[...] = jnp.concatenate([jnp.sum(dxc * _shift_down(xb, 3 - i), axis=0, keepdims=True) for i in range(4)], axis=0)
        dcb_ref[...] = jnp.sum(dxc, axis=0, keepdims=True)

    col = lambda off: pl.BlockSpec((t, LANE), lambda h: (0, off + h))
    vec = lambda rows: pl.BlockSpec((rows, LANE), lambda h: (0, h))
    wsp = pl.BlockSpec((None, LANE, LANE), lambda h: (h, 0, 0))
    w = nh * LANE
    sds = jax.ShapeDtypeStruct
    return pl.pallas_call(
        body, name=name, grid=(nh,),
        in_specs=[col(xb_col), col(yb_col), col(0), col(dlru_col), vec(4), vec(1), wsp, vec(1), wsp, vec(1), vec(1)],
        out_specs=[col(0), col(0), vec(4), vec(1), wsp, vec(1), wsp, vec(1), vec(1)],
        out_shape=[sds((t, w), BF16), sds((t, w), BF16), sds((4, w), F32), sds((1, w), F32), sds((nh, LANE, LANE), F32),
                   sds((1, w), F32), sds((nh, LANE, LANE), F32), sds((1, w), F32), sds((1, w), F32)],
        scratch_shapes=[pltpu.VMEM((t, LANE), F32)] * 3,
        compiler_params=_params(1),
    )(proj, proj, hs, dlru, cw, cb, gaw, gab, gxw, gxb, lam)


S5_CHUNKS = 8
S5_STATES = 512


def s5_fwd(proj, u_col, bre, bim, are, aim, cre, cim, dsk, name):
    t = proj.shape[0]

    def body(u_ref, bre_ref, bim_ref, are_ref, aim_ref, cre_ref, cim_ref, d_ref, y_ref, sr_ref, si_ref, br_scr, bi_scr):
        u = u_ref[...]
        br_scr[...] = _dot(u, bre_ref[...])
        bi_scr[...] = _dot(u, bim_ref[...])
        _complex_scan(are_ref[...], aim_ref[...], br_scr, bi_scr, sr_ref, si_ref, t)
        y_ref[...] = _dot(sr_ref[...], cre_ref[...]) - _dot(si_ref[...], cim_ref[...]) + d_ref[...] * u

    ucol = pl.BlockSpec((t, LANE), lambda c: (0, u_col + c))
    ycol = pl.BlockSpec((t, LANE), lambda c: (0, c))
    scol = pl.BlockSpec((t, S5_STATES), lambda c: (0, c))
    bsp = pl.BlockSpec((None, LANE, S5_STATES), lambda c: (c, 0, 0))
    csp = pl.BlockSpec((None, S5_STATES, LANE), lambda c: (c, 0, 0))
    asp = pl.BlockSpec((1, S5_STATES), lambda c: (0, c))
    dsp = pl.BlockSpec((1, LANE), lambda c: (0, c))
    sds = jax.ShapeDtypeStruct
    return pl.pallas_call(
        body, name=name, grid=(S5_CHUNKS,),
        in_specs=[ucol, bsp, bsp, asp, asp, csp, csp, dsp],
        out_specs=[ycol, scol, scol],
        out_shape=[sds((t, S5_CHUNKS * LANE), F32), sds((t, S5_CHUNKS * S5_STATES), F32), sds((t, S5_CHUNKS * S5_STATES), F32)],
        scratch_shapes=[pltpu.VMEM((t, S5_STATES), F32)] * 2,
        compiler_params=_params(1),
    )(proj, bre, bim, are, aim, cre, cim, dsk)


def s5_bwd(proj, u_col, dy, sr, si, bre, bim, are, aim, cre, cim, dsk, name):
    t = proj.shape[0]
    half = S5_STATES // 2

    def body(u_ref, dy_ref, sr_ref, si_ref, bre_ref, bim_ref, are_ref, aim_ref, cre_ref, cim_ref, d_ref,
             du_ref, dbre_ref, dbim_ref, dare_ref, daim_ref, dcre_ref, dcim_ref, dd_ref, dsr_scr, dsi_scr, gr_scr, gi_scr):
        hh = pl.program_id(1)
        u = u_ref[...]
        dy = dy_ref[...]
        dsr_scr[...] = _dot(dy, cre_ref[...], 1, 1)
        dsi_scr[...] = -_dot(dy, cim_ref[...], 1, 1)
        _complex_scan(are_ref[...], -aim_ref[...], dsr_scr, dsi_scr, gr_scr, gi_scr, t, reverse=True)
        gr = gr_scr[...]
        gi = gi_scr[...]
        spr = _shift_down(sr_ref[...], 1)
        spi = _shift_down(si_ref[...], 1)
        dare_ref[...] = jnp.sum(gr * spr + gi * spi, axis=0, keepdims=True)
        daim_ref[...] = jnp.sum(gi * spr - gr * spi, axis=0, keepdims=True)
        du = _dot(gr, bre_ref[...], 1, 1) + _dot(gi, bim_ref[...], 1, 1)

        @pl.when(hh == 0)
        def _():
            du_ref[...] = du + d_ref[...] * dy

        @pl.when(hh > 0)
        def _():
            du_ref[...] += du

        dbre_ref[...] = _dot(u, gr, 0, 0)
        dbim_ref[...] = _dot(u, gi, 0, 0)
        dcre_ref[...] = _dot(sr_ref[...], dy, 0, 0)
        dcim_ref[...] = -_dot(si_ref[...], dy, 0, 0)
        dd_ref[...] = jnp.sum(dy * u, axis=0, keepdims=True)

    ucol = pl.BlockSpec((t, LANE), lambda c, h: (0, u_col + c))
    ycol = pl.BlockSpec((t, LANE), lambda c, h: (0, c))
    scol = pl.BlockSpec((t, half), lambda c, h: (0, 2 * c + h))
    bsp = pl.BlockSpec((None, LANE, half), lambda c, h: (c, 0, h))
    csp = pl.BlockSpec((None, half, LANE), lambda c, h: (c, h, 0))
    asp = pl.BlockSpec((1, half), lambda c, h: (0, 2 * c + h))
    dsp = pl.BlockSpec((1, LANE), lambda c, h: (0, c))
    sds = jax.ShapeDtypeStruct
    return pl.pallas_call(
        body, name=name, grid=(S5_CHUNKS, 2),
        in_specs=[ucol, ycol, scol, scol, bsp, bsp, asp, asp, csp, csp, dsp],
        out_specs=[ycol, bsp, bsp, asp, asp, csp, csp, dsp],
        out_shape=[sds((t, S5_CHUNKS * LANE), F32), sds((S5_CHUNKS, LANE, S5_STATES), F32), sds((S5_CHUNKS, LANE, S5_STATES), F32),
                   sds((1, S5_CHUNKS * S5_STATES), F32), sds((1, S5_CHUNKS * S5_STATES), F32),
                   sds((S5_CHUNKS, S5_STATES, LANE), F32), sds((S5_CHUNKS, S5_STATES, LANE), F32), sds((1, S5_CHUNKS * LANE), F32)],
        scratch_shapes=[pltpu.VMEM((t, half), F32)] * 4,
        compiler_params=_params(2),
    )(proj, dy, sr, si, bre, bim, are, aim, cre, cim, dsk)


def s5_prep(a_re, a_im, b_re, b_im, c_re, c_im, log_dt):
    lam = lax.complex(a_re, a_im)
    dt = jnp.exp(log_dt)[:, None]
    a_bar = jnp.exp(lam * dt)
    b_bar = ((a_bar - 1.0) / lam)[..., None] * lax.complex(b_re, b_im)
    g, p, cg = b_re.shape
    eye = jnp.eye(8, dtype=F32)

    def in_map(m):
        m = m.reshape(g // 8, 8, p, cg)
        return jnp.einsum("ab,kapc->kacbp", eye, m).reshape(g // 8, 8 * cg, 8 * p)

    def out_map(m):
        m = m.reshape(g // 8, 8, cg, p)
        return jnp.einsum("ab,kacp->kapbc", eye, m).reshape(g // 8, 8 * p, 8 * cg)

    return (jnp.real(a_bar).reshape(1, g * p), jnp.imag(a_bar).reshape(1, g * p), in_map(jnp.real(b_bar)), in_map(jnp.imag(b_bar)),
            out_map(c_re), out_map(c_im))


def rope_tables(pos, half):
    inv = ROPE_THETA ** (-jnp.arange(half, dtype=F32) / half)
    ang = pos.astype(F32)[:, None] * inv
    cos, sin = jnp.cos(ang), jnp.sin(ang)
    reps = max(LANE // (2 * half), 1)
    return jnp.tile(jnp.concatenate([cos, cos], axis=1), (1, reps)), jnp.tile(jnp.concatenate([-sin, sin], axis=1), (1, reps))


A_W = 1024
ROW_T = 256


def _tiled(a, width, col):
    return (a, (ROW_T, width), lambda i: (i, col))


def _out_tiled(t, width, dtype):
    return ((t, width), dtype, (ROW_T, width), lambda i: (i, 0), False)


def qkv_rope_even(proj, cos, sin, name):
    t = proj.shape[0]

    def fn(q, k, v, cos, sin):
        return _rope(q, cos, sin, 64), _rope(k, cos, sin, 64), v
    ins = [_tiled(proj, A_W, 0), _tiled(proj, A_W, 1), _tiled(proj, A_W, 2), _tiled(cos, LANE, 0), _tiled(sin, LANE, 0)]
    return _tile_call(name, fn, (t // ROW_T,), ins, [_out_tiled(t, A_W, BF16)] * 3)


def merge3(o, lse, name):
    t = o[0].shape[0]

    def fn(o1, o2, o3, l1, l2, l3):
        mx = jnp.maximum(jnp.maximum(l1, l2), l3)
        e1, e2, e3 = jnp.exp(l1 - mx), jnp.exp(l2 - mx), jnp.exp(l3 - mx)
        den = e1 + e2 + e3
        out = (e1 * o1 + e2 * o2 + e3 * o3) / den
        return out, out, mx + jnp.log(den)
    ins = [_tiled(a, A_W, 0) for a in list(o) + list(lse)]
    return _tile_call(name, fn, (t // ROW_T,), ins, [_out_tiled(t, A_W, BF16), _out_tiled(t, A_W, F32), _out_tiled(t, A_W, F32)])


def _segsum_bcast(x, width):
    parts = []
    for h in range(x.shape[1] // width):
        s = jnp.sum(x[:, h * width:(h + 1) * width], axis=1, keepdims=True)
        parts.append(jnp.broadcast_to(s, (x.shape[0], width)))
    return jnp.concatenate(parts, axis=1)


def even_attn_prep(dmix, attn, name):
    t = attn.shape[0]

    def fn(dout, attn):
        return dout, _segsum_bcast(dout * attn, LANE)
    ins = [_tiled(dmix, A_W, 0), _tiled(attn, A_W, 0)]
    return _tile_call(name, fn, (t // ROW_T,), ins, [_out_tiled(t, A_W, BF16), _out_tiled(t, A_W, F32)])


def even_dproj(dq, dk, dv, dxb, dyb, cos, sin, name):
    t = dxb.shape[0]

    def fn(q1, q2, q3, k1, k2, k3, v1, v2, v3, dxb, dyb, cos, sin):
        return jnp.concatenate([_rope_t(q1 + q2 + q3, cos, sin, 64).astype(BF16), _rope_t(k1 + k2 + k3, cos, sin, 64).astype(BF16),
                                (v1 + v2 + v3).astype(BF16), dxb, dyb], axis=1)
    ins = [_tiled(a, A_W, 0) for a in list(dq) + list(dk) + list(dv) + [dxb, dyb]] + [_tiled(cos, LANE, 0), _tiled(sin, LANE, 0)]
    return _tile_call(name, fn, (t // ROW_T,), ins, [_out_tiled(t, 5 * A_W, BF16)])[0]


def perm(x, d):
    t = x.shape[0]
    return x.reshape(t // d, d, 8, LANE).transpose(1, 2, 0, 3).reshape(d * 8, t // d, LANE)


def unperm(xp, d):
    n, l, _ = xp.shape
    return xp.reshape(d, 8, l, LANE).transpose(2, 0, 1, 3).reshape(l * d, 8 * LANE)


def qkv_rope_odd(proj, cos, sin, name):
    t = proj.shape[0]

    def fn(q, k, v, cos, sin):
        return _rope(q, cos, sin, 32), _rope(k, cos, sin, 32), v
    ins = [_tiled(proj, A_W, 0), _tiled(proj, LANE, 8), _tiled(proj, LANE, 9), _tiled(cos, LANE, 0), _tiled(sin, LANE, 0)]
    return _tile_call(name, fn, (t // ROW_T,), ins, [_out_tiled(t, A_W, BF16), _out_tiled(t, LANE, BF16), _out_tiled(t, LANE, BF16)])


def _head_blocks(a):
    return (a, (None, ROW_T, a.shape[2]), lambda h, i: (h, i, 0))


def sink_fwd(o, lse, sink_b, name):
    nh, t, dh = o.shape

    def fn(o, lse, s):
        return o * _sigmoid(lse - s)
    ins = [_head_blocks(o), _head_blocks(lse), (sink_b, (None, 1, dh), lambda h, i: (h, 0, 0))]
    return _tile_call(name, fn, (nh, t // ROW_T), ins, [((nh, t, dh), BF16, (None, ROW_T, dh), lambda h, i: (h, i, 0), False)])[0]


def sink_bwd(dof, o, lse, sink_b, name):
    nh, t, dh = o.shape

    def fn(dof, o, lse, s):
        keep = _sigmoid(lse - s)
        dk = jnp.sum(dof * o, axis=1, keepdims=True)
        dlse = dk * keep * (1.0 - keep)
        return dof * keep, dk * keep * keep, -jnp.sum(dlse, axis=0, keepdims=True)
    ins = [_head_blocks(dof), _head_blocks(o), _head_blocks(lse), (sink_b, (None, 1, dh), lambda h, i: (h, 0, 0))]
    outs = [((nh, t, dh), BF16, (None, ROW_T, dh), lambda h, i: (h, i, 0), False),
            ((nh, t, dh), F32, (None, ROW_T, dh), lambda h, i: (h, i, 0), False),
            ((nh, 1, dh), F32, (None, 1, dh), lambda h, i: (h, 0, 0), True)]
    return _tile_call(name, fn, (nh, t // ROW_T), ins, outs, acc_axis=1)


def odd_dproj(dq, dk, dv, du, cos, sin, name):
    t = dq.shape[0]

    def fn(dq, dk, dv, du, cos, sin):
        return jnp.concatenate([_rope_t(dq, cos, sin, 32), _rope_t(dk, cos, sin, 32), dv, du], axis=1)
    ins = [_tiled(dq, A_W, 0), _tiled(dk, LANE, 0), _tiled(dv, LANE, 0), _tiled(du, A_W, 0), _tiled(cos, LANE, 0), _tiled(sin, LANE, 0)]
    return _tile_call(name, fn, (t // ROW_T,), ins, [_out_tiled(t, 2 * A_W + 2 * LANE, BF16)])[0]


def glu_z(y, name):
    return _rows_call(name, _gelu, y.shape[0], [y], [], [(y.shape[1], BF16)])[0]


def glu_out(y, gpre, b, name):
    def fn(y, gpre, b):
        return _gelu(y) * _sigmoid(gpre + b)
    return _rows_call(name, fn, y.shape[0], [y, gpre], [b], [(y.shape[1], BF16)])[0]


def glu_bwd_gate(dmix, y, gpre, b, name):
    t = y.shape[0]

    def fn(dout, y, gpre, b):
        gate = _sigmoid(gpre + b)
        dgp = dout * _gelu(y) * gate * (1.0 - gate)
        return dgp, jnp.sum(dgp, axis=0, keepdims=True)
    ins = [_tiled(dmix, A_W, 1), _tiled(y, A_W, 0), _tiled(gpre, A_W, 0), (b, (1, A_W), lambda i: (0, 0))]
    outs = [_out_tiled(t, A_W, BF16), ((1, A_W), F32, (1, A_W), lambda i: (0, 0), True)]
    return _tile_call(name, fn, (t // ROW_T,), ins, outs, acc_axis=0)


def glu_bwd_y(dmix, y, gpre, b, dz_mm, name):
    t = y.shape[0]

    def fn(dout, y, gpre, b, dz_mm):
        return (dout * _sigmoid(gpre + b) + dz_mm) * _gelu_grad(y)
    ins = [_tiled(dmix, A_W, 1), _tiled(y, A_W, 0), _tiled(gpre, A_W, 0), (b, (1, A_W), lambda i: (0, 0)), _tiled(dz_mm, A_W, 0)]
    return _tile_call(name, fn, (t // ROW_T,), ins, [_out_tiled(t, A_W, F32)])[0]


def adamw(w, g, m, v, name):
    def fn(w, g, m, v):
        m = ADAM_B1 * m + (1.0 - ADAM_B1) * g
        v = ADAM_B2 * v + (1.0 - ADAM_B2) * (g * g)
        m_hat = m / (1.0 - ADAM_B1 ** ADAM_STEP)
        v_hat = v / (1.0 - ADAM_B2 ** ADAM_STEP)
        return -ADAM_LR * (m_hat / (jnp.sqrt(v_hat) + ADAM_EPS) + ADAM_WD * w), m, v
    c = w.shape[1]
    return _rows_call(name, fn, w.shape[0], [w, g, m, v], [], [(c, F32)] * 3)


def _sum_in_order(v):
    s = v[0].astype(F32)
    for d in range(1, v.shape[0]):
        s = s + v[d].astype(F32)
    return s


def sum_devices(parts, name):
    nd, nl, r, c = parts.shape
    tr = 8
    while tr * 2 <= 256 and r % (tr * 2) == 0 and tr * 2 * c * 4 * nd <= 4 * 1024 * 1024:
        tr *= 2
    ins = [(parts, (nd, None, tr, c), lambda l, i: (0, l, i, 0))]
    outs = [((nl, r, c), F32, (None, tr, c), lambda l, i: (l, i, 0), False)]
    return _tile_call(name, _sum_in_order, (nl, r // tr), ins, outs)[0]


def silu_rows(c_all, name):
    def fn(c):
        return c * _sigmoid(c)
    return _rows_call(name, fn, c_all.shape[0], [c_all], [], [(c_all.shape[1], F32)])[0]


def _place():
    x, y, c = lax.axis_index("x"), lax.axis_index("y"), lax.axis_index("c")
    return x, y, c


ANY = pl.BlockSpec(memory_space=pl.ANY)


def all_gather8(v, name):
    r, cdim = v.shape

    def body(x_ref, out_ref, send_sems, recv_sems, local_sem):
        x, y, c = _place()
        me, sibling = (x, y, c), (x, y, 1 - c)
        chips = [(1 - x, y), (x, 1 - y), (1 - x, 1 - y)]

        def rows(px, py, pc):
            return out_ref.at[4 * px + 2 * py + pc]

        def copy(k, block, to, src=None):
            return pltpu.make_async_remote_copy(
                src_ref=rows(*block) if src is None else src, dst_ref=rows(*block),
                send_sem=send_sems.at[k], recv_sem=recv_sems.at[k], device_id=to, device_id_type=MESH)

        mine = pltpu.make_async_copy(x_ref, rows(*me), local_sem)
        mine.start()
        first = [copy(0, me, sibling, src=x_ref)]
        first += [copy(1 + j, me, (*chip, c), src=x_ref) for j, chip in enumerate(chips)]
        for cp in first:
            cp.start()
        passed = [copy(4 + j, (*chip, c), sibling) for j, chip in enumerate(chips)]
        for j, chip in enumerate(chips):
            copy(1 + j, (*chip, c), me).wait_recv()
            passed[j].start()
        copy(0, sibling, me).wait_recv()
        for j, chip in enumerate(chips):
            copy(4 + j, (*chip, 1 - c), me).wait_recv()
        for cp in first + passed:
            cp.wait_send()
        mine.wait()

    return pl.pallas_call(
        body, name=name, out_shape=jax.ShapeDtypeStruct((N_DEV, r, cdim), v.dtype),
        in_specs=[ANY], out_specs=ANY,
        scratch_shapes=[pltpu.SemaphoreType.DMA((7,)), pltpu.SemaphoreType.DMA((7,)), pltpu.SemaphoreType.DMA],
    )(v)


def gather_weights(shards, name):
    n = len(shards)

    def body(*refs):
        ins, outs = refs[:n], refs[n:2 * n]
        send_sems, recv_sems, local_sems = refs[2 * n:]
        x, y, c = _place()
        sibling = (x, y, 1 - c)
        chips = [(1 - x, y), (x, 1 - y), (1 - x, 1 - y)]
        my_chip = 2 * x + y
        def half(t, chip_slot, start):
            hr = ins[t].shape[1] // 2
            return outs[t].at[chip_slot, :, pl.ds(start, hr), :]

        def copy(t, k, src, dst, to):
            return pltpu.make_async_remote_copy(src_ref=src, dst_ref=dst, send_sem=send_sems.at[6 * t + k],
                                                recv_sem=recv_sems.at[6 * t + k], device_id=to, device_id_type=MESH)

        def lows(t):
            hr = ins[t].shape[1] // 2
            return hr, pl.multiple_of(c * hr, 16), pl.multiple_of((1 - c) * hr, 16)

        started, local_copies = [], []
        for t in range(n):
            hr, lo, _ = lows(t)
            mine = pltpu.make_async_copy(ins[t], outs[t].at[my_chip], local_sems.at[t])
            mine.start()
            local_copies.append(mine)
            for j, chip in enumerate(chips):
                cp = copy(t, j, ins[t].at[:, pl.ds(lo, hr), :], half(t, my_chip, lo), (*chip, c))
                cp.start()
                started.append(cp)
        for t in range(n):
            hr, lo, _ = lows(t)
            for j, (px, py) in enumerate(chips):
                slot = 2 * px + py
                copy(t, j, half(t, slot, lo), half(t, slot, lo), (px, py, c)).wait_recv()
                fwd = copy(t, 3 + j, half(t, slot, lo), half(t, slot, lo), sibling)
                fwd.start()
                started.append(fwd)
        for t in range(n):
            hr, _, lo_sib = lows(t)
            for j, (px, py) in enumerate(chips):
                slot = 2 * px + py
                copy(t, 3 + j, half(t, slot, lo_sib), half(t, slot, lo_sib), sibling).wait_recv()
        for cp in started:
            cp.wait_send()
        for cp in local_copies:
            cp.wait()

    return pl.pallas_call(
        body, name=name,
        out_shape=[jax.ShapeDtypeStruct((N_CHIP,) + s.shape, s.dtype) for s in shards],
        in_specs=[ANY] * n, out_specs=[ANY] * n,
        scratch_shapes=[pltpu.SemaphoreType.DMA((6 * n,)), pltpu.SemaphoreType.DMA((6 * n,)), pltpu.SemaphoreType.DMA((n,))],
    )(*shards)


def exchange_pieces(grads, name):
    n = len(grads)

    def body(*refs):
        ins, outs = refs[:n], refs[n:2 * n]
        send_sems, recv_sems, local_sems = refs[2 * n:]
        x, y, c = _place()
        me = 4 * x + 2 * y + c
        waits = []
        for t in range(n):
            g_ref, o_ref = ins[t], outs[t]
            hr = g_ref.shape[2] // 2

            def piece(qx, qy, qc, g_ref=g_ref, hr=hr):
                return g_ref.at[2 * qx + qy, :, pl.ds(pl.multiple_of(qc * hr, 16), hr), :]

            mine = pltpu.make_async_copy(piece(x, y, c), o_ref.at[me], local_sems.at[t])
            mine.start()
            sends = []
            for k in range(1, N_DEV):
                qx = (1 - x) if (k >> 2) & 1 else x
                qy = (1 - y) if (k >> 1) & 1 else y
                qc = (1 - c) if k & 1 else c
                cp = pltpu.make_async_remote_copy(src_ref=piece(qx, qy, qc), dst_ref=o_ref.at[me],
                                                  send_sem=send_sems.at[7 * t + k - 1], recv_sem=recv_sems.at[7 * t + k - 1],
                                                  device_id=(qx, qy, qc), device_id_type=MESH)
                cp.start()
                sends.append((cp, 4 * qx + 2 * qy + qc))
            waits.append((sends, mine, o_ref, t))
        for sends, mine, o_ref, t in waits:
            for k, (cp, peer) in enumerate(sends):
                pltpu.make_async_remote_copy(src_ref=o_ref.at[peer], dst_ref=o_ref.at[peer], send_sem=send_sems.at[7 * t + k],
                                             recv_sem=recv_sems.at[7 * t + k], device_id=(x, y, c), device_id_type=MESH).wait_recv()
            for cp, _ in sends:
                cp.wait_send()
            mine.wait()

    return pl.pallas_call(
        body, name=name,
        out_shape=[jax.ShapeDtypeStruct((N_DEV, g.shape[1], g.shape[2] // 2, g.shape[3]), g.dtype) for g in grads],
        in_specs=[ANY] * n, out_specs=[ANY] * n,
        scratch_shapes=[pltpu.SemaphoreType.DMA((7 * n,)), pltpu.SemaphoreType.DMA((7 * n,)), pltpu.SemaphoreType.DMA((n,))],
    )(*grads)


def join_halves(halves, name):
    n = len(halves)

    def body(*refs):
        ins, outs = refs[:n], refs[n:2 * n]
        send_sems, recv_sems, local_sems = refs[2 * n:]
        x, y, c = _place()
        sibling = (x, y, 1 - c)
        pending = []
        for t in range(n):
            h_ref, o_ref = ins[t], outs[t]
            hr = h_ref.shape[1]
            lo = pl.multiple_of(c * hr, 8)
            lo_sib = pl.multiple_of((1 - c) * hr, 8)
            mine = pltpu.make_async_copy(h_ref, o_ref.at[:, pl.ds(lo, hr), :], local_sems.at[t])
            mine.start()
            cp = pltpu.make_async_remote_copy(src_ref=h_ref, dst_ref=o_ref.at[:, pl.ds(lo, hr), :], send_sem=send_sems.at[t],
                                              recv_sem=recv_sems.at[t], device_id=sibling, device_id_type=MESH)
            cp.start()
            got = pltpu.make_async_remote_copy(src_ref=h_ref, dst_ref=o_ref.at[:, pl.ds(lo_sib, hr), :], send_sem=send_sems.at[t],
                                               recv_sem=recv_sems.at[t], device_id=sibling, device_id_type=MESH)
            pending.append((mine, cp, got))
        for mine, cp, got in pending:
            got.wait_recv()
            cp.wait_send()
            mine.wait()

    return pl.pallas_call(
        body, name=name,
        out_shape=[jax.ShapeDtypeStruct((h.shape[0], 2 * h.shape[1], h.shape[2]), h.dtype) for h in halves],
        in_specs=[ANY] * n, out_specs=[ANY] * n,
        scratch_shapes=[pltpu.SemaphoreType.DMA((n,)), pltpu.SemaphoreType.DMA((n,)), pltpu.SemaphoreType.DMA((n,))],
    )(*halves)


WEIGHTS = ['ada_w', 'ada_b', 'norm_mix', 'norm_ffn', 'norm_final', 'ev_w_in', 'ev_conv_w', 'ev_conv_b', 'ev_gate_a_w', 'ev_gate_a_b',
           'ev_gate_x_w', 'ev_gate_x_b', 'ev_lambda', 'ev_w_out', 'od_w_in', 'od_sinks', 'od_a_re', 'od_a_im', 'od_b_re', 'od_b_im',
           'od_c_re', 'od_c_im', 'od_d', 'od_log_dt', 'od_glu_w', 'od_glu_b', 'od_w_out', 'ffn_w_in', 'ffn_conv_w', 'ffn_conv_b', 'ffn_w_out']
BIG = ['ev_w_in', 'ev_w_out', 'od_w_in', 'od_glu_w', 'od_w_out', 'ffn_w_in', 'ffn_w_out']
COL_SHARDED = ('ev_w_in', 'od_w_in', 'ffn_w_in')
SMALL_SHARDED = ['ev_conv_w', 'od_d', 'od_glu_b', 'ffn_conv_w']
SMALL = [n for n in WEIGHTS if n not in BIG and n != 'ada_w']


def _pack(arrs):
    flat = jnp.concatenate([a.reshape(-1).astype(F32) for a in arrs])
    rows = -(-flat.shape[0] // (8 * LANE)) * 8
    return jnp.pad(flat, (0, rows * LANE - flat.shape[0])).reshape(rows, LANE)


def _unpack(flat, shapes):
    out, off = [], 0
    for s in shapes:
        n = math.prod(s)
        out.append(flat[..., off:off + n].reshape(flat.shape[:-1] + tuple(s)))
        off += n
    return out


def _ffn_fwd(l, h2, wf, cw, cb):
    u = mm(h2, wf['ffn_w_in'][l], tm=2048, tn=256, name=f"ffn_in{l}")
    act = ffn_act(u, cw, cb, f"ffn_act{l}")
    f = mm(act, wf['ffn_w_out'][l], tm=1024, tn=512, name=f"ffn_out{l}")
    return f, dict(u=u, act=act)


def _ffn_bwd(l, df, s, h2, wf, cw, cb):
    dact = mm(df, wf['ffn_w_out'][l], tb=True, tm=2048, tn=128, name=f"ffn_dact{l}")
    dwo = mm(s['act'], df, ta=True, out_dtype=BF16, tm=D_FF, tn=256, tk=512, name=f"ffn_dwo{l}")
    du, dcw, dcb = ffn_act_bwd(s['u'], dact, cw, cb, f"ffn_act_bwd{l}")
    dh2 = mm(du, wf['ffn_w_in'][l], tb=True, tm=1024, tn=512, tk=D_FF, name=f"ffn_dh{l}")
    dwi = mm(h2, du, ta=True, out_dtype=BF16, tm=2048, tn=256, name=f"ffn_dwi{l}")
    return dh2, dwi, dwo, dcw, dcb


def _even_fwd(e, h1, a, wf, fs, tabs):
    cos, sin = tabs
    proj = mm(h1, wf['ev_w_in'][e], name=f"ev_in{e}")
    q, k, v = qkv_rope_even(proj, cos, sin, f"ev_rope{e}")
    outs, lses = [], []
    for window, d in A_PATTERNS:
        o, lse = attn_fwd(perm(q, d)[:, None], perm(k, d), perm(v, d), window // d, LANE ** -0.5, f"ev_attn{e}_{d}")
        outs.append(unperm(o[:, 0], d))
        lses.append(unperm(lse[:, 0], d))
    attn_bf, attn, lse_tot = merge3(outs, lses, f"ev_merge{e}")
    lru, hs = rglru_fwd(proj, fs['ev_conv_w'][e], a['ev_conv_b'][e][None], a['ev_gate_a_w'][e], a['ev_gate_a_b'][e][None],
                        a['ev_gate_x_w'][e], a['ev_gate_x_b'][e][None], a['ev_lambda'][e][None], 24, 32, f"ev_lru{e}")
    mix = jnp.concatenate([attn_bf, lru], axis=1)
    y = mm(mix, wf['ev_w_out'][e], name=f"ev_out{e}")
    return y, dict(proj=proj, q=q, k=k, v=v, attn=attn, lse=lse_tot, hs=hs, mix=mix)


def _even_bwd(e, dyg, s, h1, a, wf, fs, tabs, gs):
    cos, sin = tabs
    dmix = mm(dyg, wf['ev_w_out'][e], tb=True, name=f"ev_dmix{e}")
    dwo = mm(s['mix'], dyg, ta=True, out_dtype=BF16, name=f"ev_dwo{e}")
    do_bf, dvec = even_attn_prep(dmix, s['attn'], f"ev_prep{e}")
    dqs, dks, dvs = [], [], []
    for window, d in A_PATTERNS:
        dq, dk, dv = attn_bwd(perm(s['q'], d)[:, None], perm(s['k'], d), perm(s['v'], d), perm(do_bf, d)[:, None],
                              perm(s['lse'], d)[:, None], perm(dvec, d)[:, None], window // d, LANE ** -0.5, f"ev_attn_bwd{e}_{d}")
        dqs.append(unperm(dq[:, 0], d))
        dks.append(unperm(dk, d))
        dvs.append(unperm(dv, d))
    dxb, dyb, dcw, dcb, dgaw, dgab, dgxw, dgxb, dlam = rglru_bwd(
        s['proj'], s['hs'], dmix, 8, fs['ev_conv_w'][e], a['ev_conv_b'][e][None], a['ev_gate_a_w'][e], a['ev_gate_a_b'][e][None],
        a['ev_gate_x_w'][e], a['ev_gate_x_b'][e][None], a['ev_lambda'][e][None], 24, 32, f"ev_lru_bwd{e}")
    for n, g in (('ev_conv_w', dcw), ('ev_conv_b', dcb[0]), ('ev_gate_a_w', dgaw), ('ev_gate_a_b', dgab[0]), ('ev_gate_x_w', dgxw),
                 ('ev_gate_x_b', dgxb[0]), ('ev_lambda', dlam[0])):
        gs[n][e] = g
    dproj = even_dproj(dqs, dks, dvs, dxb, dyb, cos, sin, f"ev_dproj{e}")
    dh1 = mm(dproj, wf['ev_w_in'][e], tb=True, tk=2560, name=f"ev_dh{e}")
    dwi = mm(h1, dproj, ta=True, out_dtype=BF16, tm=2048, tn=512, name=f"ev_dwi{e}")
    return dh1, dwi, dwo


def _odd_fwd(o, h1, a, wf, fs, tabs):
    cos, sin = tabs
    t = h1.shape[0]
    proj = mm(h1, wf['od_w_in'][o], name=f"od_in{o}")
    qr, kr, vv = qkv_rope_odd(proj, cos, sin, f"od_rope{o}")
    qh = qr.reshape(t, 2, 8, 64).transpose(1, 2, 0, 3)
    kh = kr.reshape(t, 2, 64).transpose(1, 0, 2)
    vh = vv.reshape(t, 2, 64).transpose(1, 0, 2)
    oh, lse = attn_fwd(qh, kh, vh, 127, 64 ** -0.5, f"od_attn{o}")
    sink_b = jnp.broadcast_to(a['od_sinks'][o].reshape(16, 1, 1), (16, 1, 64))
    oh, lse = oh.reshape(16, t, 64), lse.reshape(16, t, 64)
    attn_hm = sink_fwd(oh, lse, sink_b, f"od_sink{o}")
    attn_tm = attn_hm.transpose(1, 0, 2).reshape(t, A_W)
    prep_in = tuple(a[n][o] for n in ('od_a_re', 'od_a_im', 'od_b_re', 'od_b_im', 'od_c_re', 'od_c_im', 'od_log_dt'))
    (are, aim, bre, bim, cre, cim), prep_vjp = jax.vjp(s5_prep, *prep_in)
    s5w = (bre, bim, are, aim, cre, cim, fs['od_d'][o][None])
    y, sr, si = s5_fwd(proj, 10, *s5w, f"od_s5{o}")
    z = glu_z(y, f"od_glu_z{o}")
    gpre = mm(z, wf['od_glu_w'][o], name=f"od_glu_mm{o}")
    glu_b = fs['od_glu_b'][o][None]
    ssm = glu_out(y, gpre, glu_b, f"od_glu_out{o}")
    mix = jnp.concatenate([attn_tm, ssm], axis=1)
    yo = mm(mix, wf['od_w_out'][o], name=f"od_out{o}")
    return yo, dict(proj=proj, qh=qh, kh=kh, vh=vh, oh=oh, lse=lse, sink_b=sink_b, prep_vjp=prep_vjp, s5w=s5w, y=y, sr=sr, si=si,
                    z=z, gpre=gpre, glu_b=glu_b, mix=mix)


def _odd_bwd(o, dyg, s, h1, a, wf, fs, tabs, gs):
    cos, sin = tabs
    t = h1.shape[0]
    dmix = mm(dyg, wf['od_w_out'][o], tb=True, name=f"od_dmix{o}")
    dwo = mm(s['mix'], dyg, ta=True, out_dtype=BF16, name=f"od_dwo{o}")
    dgp, dglu_b = glu_bwd_gate(dmix, s['y'], s['gpre'], s['glu_b'], f"od_glu_bwd_gate{o}")
    dz_mm = mm(dgp, wf['od_glu_w'][o], tb=True, name=f"od_glu_dz{o}")
    dglu_w = mm(s['z'], dgp, ta=True, out_dtype=BF16, name=f"od_glu_dw{o}")
    dy = glu_bwd_y(dmix, s['y'], s['gpre'], s['glu_b'], dz_mm, f"od_glu_bwd_y{o}")
    du, dbre, dbim, dare, daim, dcre, dcim, dd = s5_bwd(s['proj'], 10, dy, s['sr'], s['si'], *s['s5w'], f"od_s5_bwd{o}")
    ga = s['prep_vjp']((dare, daim, dbre, dbim, dcre, dcim))
    for n, g in zip(('od_a_re', 'od_a_im', 'od_b_re', 'od_b_im', 'od_c_re', 'od_c_im', 'od_log_dt'), ga):
        gs[n][o] = g
    gs['od_d'][o] = dd[0]
    gs['od_glu_b'][o] = dglu_b[0]
    dattn_hm = dmix[:, :A_W].reshape(t, 16, 64).transpose(1, 0, 2)
    do, dvec, dsink = sink_bwd(dattn_hm, s['oh'], s['lse'], s['sink_b'], f"od_sink_bwd{o}")
    gs['od_sinks'][o] = dsink[:, 0, 0]
    dq, dk, dv = attn_bwd(s['qh'], s['kh'], s['vh'], do.reshape(2, 8, t, 64), s['lse'].reshape(2, 8, t, 64), dvec.reshape(2, 8, t, 64),
                          127, 64 ** -0.5, f"od_attn_bwd{o}")
    dq_tm = dq.transpose(2, 0, 1, 3).reshape(t, A_W)
    dk_tm = dk.transpose(1, 0, 2).reshape(t, LANE)
    dv_tm = dv.transpose(1, 0, 2).reshape(t, LANE)
    dproj = odd_dproj(dq_tm, dk_tm, dv_tm, du, cos, sin, f"od_dproj{o}")
    dh1 = mm(dproj, wf['od_w_in'][o], tb=True, name=f"od_dh{o}")
    dwi = mm(h1, dproj, ta=True, out_dtype=BF16, tm=2048, tn=768, name=f"od_dwi{o}")
    return dh1, dwi, dwo, dglu_w


def kernel(x, c, positions, ada_w, ada_b, norm_mix, norm_ffn, norm_final, ev_w_in, ev_conv_w, ev_conv_b, ev_gate_a_w, ev_gate_a_b, ev_gate_x_w, ev_gate_x_b, ev_lambda, ev_w_out, od_w_in, od_sinks, od_a_re, od_a_im, od_b_re, od_b_im, od_c_re, od_c_im, od_d, od_log_dt, od_glu_w, od_glu_b, od_w_out, ffn_w_in, ffn_conv_w, ffn_conv_b, ffn_w_out, loss_target, m_ada_w, m_ada_b, m_norm_mix, m_norm_ffn, m_norm_final, m_ev_w_in, m_ev_conv_w, m_ev_conv_b, m_ev_gate_a_w, m_ev_gate_a_b, m_ev_gate_x_w, m_ev_gate_x_b, m_ev_lambda, m_ev_w_out, m_od_w_in, m_od_sinks, m_od_a_re, m_od_a_im, m_od_b_re, m_od_b_im, m_od_c_re, m_od_c_im, m_od_d, m_od_log_dt, m_od_glu_w, m_od_glu_b, m_od_w_out, m_ffn_w_in, m_ffn_conv_w, m_ffn_conv_b, m_ffn_w_out, v_ada_w, v_ada_b, v_norm_mix, v_norm_ffn, v_norm_final, v_ev_w_in, v_ev_conv_w, v_ev_conv_b, v_ev_gate_a_w, v_ev_gate_a_b, v_ev_gate_x_w, v_ev_gate_x_b, v_ev_lambda, v_ev_w_out, v_od_w_in, v_od_sinks, v_od_a_re, v_od_a_im, v_od_b_re, v_od_b_im, v_od_c_re, v_od_c_im, v_od_d, v_od_log_dt, v_od_glu_w, v_od_glu_b, v_od_w_out, v_ffn_w_in, v_ffn_conv_w, v_ffn_conv_b, v_ffn_w_out):
    a = dict(locals())
    xi, yi, ci = _place()
    chip = 2 * xi + yi
    me = 2 * chip + ci
    x0, target, pos = x[0], loss_target[0], positions[0]
    d = D_MODEL

    g0 = all_gather8(_pack([c] + [a[n] for n in SMALL_SHARDED]), "gather_small").reshape(N_DEV, -1)
    c_all = g0[:, :d]
    fs, off = {}, d
    for n in SMALL_SHARDED:
        sh = a[n].shape
        parts = g0[0::2, off:off + math.prod(sh)].reshape((N_CHIP,) + sh)
        fs[n] = jnp.moveaxis(parts, 0, -2).reshape(sh[:-1] + (N_CHIP * sh[-1],))
        off += math.prod(sh)
    cond_all = silu_rows(c_all, "silu")

    modp = jnp.stack([mm(cond_all, ada_w[l], tm=8, tn=512, name=f"mod{l}") for l in range(DEPTH)])
    mod_all = all_gather8(modp.reshape(-1, LANE), "gather_mod").reshape(N_DEV, DEPTH, N_DEV, 6 * d // N_CHIP)[0::2]
    mod_me = lax.dynamic_index_in_dim(mod_all, me, axis=2, keepdims=False)
    mod = jnp.transpose(mod_me, (1, 0, 2)).reshape(DEPTH, 6 * d) + ada_b
    mods = [[mod[l, i * d:(i + 1) * d][None] for i in range(6)] for l in range(DEPTH)]

    full = gather_weights([a[n].astype(BF16) for n in BIG], "gather_weights")
    wf = {}
    for n, f in zip(BIG, full):
        _, nl, r, cc = f.shape
        if n in COL_SHARDED:
            wf[n] = jnp.transpose(f, (1, 2, 0, 3)).reshape(nl, r, N_CHIP * cc)
        else:
            wf[n] = jnp.transpose(f, (1, 0, 2, 3)).reshape(nl, N_CHIP * r, cc)

    tabs128 = rope_tables(pos, 64)
    tabs64 = rope_tables(pos, 32)

    saved = []
    xcur = x0
    for l in range(DEPTH):
        sh1, sc1, g1, sh2, sc2, g2 = mods[l]
        s = dict(x=xcur)
        s['h1'] = norm_mod(xcur, norm_mix[l][None], sc1, sh1, f"norm_mix{l}")
        if l % 2 == 0:
            s['y'], s['mixer'] = _even_fwd(l // 2, s['h1'], a, wf, fs, tabs128)
        else:
            s['y'], s['mixer'] = _odd_fwd(l // 2, s['h1'], a, wf, fs, tabs64)
        s['x2'], s['h2'] = resid_norm_mod(xcur, s['y'], g1, norm_ffn[l][None], sc2, sh2, f"norm_ffn{l}")
        s['f'], s['ffn'] = _ffn_fwd(l, s['h2'], wf, fs['ffn_conv_w'][l], ffn_conv_b[l][None])
        xcur = resid_add(s['x2'], s['f'], g2, f"resid{l}")
        saved.append(s)

    dx, loss_part, dnf = final_loss(xcur, norm_final[None], target, "loss")
    loss = lax.psum(loss_part[0, 0], ("x", "y", "c"))

    gs = {n: {} for n in SMALL}
    gbig = {n: {} for n in BIG}
    dmod = {}
    gs['norm_final'][0] = dnf[0]
    for l in reversed(range(DEPTH)):
        sh1, sc1, g1, sh2, sc2, g2 = mods[l]
        s = saved[l]
        df, dg2 = resid_bwd(dx, s['f'], g2, f"resid_bwd_ffn{l}")
        dh2, dwi, dwo, dcw, dcb = _ffn_bwd(l, df, s['ffn'], s['h2'], wf, fs['ffn_conv_w'][l], ffn_conv_b[l][None])
        gbig['ffn_w_in'][l], gbig['ffn_w_out'][l], gs['ffn_conv_w'][l], gs['ffn_conv_b'][l] = dwi, dwo, dcw, dcb[0]
        dx2, dsh2, dsc2, dgam2 = norm_mod_bwd(dh2, s['x2'], dx, norm_ffn[l][None], sc2, f"norm_ffn_bwd{l}")
        gs['norm_ffn'][l] = dgam2[0]
        dyg, dg1 = resid_bwd(dx2, s['y'], g1, f"resid_bwd_mix{l}")
        if l % 2 == 0:
            dh1, dwi, dwo = _even_bwd(l // 2, dyg, s['mixer'], s['h1'], a, wf, fs, tabs128, gs)
            gbig['ev_w_in'][l // 2], gbig['ev_w_out'][l // 2] = dwi, dwo
        else:
            dh1, dwi, dwo, dglu_w = _odd_bwd(l // 2, dyg, s['mixer'], s['h1'], a, wf, fs, tabs64, gs)
            gbig['od_w_in'][l // 2], gbig['od_w_out'][l // 2], gbig['od_glu_w'][l // 2] = dwi, dwo, dglu_w
        dx, dsh1, dsc1, dgam1 = norm_mod_bwd(dh1, s['x'], dx2, norm_mix[l][None], sc1, f"norm_mix_bwd{l}")
        gs['norm_mix'][l] = dgam1[0]
        dmod[l] = jnp.concatenate([dsh1, dsc1, dg1, dsh2, dsc2, dg2], axis=1)[0]
    grad_x = dx[None]
    gs['ada_b'] = dmod

    grads = {}
    g4 = []
    for n in BIG:
        g = jnp.stack([gbig[n][i] for i in range(len(gbig[n]))])
        nl = g.shape[0]
        if n in COL_SHARDED:
            g4.append(g.reshape(nl, g.shape[1], N_CHIP, g.shape[2] // N_CHIP).transpose(2, 0, 1, 3))
        else:
            g4.append(g.reshape(nl, N_CHIP, g.shape[1] // N_CHIP, g.shape[2]).transpose(1, 0, 2, 3))
    pieces = exchange_pieces(g4, "exchange_grads")
    halves = [sum_devices(p, f"sum_{n}") for n, p in zip(BIG, pieces)]
    for n, g in zip(BIG, join_halves(halves, "join_grads")):
        grads[n] = g.reshape(a[n].shape)

    small_full = [jnp.stack([gs[n][i] for i in range(len(gs[n]))]) if n != 'norm_final' else gs[n][0] for n in SMALL]
    small_shapes = [g.shape for g in small_full]
    gs_all = all_gather8(_pack(small_full), "gather_small_grads")
    gs_sum = sum_devices(gs_all[:, None], "sum_small").reshape(-1)
    for n, g in zip(SMALL, _unpack(gs_sum, small_shapes)):
        if n in SMALL_SHARDED:
            w = a[n].shape[-1]
            g = lax.dynamic_slice_in_dim(g, chip * w, w, axis=g.ndim - 1)
        grads[n] = g
    assert SMALL[0] == 'ada_b'
    dmod_all = gs_all.reshape(N_DEV, -1)[:, :DEPTH * 6 * d].reshape(N_DEV, DEPTH, 6 * d)
    wcols = 6 * d // N_CHIP
    grads['ada_w'] = jnp.stack([
        mm(cond_all, lax.dynamic_slice_in_dim(dmod_all[:, l], chip * wcols, wcols, axis=1), ta=True, tm=2048, tn=512, name=f"ada_dw{l}")
        for l in range(DEPTH)])

    delta, new_m, new_v = {}, {}, {}
    for n in ['ada_w'] + BIG:
        sh = a[n].shape
        two_d = lambda t: t.reshape(-1, sh[-1])
        dl, nm, nv = adamw(two_d(a[n]), two_d(grads[n]), two_d(a['m_' + n]), two_d(a['v_' + n]), f"adamw_{n}")
        delta[n], new_m[n], new_v[n] = dl.reshape(sh), nm.reshape(sh), nv.reshape(sh)
    shapes = [a[n].shape for n in SMALL]
    dl, nm, nv = adamw(_pack([a[n] for n in SMALL]), _pack([grads[n] for n in SMALL]), _pack([a['m_' + n] for n in SMALL]),
                       _pack([a['v_' + n] for n in SMALL]), "adamw_small")
    for n, t1, t2, t3 in zip(SMALL, _unpack(dl.reshape(-1), shapes), _unpack(nm.reshape(-1), shapes), _unpack(nv.reshape(-1), shapes)):
        delta[n], new_m[n], new_v[n] = t1, t2, t3

    return (loss, grad_x, *[grads[n] for n in WEIGHTS], *[delta[n] for n in WEIGHTS], *[new_m[n] for n in WEIGHTS],
            *[new_v[n] for n in WEIGHTS])
```

```python
import functools
import math

import jax
import jax.numpy as jnp
from jax import lax
from jax.experimental import pallas as pl
from jax.experimental.pallas import tpu as pltpu

F32 = jnp.float32
BF16 = jnp.bfloat16
MESH = pl.DeviceIdType.MESH

D_MODEL = 2048
SEQ = 2048
DEPTH = 4
N_DEV = 8
N_CHIP = 4
BLK = 128
LANE = 128
V7X_VMEM_LIMIT = 56 * 1024 * 1024
NORM_EPS = 1e-6
ROPE_THETA = 10000.0
LRU_C = 8.0
D_FF = 5504
A_PATTERNS = ((128, 1), (512, 4), (2048, 16))
ADAM_LR, ADAM_B1, ADAM_B2, ADAM_EPS, ADAM_WD, ADAM_STEP = 0.001, 0.9, 0.999, 1e-08, 0.01, 10
NEG = -1e30


def _params(n_grid):
    return pltpu.CompilerParams(dimension_semantics=("arbitrary",) * n_grid, vmem_limit_bytes=V7X_VMEM_LIMIT)


def _pick(dim, pref):
    best = None
    for t in range(LANE, min(dim, pref) + 1, LANE):
        if dim % t == 0:
            best = t
    return best or dim


def _sigmoid(x):
    return 1.0 / (1.0 + jnp.exp(-x))


_GELU_C = math.sqrt(2.0 / math.pi)


def _gelu(x):
    t = jnp.tanh(_GELU_C * (x + 0.044715 * (x * x * x)))
    return 0.5 * x * (1.0 + t)


def _gelu_grad(x):
    t = jnp.tanh(_GELU_C * (x + 0.044715 * (x * x * x)))
    return 0.5 * (1.0 + t) + 0.5 * x * (1.0 - t * t) * (_GELU_C * (1.0 + 3.0 * 0.044715 * (x * x)))


def _softplus(x):
    return jnp.maximum(x, 0.0) + jnp.log(1.0 + jnp.exp(-jnp.abs(x)))


def _neg_expm1(x):
    series = -x * (1.0 + x * (0.5 + x * (1.0 / 6.0 + x * (1.0 / 24.0))))
    return jnp.where(x > -0.03, series, 1.0 - jnp.exp(x))


def _shift_down(x, k):
    if k == 0:
        return x
    row = lax.broadcasted_iota(jnp.int32, x.shape, 0)
    return jnp.where(row >= k, pltpu.roll(x, k, 0), 0.0)


def _shift_up(x, k):
    if k == 0:
        return x
    n = x.shape[0]
    row = lax.broadcasted_iota(jnp.int32, x.shape, 0)
    return jnp.where(row < n - k, pltpu.roll(x, n - k, 0), 0.0)


def _dot(a, b, ca=1, cb=0):
    return lax.dot_general(a.astype(BF16), b.astype(BF16), (((ca,), (cb,)), ((), ())), preferred_element_type=F32)


def _rope(x, cos, sin_signed, half):
    c = x.shape[1]
    reps = c // cos.shape[1]
    cos_c = jnp.tile(cos, (1, reps)) if reps > 1 else cos
    sin_c = jnp.tile(sin_signed, (1, reps)) if reps > 1 else sin_signed
    lane = lax.broadcasted_iota(jnp.int32, x.shape, 1)
    first = (lane % (2 * half)) < half
    partner = jnp.where(first, pltpu.roll(x, c - half, 1), pltpu.roll(x, half, 1))
    return x * cos_c + partner * sin_c


def _rope_t(dy, cos, sin_signed, half):
    c = dy.shape[1]
    reps = c // cos.shape[1]
    cos_c = jnp.tile(cos, (1, reps)) if reps > 1 else cos
    sin_c = jnp.tile(sin_signed, (1, reps)) if reps > 1 else sin_signed
    lane = lax.broadcasted_iota(jnp.int32, dy.shape, 1)
    first = (lane % (2 * half)) < half
    ys = dy * sin_c
    partner = jnp.where(first, pltpu.roll(ys, c - half, 1), pltpu.roll(ys, half, 1))
    return dy * cos_c + partner


def _tile_call(name, fn, grid, ins, outs, acc_axis=None):
    n_in = len(ins)
    accs = [o[4] for o in outs]

    def body(*refs):
        vals = fn(*[r[...] for r in refs[:n_in]])
        if not isinstance(vals, (tuple, list)):
            vals = (vals,)
        for r, v, acc in zip(refs[n_in:], vals, accs):
            if acc:
                first = pl.program_id(acc_axis) == 0

                @pl.when(first)
                def _():
                    r[...] = v.astype(r.dtype)

                @pl.when(jnp.logical_not(first))
                def _():
                    r[...] += v.astype(r.dtype)
            else:
                r[...] = v.astype(r.dtype)

    res = pl.pallas_call(
        body, name=name, grid=grid,
        in_specs=[pl.BlockSpec(b, im) for _, b, im in ins],
        out_specs=[pl.BlockSpec(o[2], o[3]) for o in outs],
        out_shape=[jax.ShapeDtypeStruct(o[0], o[1]) for o in outs],
        compiler_params=_params(len(grid)),
    )(*[a for a, _, _ in ins])
    return res


def _row_tile(cols, n_arrays, rows):
    budget = 24 * 1024 * 1024 // (2 * 4 * max(n_arrays, 1) * cols)
    t = 8
    while t * 2 <= budget and rows % (t * 2) == 0 and t * 2 <= 1024:
        t *= 2
    return t


def _rows_call(name, fn, rows, tiled, full, outs_tiled, outs_acc=()):
    cols = max([a.shape[1] for a in tiled] + [c for c, _ in outs_tiled])
    tt = _row_tile(cols, len(tiled) + len(outs_tiled), rows)
    ins = [(a, (tt, a.shape[1]), lambda i: (i, 0)) for a in tiled]
    ins += [(a, a.shape, (lambda nd: (lambda i: (0,) * nd))(a.ndim)) for a in full]
    outs = [((rows, c), dt, (tt, c), lambda i: (i, 0), False) for c, dt in outs_tiled]
    outs += [(s, dt, s, (lambda nd: (lambda i: (0,) * nd))(len(s)), True) for s, dt in outs_acc]
    return _tile_call(name, fn, (rows // tt,), ins, outs, acc_axis=0)


def mm(a, b, *, ta=False, tb=False, out_dtype=F32, tm=None, tn=None, tk=None, name):
    m, k = (a.shape[1], a.shape[0]) if ta else a.shape
    n = b.shape[0] if tb else b.shape[1]
    tm = tm or _pick(m, 1024)
    tn = tn or _pick(n, 1024)
    tk = tk or k
    assert m % tm == 0 and n % tn == 0 and k % tk == 0, (name, m, n, k, tm, tn, tk)
    nk = k // tk
    a_spec = pl.BlockSpec((tk, tm), lambda i, j, kk: (kk, i)) if ta else pl.BlockSpec((tm, tk), lambda i, j, kk: (i, kk))
    b_spec = pl.BlockSpec((tn, tk), lambda i, j, kk: (j, kk)) if tb else pl.BlockSpec((tk, tn), lambda i, j, kk: (kk, j))
    ca, cb = (0 if ta else 1), (1 if tb else 0)

    def body(a_ref, b_ref, o_ref, *scratch):
        p = _dot(a_ref[...], b_ref[...], ca, cb)
        if nk == 1:
            o_ref[...] = p.astype(o_ref.dtype)
        else:
            acc = scratch[0]
            kk = pl.program_id(2)

            @pl.when(kk == 0)
            def _():
                acc[...] = p

            @pl.when(kk > 0)
            def _():
                acc[...] += p

            @pl.when(kk == nk - 1)
            def _():
                o_ref[...] = acc[...].astype(o_ref.dtype)

    return pl.pallas_call(
        body, name=name, grid=(m // tm, n // tn, nk),
        in_specs=[a_spec, b_spec],
        out_specs=pl.BlockSpec((tm, tn), lambda i, j, kk: (i, j)),
        out_shape=jax.ShapeDtypeStruct((m, n), out_dtype),
        scratch_shapes=[pltpu.VMEM((tm, tn), F32)] if nk > 1 else [],
        compiler_params=_params(3),
    )(a, b)


def _rstd(x):
    return lax.rsqrt(jnp.mean(x * x, axis=-1, keepdims=True) + NORM_EPS)


def norm_mod(x, gamma, sc, sh, name):
    def fn(x, gamma, sc, sh):
        return (x * _rstd(x)) * gamma * (1.0 + sc) + sh
    return _rows_call(name, fn, x.shape[0], [x], [gamma, sc, sh], [(x.shape[1], BF16)])[0]


def resid_norm_mod(x, y, g, gamma, sc, sh, name):
    def fn(x, y, g, gamma, sc, sh):
        x2 = x + g * y
        return x2, (x2 * _rstd(x2)) * gamma * (1.0 + sc) + sh
    return _rows_call(name, fn, x.shape[0], [x, y], [g, gamma, sc, sh], [(x.shape[1], F32), (x.shape[1], BF16)])


def resid_add(x, y, g, name):
    def fn(x, y, g):
        return x + g * y
    return _rows_call(name, fn, x.shape[0], [x, y], [g], [(x.shape[1], F32)])[0]


def resid_bwd(dxo, f, g, name):
    def fn(dxo, f, g):
        return dxo * g, jnp.sum(dxo * f, axis=0, keepdims=True)
    d = dxo.shape[1]
    return _rows_call(name, fn, dxo.shape[0], [dxo, f], [g], [(d, BF16)], [((1, d), F32)])


def norm_mod_bwd(dh, x, dres, gamma, sc, name):
    def fn(dh, x, dres, gamma, sc):
        rstd = _rstd(x)
        xhat = x * rstd
        dxhat = dh * (gamma * (1.0 + sc))
        dx = rstd * (dxhat - xhat * jnp.mean(dxhat * xhat, axis=-1, keepdims=True))
        dhx = dh * xhat
        return (dres + dx, jnp.sum(dh, axis=0, keepdims=True), jnp.sum(dhx * gamma, axis=0, keepdims=True),
                jnp.sum(dhx * (1.0 + sc), axis=0, keepdims=True))
    d = x.shape[1]
    return _rows_call(name, fn, x.shape[0], [dh, x, dres], [gamma, sc], [(d, F32)], [((1, d), F32)] * 3)


def final_loss(x, gamma, target, name):
    d = x.shape[1]

    def fn(x, target, gamma):
        rstd = _rstd(x)
        xhat = x * rstd
        e = xhat * gamma - target
        part = 0.5 * jnp.sum(jnp.sum(e * e, axis=-1, keepdims=True) / d, axis=0, keepdims=True)
        dy = e / d
        dxhat = dy * gamma
        dx = rstd * (dxhat - xhat * jnp.mean(dxhat * xhat, axis=-1, keepdims=True))
        return dx, jnp.broadcast_to(part, (8, LANE)), jnp.sum(dy * xhat, axis=0, keepdims=True)
    return _rows_call(name, fn, x.shape[0], [x, target], [gamma], [(d, F32)], [((8, LANE), F32), ((1, d), F32)])


def _conv3(u, w, b):
    return b + w[2:3] * u + w[1:2] * _shift_down(u, 1) + w[0:1] * _shift_down(u, 2)


def ileave(x):
    lead = x.shape[:-1]
    nd = len(lead)
    return x.reshape(lead + (2, D_FF // LANE, LANE)).transpose(tuple(range(nd)) + (nd + 1, nd, nd + 2)).reshape(lead + (2 * D_FF,))


def unileave(x):
    lead = x.shape[:-1]
    nd = len(lead)
    return x.reshape(lead + (D_FF // LANE, 2, LANE)).transpose(tuple(range(nd)) + (nd + 1, nd, nd + 2)).reshape(lead + (2 * D_FF,))


def ffn_act(u, cw, cb, name):
    t = u.shape[0]
    nb = D_FF // LANE

    def fn(ub, wb, bb):
        return _gelu(_conv3(ub[:, :LANE], wb[:, :LANE], bb[:, :LANE])) * _conv3(ub[:, LANE:], wb[:, LANE:], bb[:, LANE:])
    ins = [(u, (t, 2 * LANE), lambda j: (0, j)), (cw, (3, 2 * LANE), lambda j: (0, j)), (cb, (1, 2 * LANE), lambda j: (0, j))]
    return _tile_call(name, fn, (nb,), ins, [((t, D_FF), BF16, (t, LANE), lambda j: (0, j), False)])[0]


def ffn_act_bwd(u, dact, cw, cb, name):
    t = u.shape[0]
    nb = D_FF // LANE

    def fn(ub, wb, bb, da):
        g = _conv3(ub[:, :LANE], wb[:, :LANE], bb[:, :LANE])
        v = _conv3(ub[:, LANE:], wb[:, LANE:], bb[:, LANE:])
        duc = jnp.concatenate([da * v * _gelu_grad(g), da * _gelu(g)], axis=1)
        du = wb[2:3] * duc + wb[1:2] * _shift_up(duc, 1) + wb[0:1] * _shift_up(duc, 2)
        dw = jnp.concatenate([jnp.sum(duc * _shift_down(ub, 2), axis=0, keepdims=True),
                              jnp.sum(duc * _shift_down(ub, 1), axis=0, keepdims=True),
                              jnp.sum(duc * ub, axis=0, keepdims=True)], axis=0)
        return du, dw, jnp.sum(duc, axis=0, keepdims=True)
    ins = [(u, (t, 2 * LANE), lambda j: (0, j)), (cw, (3, 2 * LANE), lambda j: (0, j)), (cb, (1, 2 * LANE), lambda j: (0, j)),
           (dact, (t, LANE), lambda j: (0, j))]
    outs = [((t, 2 * D_FF), BF16, (t, 2 * LANE), lambda j: (0, j), False),
            ((3, 2 * D_FF), F32, (3, 2 * LANE), lambda j: (0, j), False),
            ((1, 2 * D_FF), F32, (1, 2 * LANE), lambda j: (0, j), False)]
    return _tile_call(name, fn, (nb,), ins, outs)


def attn_fwd(q, k, v, max_dist, scale, name):
    n, r, l, dh = q.shape
    nb = l // BLK
    m_rows = r * BLK

    def body(q_ref, kc_ref, kp_ref, vc_ref, vp_ref, o_ref, lse_ref):
        b = pl.program_id(1)
        qv = q_ref[...].reshape(m_rows, dh)
        s_c = _dot(qv, kc_ref[...], 1, 1) * scale
        s_p = _dot(qv, kp_ref[...], 1, 1) * scale
        qi = lax.broadcasted_iota(jnp.int32, (m_rows, BLK), 0) % BLK
        kj = lax.broadcasted_iota(jnp.int32, (m_rows, BLK), 1)
        s_c = jnp.where(kj <= qi, s_c, NEG)
        s_p = jnp.where((kj >= qi + (BLK - max_dist)) & (b > 0), s_p, NEG)
        mx = jnp.maximum(jnp.max(s_c, axis=1, keepdims=True), jnp.max(s_p, axis=1, keepdims=True))
        p_c = jnp.exp(s_c - mx)
        p_p = jnp.exp(s_p - mx)
        den = jnp.sum(p_c, axis=1, keepdims=True) + jnp.sum(p_p, axis=1, keepdims=True)
        o = (_dot(p_c, vc_ref[...]) + _dot(p_p, vp_ref[...])) / den
        o_ref[...] = o.reshape(r, BLK, dh)
        lse_ref[...] = jnp.broadcast_to(mx + jnp.log(den), (m_rows, dh)).reshape(r, BLK, dh)

    qspec = pl.BlockSpec((None, r, BLK, dh), lambda i, b: (i, 0, b, 0))
    cur = pl.BlockSpec((None, BLK, dh), lambda i, b: (i, b, 0))
    prev = pl.BlockSpec((None, BLK, dh), lambda i, b: (i, jnp.maximum(b - 1, 0), 0))
    return pl.pallas_call(
        body, name=name, grid=(n, nb),
        in_specs=[qspec, cur, prev, cur, prev],
        out_specs=[qspec, qspec],
        out_shape=[jax.ShapeDtypeStruct((n, r, l, dh), F32)] * 2,
        compiler_params=_params(2),
    )(q, k, k, v, v)


def attn_bwd(q, k, v, do, lse, dvec, max_dist, scale, name):
    n, r, l, dh = q.shape
    nb = l // BLK
    m_rows = r * BLK

    def body(qc_ref, qn_ref, kc_ref, kp_ref, vc_ref, vp_ref, doc_ref, don_ref, lc_ref, ln_ref, dc_ref, dn_ref,
             dq_ref, dk_ref, dv_ref):
        b = pl.program_id(1)
        qi = lax.broadcasted_iota(jnp.int32, (m_rows, BLK), 0) % BLK
        kj = lax.broadcasted_iota(jnp.int32, (m_rows, BLK), 1)
        m_cur = kj <= qi
        m_prev = kj >= qi + (BLK - max_dist)

        def pair(q_ref, do_ref, l_ref, d_ref, k_ref, v_ref, mask):
            qv = q_ref[...].reshape(m_rows, dh)
            dov = do_ref[...].reshape(m_rows, dh)
            lrow = l_ref[...].reshape(m_rows, dh)[:, 0:1]
            drow = d_ref[...].reshape(m_rows, dh)[:, 0:1]
            s = _dot(qv, k_ref[...], 1, 1) * scale
            p = jnp.where(mask, jnp.exp(jnp.where(mask, s, NEG) - lrow), 0.0)
            dp = _dot(dov, v_ref[...], 1, 1)
            ds = p * (dp - drow) * scale
            return qv, dov, p, ds

        q_a, do_a, p_a, ds_a = pair(qc_ref, doc_ref, lc_ref, dc_ref, kc_ref, vc_ref, m_cur)
        _, _, _, ds_b = pair(qc_ref, doc_ref, lc_ref, dc_ref, kp_ref, vp_ref, m_prev & (b > 0))
        q_c, do_c, p_c, ds_c = pair(qn_ref, don_ref, ln_ref, dn_ref, kc_ref, vc_ref, m_prev & (b < nb - 1))
        dq = _dot(ds_a, kc_ref[...]) + _dot(ds_b, kp_ref[...])
        dq_ref[...] = dq.reshape(r, BLK, dh)
        dk_ref[...] = _dot(ds_a, q_a, 0, 0) + _dot(ds_c, q_c, 0, 0)
        dv_ref[...] = _dot(p_a, do_a, 0, 0) + _dot(p_c, do_c, 0, 0)

    qcur = pl.BlockSpec((None, r, BLK, dh), lambda i, b: (i, 0, b, 0))
    qnext = pl.BlockSpec((None, r, BLK, dh), lambda i, b: (i, 0, jnp.minimum(b + 1, nb - 1), 0))
    cur = pl.BlockSpec((None, BLK, dh), lambda i, b: (i, b, 0))
    prev = pl.BlockSpec((None, BLK, dh), lambda i, b: (i, jnp.maximum(b - 1, 0), 0))
    return pl.pallas_call(
        body, name=name, grid=(n, nb),
        in_specs=[qcur, qnext, cur, prev, cur, prev, qcur, qnext, qcur, qnext, qcur, qnext],
        out_specs=[qcur, cur, cur],
        out_shape=[jax.ShapeDtypeStruct((n, r, l, dh), F32), jax.ShapeDtypeStruct((n, l, dh), F32),
                   jax.ShapeDtypeStruct((n, l, dh), F32)],
        compiler_params=_params(2),
    )(q, q, k, k, v, v, do, do, lse, lse, dvec, dvec)


def _scan_rows(t_len, step, init, reverse=False):
    n_chunks = t_len // 8

    def chunk(ci, carry):
        c = (n_chunks - 1 - ci) if reverse else ci
        base = pl.multiple_of(c * 8, 8)
        order = range(7, -1, -1) if reverse else range(8)
        return step(base, order, carry)
    return lax.fori_loop(0, n_chunks, chunk, init)


def _put_row(acc, i, row):
    rid = lax.broadcasted_iota(jnp.int32, acc.shape, 0)
    return jnp.where(rid == i, row, acc)


def _real_scan(a_ref, b_ref, h_ref, t_len, reverse=False):
    c = a_ref.shape[1]

    def step(base, order, h):
        a8 = a_ref[pl.ds(base, 8), :]
        b8 = b_ref[pl.ds(base, 8), :]
        out = jnp.zeros((8, c), F32)
        for i in order:
            h = a8[i:i + 1, :] * h + b8[i:i + 1, :]
            out = _put_row(out, i, h)
        h_ref[pl.ds(base, 8), :] = out
        return h
    _scan_rows(t_len, step, jnp.zeros((1, c), F32), reverse)


def _complex_scan(ar, ai, br_ref, bi_ref, sr_ref, si_ref, t_len, reverse=False):
    c = br_ref.shape[1]

    def step(base, order, carry):
        sr, si = carry
        br8 = br_ref[pl.ds(base, 8), :]
        bi8 = bi_ref[pl.ds(base, 8), :]
        outr = jnp.zeros((8, c), F32)
        outi = jnp.zeros((8, c), F32)
        for i in order:
            nr = ar * sr - ai * si + br8[i:i + 1, :]
            ni = ar * si + ai * sr + bi8[i:i + 1, :]
            sr, si = nr, ni
            outr = _put_row(outr, i, sr)
            outi = _put_row(outi, i, si)
        sr_ref[pl.ds(base, 8), :] = outr
        si_ref[pl.ds(base, 8), :] = outi
        return sr, si
    _scan_rows(t_len, step, (jnp.zeros((1, c), F32), jnp.zeros((1, c), F32)), reverse)


def _rglru_pre(xb, cw, cb, gaw, gab, gxw, gxb, lam):
    xc = cb + cw[3:4] * xb + cw[2:3] * _shift_down(xb, 1) + cw[1:2] * _shift_down(xb, 2) + cw[0:1] * _shift_down(xb, 3)
    r = _sigmoid(_dot(xc, gaw) + gab)
    ig = _sigmoid(_dot(xc, gxw) + gxb)
    sp = _softplus(-lam)
    log_a = -LRU_C * r * sp
    a = jnp.exp(log_a)
    mult = jnp.sqrt(_neg_expm1(2.0 * log_a))
    return xc, r, ig, sp, a, mult


def rglru_fwd(proj, cw, cb, gaw, gab, gxw, gxb, lam, xb_col, yb_col, name):
    t = proj.shape[0]
    nh = cw.shape[1] // LANE

    def body(xb_ref, yb_ref, cw_ref, cb_ref, gaw_ref, gab_ref, gxw_ref, gxb_ref, lam_ref, out_ref, h_ref, a_scr, b_scr):
        xc, r, ig, sp, a, mult = _rglru_pre(xb_ref[...], cw_ref[...], cb_ref[...], gaw_ref[...], gab_ref[...],
                                            gxw_ref[...], gxb_ref[...], lam_ref[...])
        a_scr[...] = a
        b_scr[...] = mult * (ig * xc)
        _real_scan(a_scr, b_scr, h_ref, t)
        out_ref[...] = (h_ref[...] * _gelu(yb_ref[...])).astype(out_ref.dtype)

    col = lambda off: pl.BlockSpec((t, LANE), lambda h: (0, off + h))
    vec = lambda rows: pl.BlockSpec((rows, LANE), lambda h: (0, h))
    wsp = pl.BlockSpec((None, LANE, LANE), lambda h: (h, 0, 0))
    return pl.pallas_call(
        body, name=name, grid=(nh,),
        in_specs=[col(xb_col), col(yb_col), vec(4), vec(1), wsp, vec(1), wsp, vec(1), vec(1)],
        out_specs=[col(0), col(0)],
        out_shape=[jax.ShapeDtypeStruct((t, nh * LANE), BF16), jax.ShapeDtypeStruct((t, nh * LANE), F32)],
        scratch_shapes=[pltpu.VMEM((t, LANE), F32)] * 2,
        compiler_params=_params(1),
    )(proj, proj, cw, cb, gaw, gab, gxw, gxb, lam)


def rglru_bwd(proj, hs, dlru, dlru_col, cw, cb, gaw, gab, gxw, gxb, lam, xb_col, yb_col, name):
    t = proj.shape[0]
    nh = cw.shape[1] // LANE

    def body(xb_ref, yb_ref, h_ref, dl_ref, cw_ref, cb_ref, gaw_ref, gab_ref, gxw_ref, gxb_ref, lam_ref,
             dxb_ref, dyb_ref, dcw_ref, dcb_ref, dgaw_ref, dgab_ref, dgxw_ref, dgxb_ref, dlam_ref, an_scr, dh_scr, gh_scr):
        xb = xb_ref[...]
        yb = yb_ref[...]
        cwv = cw_ref[...]
        lam = lam_ref[...]
        xc, r, ig, sp, a, mult = _rglru_pre(xb, cwv, cb_ref[...], gaw_ref[...], gab_ref[...], gxw_ref[...], gxb_ref[...], lam)
        h = h_ref[...]
        dl = dl_ref[...]
        dyb_ref[...] = (dl * h * _gelu_grad(yb)).astype(dyb_ref.dtype)
        dh_scr[...] = dl * _gelu(yb)
        an_scr[...] = _shift_up(a, 1)
        _real_scan(an_scr, dh_scr, gh_scr, t, reverse=True)
        gh = gh_scr[...]
        da = gh * _shift_down(h, 1)
        dmult = gh * ig * xc
        dig = gh * mult * xc
        dxc = gh * mult * ig
        dla = (da - dmult * a / mult) * a
        dr = dla * (-LRU_C * sp)
        dsp = jnp.sum(dla * (-LRU_C * r), axis=0, keepdims=True)
        dlam_ref[...] = dsp * (-_sigmoid(-lam))
        dpr = dr * r * (1.0 - r)
        dpi = dig * ig * (1.0 - ig)
        dgab_ref[...] = jnp.sum(dpr, axis=0, keepdims=True)
        dgxb_ref[...] = jnp.sum(dpi, axis=0, keepdims=True)
        dgaw_ref[...] = _dot(xc, dpr, 0, 0)
        dgxw_ref[...] = _dot(xc, dpi, 0, 0)
        dxc = dxc + _dot(dpr, gaw_ref[...], 1, 1) + _dot(dpi, gxw_ref[...], 1, 1)
        dxb = cwv[3:4] * dxc + cwv[2:3] * _shift_up(dxc, 1) + cwv[1:2] * _shift_up(dxc, 2) + cwv[0:1] * _shift_up(dxc, 3)
        dxb_ref[...] = dxb.astype(dxb_ref.dtype)
        dcw_ref[...] = jnp.concatenate([jnp.sum(dxc * _shift_down(xb, 3 - i), axis=0, keepdims=True) for i in range(4)], axis=0)
        dcb_ref[...] = jnp.sum(dxc, axis=0, keepdims=True)

    col = lambda off: pl.BlockSpec((t, LANE), lambda h: (0, off + h))
    vec = lambda rows: pl.BlockSpec((rows, LANE), lambda h: (0, h))
    wsp = pl.BlockSpec((None, LANE, LANE), lambda h: (h, 0, 0))
    w = nh * LANE
    sds = jax.ShapeDtypeStruct
    return pl.pallas_call(
        body, name=name, grid=(nh,),
        in_specs=[col(xb_col), col(yb_col), col(0), col(dlru_col), vec(4), vec(1), wsp, vec(1), wsp, vec(1), vec(1)],
        out_specs=[col(0), col(0), vec(4), vec(1), wsp, vec(1), wsp, vec(1), vec(1)],
        out_shape=[sds((t, w), BF16), sds((t, w), BF16), sds((4, w), F32), sds((1, w), F32), sds((nh, LANE, LANE), F32),
                   sds((1, w), F32), sds((nh, LANE, LANE), F32), sds((1, w), F32), sds((1, w), F32)],
        scratch_shapes=[pltpu.VMEM((t, LANE), F32)] * 3,
        compiler_params=_params(1),
    )(proj, proj, hs, dlru, cw, cb, gaw, gab, gxw, gxb, lam)


S5_CHUNKS = 8
S5_STATES = 512


def s5_fwd(proj, u_col, bre, bim, are, aim, cre, cim, dsk, name):
    t = proj.shape[0]

    def body(u_ref, bre_ref, bim_ref, are_ref, aim_ref, cre_ref, cim_ref, d_ref, y_ref, sr_ref, si_ref, br_scr, bi_scr):
        u = u_ref[...]
        br_scr[...] = _dot(u, bre_ref[...])
        bi_scr[...] = _dot(u, bim_ref[...])
        _complex_scan(are_ref[...], aim_ref[...], br_scr, bi_scr, sr_ref, si_ref, t)
        y_ref[...] = _dot(sr_ref[...], cre_ref[...]) - _dot(si_ref[...], cim_ref[...]) + d_ref[...] * u

    ucol = pl.BlockSpec((t, LANE), lambda c: (0, u_col + c))
    ycol = pl.BlockSpec((t, LANE), lambda c: (0, c))
    scol = pl.BlockSpec((t, S5_STATES), lambda c: (0, c))
    bsp = pl.BlockSpec((None, LANE, S5_STATES), lambda c: (c, 0, 0))
    csp = pl.BlockSpec((None, S5_STATES, LANE), lambda c: (c, 0, 0))
    asp = pl.BlockSpec((1, S5_STATES), lambda c: (0, c))
    dsp = pl.BlockSpec((1, LANE), lambda c: (0, c))
    sds = jax.ShapeDtypeStruct
    return pl.pallas_call(
        body, name=name, grid=(S5_CHUNKS,),
        in_specs=[ucol, bsp, bsp, asp, asp, csp, csp, dsp],
        out_specs=[ycol, scol, scol],
        out_shape=[sds((t, S5_CHUNKS * LANE), F32), sds((t, S5_CHUNKS * S5_STATES), F32), sds((t, S5_CHUNKS * S5_STATES), F32)],
        scratch_shapes=[pltpu.VMEM((t, S5_STATES), F32)] * 2,
        compiler_params=_params(1),
    )(proj, bre, bim, are, aim, cre, cim, dsk)


def s5_bwd(proj, u_col, dy, sr, si, bre, bim, are, aim, cre, cim, dsk, name):
    t = proj.shape[0]
    half = S5_STATES // 2

    def body(u_ref, dy_ref, sr_ref, si_ref, bre_ref, bim_ref, are_ref, aim_ref, cre_ref, cim_ref, d_ref,
             du_ref, dbre_ref, dbim_ref, dare_ref, daim_ref, dcre_ref, dcim_ref, dd_ref, dsr_scr, dsi_scr, gr_scr, gi_scr):
        hh = pl.program_id(1)
        u = u_ref[...]
        dy = dy_ref[...]
        dsr_scr[...] = _dot(dy, cre_ref[...], 1, 1)
        dsi_scr[...] = -_dot(dy, cim_ref[...], 1, 1)
        _complex_scan(are_ref[...], -aim_ref[...], dsr_scr, dsi_scr, gr_scr, gi_scr, t, reverse=True)
        gr = gr_scr[...]
        gi = gi_scr[...]
        spr = _shift_down(sr_ref[...], 1)
        spi = _shift_down(si_ref[...], 1)
        dare_ref[...] = jnp.sum(gr * spr + gi * spi, axis=0, keepdims=True)
        daim_ref[...] = jnp.sum(gi * spr - gr * spi, axis=0, keepdims=True)
        du = _dot(gr, bre_ref[...], 1, 1) + _dot(gi, bim_ref[...], 1, 1)

        @pl.when(hh == 0)
        def _():
            du_ref[...] = du + d_ref[...] * dy

        @pl.when(hh > 0)
        def _():
            du_ref[...] += du

        dbre_ref[...] = _dot(u, gr, 0, 0)
        dbim_ref[...] = _dot(u, gi, 0, 0)
        dcre_ref[...] = _dot(sr_ref[...], dy, 0, 0)
        dcim_ref[...] = -_dot(si_ref[...], dy, 0, 0)
        dd_ref[...] = jnp.sum(dy * u, axis=0, keepdims=True)

    ucol = pl.BlockSpec((t, LANE), lambda c, h: (0, u_col + c))
    ycol = pl.BlockSpec((t, LANE), lambda c, h: (0, c))
    scol = pl.BlockSpec((t, half), lambda c, h: (0, 2 * c + h))
    bsp = pl.BlockSpec((None, LANE, half), lambda c, h: (c, 0, h))
    csp = pl.BlockSpec((None, half, LANE), lambda c, h: (c, h, 0))
    asp = pl.BlockSpec((1, half), lambda c, h: (0, 2 * c + h))
    dsp = pl.BlockSpec((1, LANE), lambda c, h: (0, c))
    sds = jax.ShapeDtypeStruct
    return pl.pallas_call(
        body, name=name, grid=(S5_CHUNKS, 2),
        in_specs=[ucol, ycol, scol, scol, bsp, bsp, asp, asp, csp, csp, dsp],
        out_specs=[ycol, bsp, bsp, asp, asp, csp, csp, dsp],
        out_shape=[sds((t, S5_CHUNKS * LANE), F32), sds((S5_CHUNKS, LANE, S5_STATES), F32), sds((S5_CHUNKS, LANE, S5_STATES), F32),
                   sds((1, S5_CHUNKS * S5_STATES), F32), sds((1, S5_CHUNKS * S5_STATES), F32),
                   sds((S5_CHUNKS, S5_STATES, LANE), F32), sds((S5_CHUNKS, S5_STATES, LANE), F32), sds((1, S5_CHUNKS * LANE), F32)],
        scratch_shapes=[pltpu.VMEM((t, half), F32)] * 4,
        compiler_params=_params(2),
    )(proj, dy, sr, si, bre, bim, are, aim, cre, cim, dsk)


def s5_prep(a_re, a_im, b_re, b_im, c_re, c_im, log_dt):
    lam = lax.complex(a_re, a_im)
    dt = jnp.exp(log_dt)[:, None]
    a_bar = jnp.exp(lam * dt)
    b_bar = ((a_bar - 1.0) / lam)[..., None] * lax.complex(b_re, b_im)
    g, p, cg = b_re.shape
    eye = jnp.eye(8, dtype=F32)

    def in_map(m):
        m = m.reshape(g // 8, 8, p, cg)
        return jnp.einsum("ab,kapc->kacbp", eye, m).reshape(g // 8, 8 * cg, 8 * p)

    def out_map(m):
        m = m.reshape(g // 8, 8, cg, p)
        return jnp.einsum("ab,kacp->kapbc", eye, m).reshape(g // 8, 8 * p, 8 * cg)

    return (jnp.real(a_bar).reshape(1, g * p), jnp.imag(a_bar).reshape(1, g * p), in_map(jnp.real(b_bar)), in_map(jnp.imag(b_bar)),
            out_map(c_re), out_map(c_im))


def rope_tables(pos, half):
    inv = ROPE_THETA ** (-jnp.arange(half, dtype=F32) / half)
    ang = pos.astype(F32)[:, None] * inv
    cos, sin = jnp.cos(ang), jnp.sin(ang)
    reps = max(LANE // (2 * half), 1)
    return jnp.tile(jnp.concatenate([cos, cos], axis=1), (1, reps)), jnp.tile(jnp.concatenate([-sin, sin], axis=1), (1, reps))


A_W = 1024
ROW_T = 256


def _tiled(a, width, col):
    return (a, (ROW_T, width), lambda i: (i, col))


def _out_tiled(t, width, dtype):
    return ((t, width), dtype, (ROW_T, width), lambda i: (i, 0), False)


def qkv_rope_even(proj, cos, sin, name):
    t = proj.shape[0]

    def fn(q, k, v, cos, sin):
        return _rope(q, cos, sin, 64), _rope(k, cos, sin, 64), v
    ins = [_tiled(proj, A_W, 0), _tiled(proj, A_W, 1), _tiled(proj, A_W, 2), _tiled(cos, LANE, 0), _tiled(sin, LANE, 0)]
    return _tile_call(name, fn, (t // ROW_T,), ins, [_out_tiled(t, A_W, BF16)] * 3)


def merge3(o, lse, name):
    t = o[0].shape[0]

    def fn(o1, o2, o3, l1, l2, l3):
        mx = jnp.maximum(jnp.maximum(l1, l2), l3)
        e1, e2, e3 = jnp.exp(l1 - mx), jnp.exp(l2 - mx), jnp.exp(l3 - mx)
        den = e1 + e2 + e3
        out = (e1 * o1 + e2 * o2 + e3 * o3) / den
        return out, out, mx + jnp.log(den)
    ins = [_tiled(a, A_W, 0) for a in list(o) + list(lse)]
    return _tile_call(name, fn, (t // ROW_T,), ins, [_out_tiled(t, A_W, BF16), _out_tiled(t, A_W, F32), _out_tiled(t, A_W, F32)])


def _segsum_bcast(x, width):
    parts = []
    for h in range(x.shape[1] // width):
        s = jnp.sum(x[:, h * width:(h + 1) * width], axis=1, keepdims=True)
        parts.append(jnp.broadcast_to(s, (x.shape[0], width)))
    return jnp.concatenate(parts, axis=1)


def even_attn_prep(dmix, attn, name):
    t = attn.shape[0]

    def fn(dout, attn):
        return dout, _segsum_bcast(dout * attn, LANE)
    ins = [_tiled(dmix, A_W, 0), _tiled(attn, A_W, 0)]
    return _tile_call(name, fn, (t // ROW_T,), ins, [_out_tiled(t, A_W, BF16), _out_tiled(t, A_W, F32)])


def even_dproj(dq, dk, dv, dxb, dyb, cos, sin, name):
    t = dxb.shape[0]

    def fn(q1, q2, q3, k1, k2, k3, v1, v2, v3, dxb, dyb, cos, sin):
        return jnp.concatenate([_rope_t(q1 + q2 + q3, cos, sin, 64).astype(BF16), _rope_t(k1 + k2 + k3, cos, sin, 64).astype(BF16),
                                (v1 + v2 + v3).astype(BF16), dxb, dyb], axis=1)
    ins = [_tiled(a, A_W, 0) for a in list(dq) + list(dk) + list(dv) + [dxb, dyb]] + [_tiled(cos, LANE, 0), _tiled(sin, LANE, 0)]
    return _tile_call(name, fn, (t // ROW_T,), ins, [_out_tiled(t, 5 * A_W, BF16)])[0]


def perm(x, d):
    t = x.shape[0]
    return x.reshape(t // d, d, 8, LANE).transpose(1, 2, 0, 3).reshape(d * 8, t // d, LANE)


def unperm(xp, d):
    n, l, _ = xp.shape
    return xp.reshape(d, 8, l, LANE).transpose(2, 0, 1, 3).reshape(l * d, 8 * LANE)


def qkv_rope_odd(proj, cos, sin, name):
    t = proj.shape[0]

    def fn(q, k, v, cos, sin):
        return _rope(q, cos, sin, 32), _rope(k, cos, sin, 32), v
    ins = [_tiled(proj, A_W, 0), _tiled(proj, LANE, 8), _tiled(proj, LANE, 9), _tiled(cos, LANE, 0), _tiled(sin, LANE, 0)]
    return _tile_call(name, fn, (t // ROW_T,), ins, [_out_tiled(t, A_W, BF16), _out_tiled(t, LANE, BF16), _out_tiled(t, LANE, BF16)])


def _head_blocks(a):
    return (a, (None, ROW_T, a.shape[2]), lambda h, i: (h, i, 0))


def sink_fwd(o, lse, sink_b, name):
    nh, t, dh = o.shape

    def fn(o, lse, s):
        return o * _sigmoid(lse - s)
    ins = [_head_blocks(o), _head_blocks(lse), (sink_b, (None, 1, dh), lambda h, i: (h, 0, 0))]
    return _tile_call(name, fn, (nh, t // ROW_T), ins, [((nh, t, dh), BF16, (None, ROW_T, dh), lambda h, i: (h, i, 0), False)])[0]


def sink_bwd(dof, o, lse, sink_b, name):
    nh, t, dh = o.shape

    def fn(dof, o, lse, s):
        keep = _sigmoid(lse - s)
        dk = jnp.sum(dof * o, axis=1, keepdims=True)
        dlse = dk * keep * (1.0 - keep)
        return dof * keep, dk * keep * keep, -jnp.sum(dlse, axis=0, keepdims=True)
    ins = [_head_blocks(dof), _head_blocks(o), _head_blocks(lse), (sink_b, (None, 1, dh), lambda h, i: (h, 0, 0))]
    outs = [((nh, t, dh), BF16, (None, ROW_T, dh), lambda h, i: (h, i, 0), False),
            ((nh, t, dh), F32, (None, ROW_T, dh), lambda h, i: (h, i, 0), False),
            ((nh, 1, dh), F32, (None, 1, dh), lambda h, i: (h, 0, 0), True)]
    return _tile_call(name, fn, (nh, t // ROW_T), ins, outs, acc_axis=1)


def odd_dproj(dq, dk, dv, du, cos, sin, name):
    t = dq.shape[0]

    def fn(dq, dk, dv, du, cos, sin):
        return jnp.concatenate([_rope_t(dq, cos, sin, 32), _rope_t(dk, cos, sin, 32), dv, du], axis=1)
    ins = [_tiled(dq, A_W, 0), _tiled(dk, LANE, 0), _tiled(dv, LANE, 0), _tiled(du, A_W, 0), _tiled(cos, LANE, 0), _tiled(sin, LANE, 0)]
    return _tile_call(name, fn, (t // ROW_T,), ins, [_out_tiled(t, 2 * A_W + 2 * LANE, BF16)])[0]


def glu_z(y, name):
    return _rows_call(name, _gelu, y.shape[0], [y], [], [(y.shape[1], BF16)])[0]


def glu_out(y, gpre, b, name):
    def fn(y, gpre, b):
        return _gelu(y) * _sigmoid(gpre + b)
    return _rows_call(name, fn, y.shape[0], [y, gpre], [b], [(y.shape[1], BF16)])[0]


def glu_bwd_gate(dmix, y, gpre, b, name):
    t = y.shape[0]

    def fn(dout, y, gpre, b):
        gate = _sigmoid(gpre + b)
        dgp = dout * _gelu(y) * gate * (1.0 - gate)
        return dgp, jnp.sum(dgp, axis=0, keepdims=True)
    ins = [_tiled(dmix, A_W, 1), _tiled(y, A_W, 0), _tiled(gpre, A_W, 0), (b, (1, A_W), lambda i: (0, 0))]
    outs = [_out_tiled(t, A_W, BF16), ((1, A_W), F32, (1, A_W), lambda i: (0, 0), True)]
    return _tile_call(name, fn, (t // ROW_T,), ins, outs, acc_axis=0)


def glu_bwd_y(dmix, y, gpre, b, dz_mm, name):
    t = y.shape[0]

    def fn(dout, y, gpre, b, dz_mm):
        return (dout * _sigmoid(gpre + b) + dz_mm) * _gelu_grad(y)
    ins = [_tiled(dmix, A_W, 1), _tiled(y, A_W, 0), _tiled(gpre, A_W, 0), (b, (1, A_W), lambda i: (0, 0)), _tiled(dz_mm, A_W, 0)]
    return _tile_call(name, fn, (t // ROW_T,), ins, [_out_tiled(t, A_W, F32)])[0]


def adamw(w, g, m, v, name):
    def fn(w, g, m, v):
        m = ADAM_B1 * m + (1.0 - ADAM_B1) * g
        v = ADAM_B2 * v + (1.0 - ADAM_B2) * (g * g)
        m_hat = m / (1.0 - ADAM_B1 ** ADAM_STEP)
        v_hat = v / (1.0 - ADAM_B2 ** ADAM_STEP)
        return -ADAM_LR * (m_hat / (jnp.sqrt(v_hat) + ADAM_EPS) + ADAM_WD * w), m, v
    c = w.shape[1]
    return _rows_call(name, fn, w.shape[0], [w, g, m, v], [], [(c, F32)] * 3)


def _sum_in_order(v):
    s = v[0].astype(F32)
    for d in range(1, v.shape[0]):
        s = s + v[d].astype(F32)
    return s


def sum_devices(parts, name):
    nd, nl, r, c = parts.shape
    tr = 8
    while tr * 2 <= 256 and r % (tr * 2) == 0 and tr * 2 * c * 4 * nd <= 4 * 1024 * 1024:
        tr *= 2
    ins = [(parts, (nd, None, tr, c), lambda l, i: (0, l, i, 0))]
    outs = [((nl, r, c), F32, (None, tr, c), lambda l, i: (l, i, 0), False)]
    return _tile_call(name, _sum_in_order, (nl, r // tr), ins, outs)[0]


def silu_rows(c_all, name):
    def fn(c):
        return c * _sigmoid(c)
    return _rows_call(name, fn, c_all.shape[0], [c_all], [], [(c_all.shape[1], F32)])[0]


def _place():
    x, y, c = lax.axis_index("x"), lax.axis_index("y"), lax.axis_index("c")
    return x, y, c


ANY = pl.BlockSpec(memory_space=pl.ANY)


def all_gather8(v, name):
    r, cdim = v.shape

    def body(x_ref, out_ref, send_sems, recv_sems, local_sem):
        x, y, c = _place()
        me, sibling = (x, y, c), (x, y, 1 - c)
        chips = [(1 - x, y), (x, 1 - y), (1 - x, 1 - y)]

        def rows(px, py, pc):
            return out_ref.at[4 * px + 2 * py + pc]

        def copy(k, block, to, src=None):
            return pltpu.make_async_remote_copy(
                src_ref=rows(*block) if src is None else src, dst_ref=rows(*block),
                send_sem=send_sems.at[k], recv_sem=recv_sems.at[k], device_id=to, device_id_type=MESH)

        mine = pltpu.make_async_copy(x_ref, rows(*me), local_sem)
        mine.start()
        first = [copy(0, me, sibling, src=x_ref)]
        first += [copy(1 + j, me, (*chip, c), src=x_ref) for j, chip in enumerate(chips)]
        for cp in first:
            cp.start()
        passed = [copy(4 + j, (*chip, c), sibling) for j, chip in enumerate(chips)]
        for j, chip in enumerate(chips):
            copy(1 + j, (*chip, c), me).wait_recv()
            passed[j].start()
        copy(0, sibling, me).wait_recv()
        for j, chip in enumerate(chips):
            copy(4 + j, (*chip, 1 - c), me).wait_recv()
        for cp in first + passed:
            cp.wait_send()
        mine.wait()

    return pl.pallas_call(
        body, name=name, out_shape=jax.ShapeDtypeStruct((N_DEV, r, cdim), v.dtype),
        in_specs=[ANY], out_specs=ANY,
        scratch_shapes=[pltpu.SemaphoreType.DMA((7,)), pltpu.SemaphoreType.DMA((7,)), pltpu.SemaphoreType.DMA],
    )(v)


def gather_weights(shards, name):
    n = len(shards)

    def body(*refs):
        ins, outs = refs[:n], refs[n:2 * n]
        send_sems, recv_sems = refs[2 * n:]
        x, y, c = _place()
        sibling = (x, y, 1 - c)
        chips = [(1 - x, y), (x, 1 - y), (1 - x, 1 - y)]
        my_chip = 2 * x + y

        def half(t, chip_slot, start):
            hr = ins[t].shape[1] // 2
            return outs[t].at[chip_slot, :, pl.ds(start, hr), :]

        def copy(t, k, src, dst, to):
            return pltpu.make_async_remote_copy(src_ref=src, dst_ref=dst, send_sem=send_sems.at[6 * t + k],
                                                recv_sem=recv_sems.at[6 * t + k], device_id=to, device_id_type=MESH)

        def lows(t):
            hr = ins[t].shape[1] // 2
            return hr, pl.multiple_of(c * hr, 16), pl.multiple_of((1 - c) * hr, 16)

        started = []
        for t in range(n):
            hr, lo, _ = lows(t)
            for j, chip in enumerate(chips):
                cp = copy(t, j, ins[t].at[:, pl.ds(lo, hr), :], half(t, my_chip, lo), (*chip, c))
                cp.start()
                started.append(cp)
        for t in range(n):
            hr, lo, _ = lows(t)
            for j, (px, py) in enumerate(chips):
                slot = 2 * px + py
                copy(t, j, half(t, slot, lo), half(t, slot, lo), (px, py, c)).wait_recv()
                fwd = copy(t, 3 + j, half(t, slot, lo), half(t, slot, lo), sibling)
                fwd.start()
                started.append(fwd)
        for t in range(n):
            hr, _, lo_sib = lows(t)
            for j, (px, py) in enumerate(chips):
                slot = 2 * px + py
                copy(t, 3 + j, half(t, slot, lo_sib), half(t, slot, lo_sib), sibling).wait_recv()
        for cp in started:
            cp.wait_send()

    got = pl.pallas_call(
        body, name=name,
        out_shape=[jax.ShapeDtypeStruct((N_CHIP,) + s.shape, s.dtype) for s in shards],
        in_specs=[ANY] * n, out_specs=[ANY] * n,
        scratch_shapes=[pltpu.SemaphoreType.DMA((6 * n,)), pltpu.SemaphoreType.DMA((6 * n,))],
    )(*shards)
    my_chip = 2 * lax.axis_index("x") + lax.axis_index("y")
    return [lax.dynamic_update_index_in_dim(g, s, my_chip, 0) for g, s in zip(got, shards)]


def exchange_pieces(grads, name):
    n = len(grads)

    def body(*refs):
        ins, outs = refs[:n], refs[n:2 * n]
        send_sems, recv_sems = refs[2 * n:]
        x, y, c = _place()
        me = 4 * x + 2 * y + c
        waits = []
        for t in range(n):
            g_ref, o_ref = ins[t], outs[t]
            hr = g_ref.shape[2] // 2

            def piece(qx, qy, qc, g_ref=g_ref, hr=hr):
                return g_ref.at[2 * qx + qy, :, pl.ds(pl.multiple_of(qc * hr, 16), hr), :]

            sends = []
            for k in range(1, N_DEV):
                qx = (1 - x) if (k >> 2) & 1 else x
                qy = (1 - y) if (k >> 1) & 1 else y
                qc = (1 - c) if k & 1 else c
                cp = pltpu.make_async_remote_copy(src_ref=piece(qx, qy, qc), dst_ref=o_ref.at[me],
                                                  send_sem=send_sems.at[7 * t + k - 1], recv_sem=recv_sems.at[7 * t + k - 1],
                                                  device_id=(qx, qy, qc), device_id_type=MESH)
                cp.start()
                sends.append((cp, 4 * qx + 2 * qy + qc))
            waits.append((sends, o_ref, t))
        for sends, o_ref, t in waits:
            for k, (cp, peer) in enumerate(sends):
                pltpu.make_async_remote_copy(src_ref=o_ref.at[peer], dst_ref=o_ref.at[peer], send_sem=send_sems.at[7 * t + k],
                                             recv_sem=recv_sems.at[7 * t + k], device_id=(x, y, c), device_id_type=MESH).wait_recv()
            for cp, _ in sends:
                cp.wait_send()

    got = pl.pallas_call(
        body, name=name,
        out_shape=[jax.ShapeDtypeStruct((N_DEV, g.shape[1], g.shape[2] // 2, g.shape[3]), g.dtype) for g in grads],
        in_specs=[ANY] * n, out_specs=[ANY] * n,
        scratch_shapes=[pltpu.SemaphoreType.DMA((7 * n,)), pltpu.SemaphoreType.DMA((7 * n,))],
    )(*grads)
    xi, yi, ci = _place()
    out = []
    for g, o in zip(grads, got):
        hr = g.shape[2] // 2
        own = lax.dynamic_slice(g, (2 * xi + yi, 0, ci * hr, 0), (1, g.shape[1], hr, g.shape[3]))
        out.append(lax.dynamic_update_slice(o, own, (4 * xi + 2 * yi + ci, 0, 0, 0)))
    return out


def join_halves(halves, name):
    n = len(halves)
    chunks = [(t, l, j) for t in range(n) for l in range(halves[t].shape[0]) for j in range(2)]

    def body(*refs):
        ins, outs = refs[:n], refs[n:2 * n]
        send_sems, recv_sems = refs[2 * n:]
        x, y, c = _place()
        sibling = (x, y, 1 - c)
        pending = []
        for k, (t, l, j) in enumerate(chunks):
            h_ref, o_ref = ins[t], outs[t]
            hr = h_ref.shape[1]
            rows = hr // 2
            lo = pl.multiple_of(c * hr + j * rows, 8)
            lo_sib = pl.multiple_of((1 - c) * hr + j * rows, 8)
            src = h_ref.at[l, pl.ds(j * rows, rows), :]
            cp = pltpu.make_async_remote_copy(src_ref=src, dst_ref=o_ref.at[l, pl.ds(lo, rows), :], send_sem=send_sems.at[k],
                                              recv_sem=recv_sems.at[k], device_id=sibling, device_id_type=MESH)
            cp.start()
            got = pltpu.make_async_remote_copy(src_ref=src, dst_ref=o_ref.at[l, pl.ds(lo_sib, rows), :], send_sem=send_sems.at[k],
                                               recv_sem=recv_sems.at[k], device_id=sibling, device_id_type=MESH)
            pending.append((cp, got))
        for cp, got in pending:
            got.wait_recv()
            cp.wait_send()

    got = pl.pallas_call(
        body, name=name,
        out_shape=[jax.ShapeDtypeStruct((h.shape[0], 2 * h.shape[1], h.shape[2]), h.dtype) for h in halves],
        in_specs=[ANY] * n, out_specs=[ANY] * n,
        scratch_shapes=[pltpu.SemaphoreType.DMA((len(chunks),)), pltpu.SemaphoreType.DMA((len(chunks),))],
    )(*halves)
    ci = lax.axis_index("c")
    return [lax.dynamic_update_slice(g, h, (0, ci * h.shape[1], 0)) for g, h in zip(got, halves)]


WEIGHTS = ['ada_w', 'ada_b', 'norm_mix', 'norm_ffn', 'norm_final', 'ev_w_in', 'ev_conv_w', 'ev_conv_b', 'ev_gate_a_w', 'ev_gate_a_b',
           'ev_gate_x_w', 'ev_gate_x_b', 'ev_lambda', 'ev_w_out', 'od_w_in', 'od_sinks', 'od_a_re', 'od_a_im', 'od_b_re', 'od_b_im',
           'od_c_re', 'od_c_im', 'od_d', 'od_log_dt', 'od_glu_w', 'od_glu_b', 'od_w_out', 'ffn_w_in', 'ffn_conv_w', 'ffn_conv_b', 'ffn_w_out']
BIG = ['ev_w_in', 'ev_w_out', 'od_w_in', 'od_glu_w', 'od_w_out', 'ffn_w_in', 'ffn_w_out']
COL_SHARDED = ('ev_w_in', 'od_w_in', 'ffn_w_in')
SMALL_SHARDED = ['ev_conv_w', 'od_d', 'od_glu_b', 'ffn_conv_w']
SMALL = [n for n in WEIGHTS if n not in BIG and n != 'ada_w']


def _pack(arrs):
    flat = jnp.concatenate([a.reshape(-1).astype(F32) for a in arrs])
    rows = -(-flat.shape[0] // (1024 * LANE)) * 1024
    return jnp.pad(flat, (0, rows * LANE - flat.shape[0])).reshape(rows, LANE)


def _unpack(flat, shapes):
    out, off = [], 0
    for s in shapes:
        n = math.prod(s)
        out.append(flat[..., off:off + n].reshape(flat.shape[:-1] + tuple(s)))
        off += n
    return out


def _ffn_fwd(l, h2, wf, cw, cb):
    u = mm(h2, wf['ffn_w_in'][l], tm=2048, tn=256, name=f"ffn_in{l}")
    act = ffn_act(u, cw, cb, f"ffn_act{l}")
    f = mm(act, wf['ffn_w_out'][l], tm=1024, tn=512, name=f"ffn_out{l}")
    return f, dict(u=u, act=act)


def _ffn_bwd(l, df, s, h2, wf, cw, cb):
    dact = mm(df, wf['ffn_w_out'][l], tb=True, tm=2048, tn=128, name=f"ffn_dact{l}")
    dwo = mm(s['act'], df, ta=True, out_dtype=BF16, tm=D_FF, tn=256, tk=512, name=f"ffn_dwo{l}")
    du, dcw, dcb = ffn_act_bwd(s['u'], dact, cw, cb, f"ffn_act_bwd{l}")
    dh2 = mm(du, wf['ffn_w_in'][l], tb=True, tm=1024, tn=512, tk=D_FF, name=f"ffn_dh{l}")
    dwi = mm(h2, du, ta=True, out_dtype=BF16, tm=2048, tn=256, name=f"ffn_dwi{l}")
    return dh2, dwi, dwo, dcw, dcb


def _even_fwd(e, h1, a, wf, fs, tabs):
    cos, sin = tabs
    proj = mm(h1, wf['ev_w_in'][e], name=f"ev_in{e}")
    q, k, v = qkv_rope_even(proj, cos, sin, f"ev_rope{e}")
    outs, lses = [], []
    for window, d in A_PATTERNS:
        o, lse = attn_fwd(perm(q, d)[:, None], perm(k, d), perm(v, d), window // d, LANE ** -0.5, f"ev_attn{e}_{d}")
        outs.append(unperm(o[:, 0], d))
        lses.append(unperm(lse[:, 0], d))
    attn_bf, attn, lse_tot = merge3(outs, lses, f"ev_merge{e}")
    lru, hs = rglru_fwd(proj, fs['ev_conv_w'][e], a['ev_conv_b'][e][None], a['ev_gate_a_w'][e], a['ev_gate_a_b'][e][None],
                        a['ev_gate_x_w'][e], a['ev_gate_x_b'][e][None], a['ev_lambda'][e][None], 24, 32, f"ev_lru{e}")
    mix = jnp.concatenate([attn_bf, lru], axis=1)
    y = mm(mix, wf['ev_w_out'][e], name=f"ev_out{e}")
    return y, dict(proj=proj, q=q, k=k, v=v, attn=attn, lse=lse_tot, hs=hs, mix=mix)


def _even_bwd(e, dyg, s, h1, a, wf, fs, tabs, gs):
    cos, sin = tabs
    dmix = mm(dyg, wf['ev_w_out'][e], tb=True, name=f"ev_dmix{e}")
    dwo = mm(s['mix'], dyg, ta=True, out_dtype=BF16, name=f"ev_dwo{e}")
    do_bf, dvec = even_attn_prep(dmix, s['attn'], f"ev_prep{e}")
    dqs, dks, dvs = [], [], []
    for window, d in A_PATTERNS:
        dq, dk, dv = attn_bwd(perm(s['q'], d)[:, None], perm(s['k'], d), perm(s['v'], d), perm(do_bf, d)[:, None],
                              perm(s['lse'], d)[:, None], perm(dvec, d)[:, None], window // d, LANE ** -0.5, f"ev_attn_bwd{e}_{d}")
        dqs.append(unperm(dq[:, 0], d))
        dks.append(unperm(dk, d))
        dvs.append(unperm(dv, d))
    dxb, dyb, dcw, dcb, dgaw, dgab, dgxw, dgxb, dlam = rglru_bwd(
        s['proj'], s['hs'], dmix, 8, fs['ev_conv_w'][e], a['ev_conv_b'][e][None], a['ev_gate_a_w'][e], a['ev_gate_a_b'][e][None],
        a['ev_gate_x_w'][e], a['ev_gate_x_b'][e][None], a['ev_lambda'][e][None], 24, 32, f"ev_lru_bwd{e}")
    for n, g in (('ev_conv_w', dcw), ('ev_conv_b', dcb[0]), ('ev_gate_a_w', dgaw), ('ev_gate_a_b', dgab[0]), ('ev_gate_x_w', dgxw),
                 ('ev_gate_x_b', dgxb[0]), ('ev_lambda', dlam[0])):
        gs[n][e] = g
    dproj = even_dproj(dqs, dks, dvs, dxb, dyb, cos, sin, f"ev_dproj{e}")
    dh1 = mm(dproj, wf['ev_w_in'][e], tb=True, tk=2560, name=f"ev_dh{e}")
    dwi = mm(h1, dproj, ta=True, out_dtype=BF16, tm=2048, tn=512, name=f"ev_dwi{e}")
    return dh1, dwi, dwo


def _odd_fwd(o, h1, a, wf, fs, tabs):
    cos, sin = tabs
    t = h1.shape[0]
    proj = mm(h1, wf['od_w_in'][o], name=f"od_in{o}")
    qr, kr, vv = qkv_rope_odd(proj, cos, sin, f"od_rope{o}")
    qh = qr.reshape(t, 2, 8, 64).transpose(1, 2, 0, 3)
    kh = kr.reshape(t, 2, 64).transpose(1, 0, 2)
    vh = vv.reshape(t, 2, 64).transpose(1, 0, 2)
    oh, lse = attn_fwd(qh, kh, vh, 127, 64 ** -0.5, f"od_attn{o}")
    sink_b = jnp.broadcast_to(a['od_sinks'][o].reshape(16, 1, 1), (16, 1, 64))
    oh, lse = oh.reshape(16, t, 64), lse.reshape(16, t, 64)
    attn_hm = sink_fwd(oh, lse, sink_b, f"od_sink{o}")
    attn_tm = attn_hm.transpose(1, 0, 2).reshape(t, A_W)
    prep_in = tuple(a[n][o] for n in ('od_a_re', 'od_a_im', 'od_b_re', 'od_b_im', 'od_c_re', 'od_c_im', 'od_log_dt'))
    (are, aim, bre, bim, cre, cim), prep_vjp = jax.vjp(s5_prep, *prep_in)
    s5w = (bre, bim, are, aim, cre, cim, fs['od_d'][o][None])
    y, sr, si = s5_fwd(proj, 10, *s5w, f"od_s5{o}")
    z = glu_z(y, f"od_glu_z{o}")
    gpre = mm(z, wf['od_glu_w'][o], name=f"od_glu_mm{o}")
    glu_b = fs['od_glu_b'][o][None]
    ssm = glu_out(y, gpre, glu_b, f"od_glu_out{o}")
    mix = jnp.concatenate([attn_tm, ssm], axis=1)
    yo = mm(mix, wf['od_w_out'][o], name=f"od_out{o}")
    return yo, dict(proj=proj, qh=qh, kh=kh, vh=vh, oh=oh, lse=lse, sink_b=sink_b, prep_vjp=prep_vjp, s5w=s5w, y=y, sr=sr, si=si,
                    z=z, gpre=gpre, glu_b=glu_b, mix=mix)


def _odd_bwd(o, dyg, s, h1, a, wf, fs, tabs, gs):
    cos, sin = tabs
    t = h1.shape[0]
    dmix = mm(dyg, wf['od_w_out'][o], tb=True, name=f"od_dmix{o}")
    dwo = mm(s['mix'], dyg, ta=True, out_dtype=BF16, name=f"od_dwo{o}")
    dgp, dglu_b = glu_bwd_gate(dmix, s['y'], s['gpre'], s['glu_b'], f"od_glu_bwd_gate{o}")
    dz_mm = mm(dgp, wf['od_glu_w'][o], tb=True, name=f"od_glu_dz{o}")
    dglu_w = mm(s['z'], dgp, ta=True, out_dtype=BF16, name=f"od_glu_dw{o}")
    dy = glu_bwd_y(dmix, s['y'], s['gpre'], s['glu_b'], dz_mm, f"od_glu_bwd_y{o}")
    du, dbre, dbim, dare, daim, dcre, dcim, dd = s5_bwd(s['proj'], 10, dy, s['sr'], s['si'], *s['s5w'], f"od_s5_bwd{o}")
    ga = s['prep_vjp']((dare, daim, dbre, dbim, dcre, dcim))
    for n, g in zip(('od_a_re', 'od_a_im', 'od_b_re', 'od_b_im', 'od_c_re', 'od_c_im', 'od_log_dt'), ga):
        gs[n][o] = g
    gs['od_d'][o] = dd[0]
    gs['od_glu_b'][o] = dglu_b[0]
    dattn_hm = dmix[:, :A_W].reshape(t, 16, 64).transpose(1, 0, 2)
    do, dvec, dsink = sink_bwd(dattn_hm, s['oh'], s['lse'], s['sink_b'], f"od_sink_bwd{o}")
    gs['od_sinks'][o] = dsink[:, 0, 0]
    dq, dk, dv = attn_bwd(s['qh'], s['kh'], s['vh'], do.reshape(2, 8, t, 64), s['lse'].reshape(2, 8, t, 64), dvec.reshape(2, 8, t, 64),
                          127, 64 ** -0.5, f"od_attn_bwd{o}")
    dq_tm = dq.transpose(2, 0, 1, 3).reshape(t, A_W)
    dk_tm = dk.transpose(1, 0, 2).reshape(t, LANE)
    dv_tm = dv.transpose(1, 0, 2).reshape(t, LANE)
    dproj = odd_dproj(dq_tm, dk_tm, dv_tm, du, cos, sin, f"od_dproj{o}")
    dh1 = mm(dproj, wf['od_w_in'][o], tb=True, name=f"od_dh{o}")
    dwi = mm(h1, dproj, ta=True, out_dtype=BF16, tm=2048, tn=768, name=f"od_dwi{o}")
    return dh1, dwi, dwo, dglu_w


def kernel(x, c, positions, ada_w, ada_b, norm_mix, norm_ffn, norm_final, ev_w_in, ev_conv_w, ev_conv_b, ev_gate_a_w, ev_gate_a_b, ev_gate_x_w, ev_gate_x_b, ev_lambda, ev_w_out, od_w_in, od_sinks, od_a_re, od_a_im, od_b_re, od_b_im, od_c_re, od_c_im, od_d, od_log_dt, od_glu_w, od_glu_b, od_w_out, ffn_w_in, ffn_conv_w, ffn_conv_b, ffn_w_out, loss_target, m_ada_w, m_ada_b, m_norm_mix, m_norm_ffn, m_norm_final, m_ev_w_in, m_ev_conv_w, m_ev_conv_b, m_ev_gate_a_w, m_ev_gate_a_b, m_ev_gate_x_w, m_ev_gate_x_b, m_ev_lambda, m_ev_w_out, m_od_w_in, m_od_sinks, m_od_a_re, m_od_a_im, m_od_b_re, m_od_b_im, m_od_c_re, m_od_c_im, m_od_d, m_od_log_dt, m_od_glu_w, m_od_glu_b, m_od_w_out, m_ffn_w_in, m_ffn_conv_w, m_ffn_conv_b, m_ffn_w_out, v_ada_w, v_ada_b, v_norm_mix, v_norm_ffn, v_norm_final, v_ev_w_in, v_ev_conv_w, v_ev_conv_b, v_ev_gate_a_w, v_ev_gate_a_b, v_ev_gate_x_w, v_ev_gate_x_b, v_ev_lambda, v_ev_w_out, v_od_w_in, v_od_sinks, v_od_a_re, v_od_a_im, v_od_b_re, v_od_b_im, v_od_c_re, v_od_c_im, v_od_d, v_od_log_dt, v_od_glu_w, v_od_glu_b, v_od_w_out, v_ffn_w_in, v_ffn_conv_w, v_ffn_conv_b, v_ffn_w_out):
    a = dict(locals())
    xi, yi, ci = _place()
    chip = 2 * xi + yi
    me = 2 * chip + ci
    x0, target, pos = x[0], loss_target[0], positions[0]
    d = D_MODEL

    g0 = all_gather8(_pack([c] + [a[n] for n in SMALL_SHARDED]), "gather_small").reshape(N_DEV, -1)
    c_all = g0[:, :d]
    fs, off = {}, d
    for n in SMALL_SHARDED:
        sh = a[n].shape
        parts = g0[0::2, off:off + math.prod(sh)].reshape((N_CHIP,) + sh)
        fs[n] = jnp.moveaxis(parts, 0, -2).reshape(sh[:-1] + (N_CHIP * sh[-1],))
        off += math.prod(sh)
    cond_all = silu_rows(c_all, "silu")

    modp = jnp.stack([mm(cond_all, ada_w[l], tm=8, tn=512, name=f"mod{l}") for l in range(DEPTH)])
    mod_all = all_gather8(modp.reshape(-1, LANE), "gather_mod").reshape(N_DEV, DEPTH, N_DEV, 6 * d // N_CHIP)[0::2]
    mod_me = lax.dynamic_index_in_dim(mod_all, me, axis=2, keepdims=False)
    mod = jnp.transpose(mod_me, (1, 0, 2)).reshape(DEPTH, 6 * d) + ada_b
    mods = [[mod[l, i * d:(i + 1) * d][None] for i in range(6)] for l in range(DEPTH)]

    full = gather_weights([a[n].astype(BF16) for n in BIG], "gather_weights")
    wf = {}
    for n, f in zip(BIG, full):
        _, nl, r, cc = f.shape
        if n in COL_SHARDED:
            wf[n] = jnp.transpose(f, (1, 2, 0, 3)).reshape(nl, r, N_CHIP * cc)
        else:
            wf[n] = jnp.transpose(f, (1, 0, 2, 3)).reshape(nl, N_CHIP * r, cc)

    wf['ffn_w_in'] = ileave(wf['ffn_w_in'])
    ffn_cw, ffn_cb = ileave(fs['ffn_conv_w']), ileave(ffn_conv_b)

    tabs128 = rope_tables(pos, 64)
    tabs64 = rope_tables(pos, 32)

    saved = []
    xcur = x0
    for l in range(DEPTH):
        sh1, sc1, g1, sh2, sc2, g2 = mods[l]
        s = dict(x=xcur)
        s['h1'] = norm_mod(xcur, norm_mix[l][None], sc1, sh1, f"norm_mix{l}")
        if l % 2 == 0:
            s['y'], s['mixer'] = _even_fwd(l // 2, s['h1'], a, wf, fs, tabs128)
        else:
            s['y'], s['mixer'] = _odd_fwd(l // 2, s['h1'], a, wf, fs, tabs64)
        s['x2'], s['h2'] = resid_norm_mod(xcur, s['y'], g1, norm_ffn[l][None], sc2, sh2, f"norm_ffn{l}")
        s['f'], s['ffn'] = _ffn_fwd(l, s['h2'], wf, ffn_cw[l], ffn_cb[l][None])
        xcur = resid_add(s['x2'], s['f'], g2, f"resid{l}")
        saved.append(s)

    dx, loss_part, dnf = final_loss(xcur, norm_final[None], target, "loss")
    loss = lax.psum(loss_part[0, 0], ("x", "y", "c"))

    gs = {n: {} for n in SMALL}
    gbig = {n: {} for n in BIG}
    dmod = {}
    gs['norm_final'][0] = dnf[0]
    for l in reversed(range(DEPTH)):
        sh1, sc1, g1, sh2, sc2, g2 = mods[l]
        s = saved[l]
        df, dg2 = resid_bwd(dx, s['f'], g2, f"resid_bwd_ffn{l}")
        dh2, dwi, dwo, dcw, dcb = _ffn_bwd(l, df, s['ffn'], s['h2'], wf, ffn_cw[l], ffn_cb[l][None])
        gbig['ffn_w_in'][l], gbig['ffn_w_out'][l], gs['ffn_conv_w'][l], gs['ffn_conv_b'][l] = dwi, dwo, unileave(dcw), unileave(dcb[0])
        dx2, dsh2, dsc2, dgam2 = norm_mod_bwd(dh2, s['x2'], dx, norm_ffn[l][None], sc2, f"norm_ffn_bwd{l}")
        gs['norm_ffn'][l] = dgam2[0]
        dyg, dg1 = resid_bwd(dx2, s['y'], g1, f"resid_bwd_mix{l}")
        if l % 2 == 0:
            dh1, dwi, dwo = _even_bwd(l // 2, dyg, s['mixer'], s['h1'], a, wf, fs, tabs128, gs)
            gbig['ev_w_in'][l // 2], gbig['ev_w_out'][l // 2] = dwi, dwo
        else:
            dh1, dwi, dwo, dglu_w = _odd_bwd(l // 2, dyg, s['mixer'], s['h1'], a, wf, fs, tabs64, gs)
            gbig['od_w_in'][l // 2], gbig['od_w_out'][l // 2], gbig['od_glu_w'][l // 2] = dwi, dwo, dglu_w
        dx, dsh1, dsc1, dgam1 = norm_mod_bwd(dh1, s['x'], dx2, norm_mix[l][None], sc1, f"norm_mix_bwd{l}")
        gs['norm_mix'][l] = dgam1[0]
        dmod[l] = jnp.concatenate([dsh1, dsc1, dg1, dsh2, dsc2, dg2], axis=1)[0]
    grad_x = dx[None]
    gs['ada_b'] = dmod

    grads = {}
    g4 = []
    for n in BIG:
        g = jnp.stack([gbig[n][i] for i in range(len(gbig[n]))])
        if n == 'ffn_w_in':
            g = unileave(g)
        nl = g.shape[0]
        if n in COL_SHARDED:
            g4.append(g.reshape(nl, g.shape[1], N_CHIP, g.shape[2] // N_CHIP).transpose(2, 0, 1, 3))
        else:
            g4.append(g.reshape(nl, N_CHIP, g.shape[1] // N_CHIP, g.shape[2]).transpose(1, 0, 2, 3))
    pieces = exchange_pieces(g4, "exchange_grads")
    halves = [sum_devices(p, f"sum_{n}") for n, p in zip(BIG, pieces)]
    for n, g in zip(BIG, join_halves(halves, "join_grads")):
        grads[n] = g.reshape(a[n].shape)

    small_full = [jnp.stack([gs[n][i] for i in range(len(gs[n]))]) if n != 'norm_final' else gs[n][0] for n in SMALL]
    small_shapes = [g.shape for g in small_full]
    gs_all = all_gather8(_pack(small_full), "gather_small_grads")
    gs_sum = sum_devices(gs_all[:, None], "sum_small").reshape(-1)
    for n, g in zip(SMALL, _unpack(gs_sum, small_shapes)):
        if n in SMALL_SHARDED:
            w = a[n].shape[-1]
            g = lax.dynamic_slice_in_dim(g, chip * w, w, axis=g.ndim - 1)
        grads[n] = g
    assert SMALL[0] == 'ada_b'
    dmod_all = gs_all.reshape(N_DEV, -1)[:, :DEPTH * 6 * d].reshape(N_DEV, DEPTH, 6 * d)
    wcols = 6 * d // N_CHIP
    grads['ada_w'] = jnp.stack([
        mm(cond_all, lax.dynamic_slice_in_dim(dmod_all[:, l], chip * wcols, wcols, axis=1), ta=True, tm=2048, tn=512, name=f"ada_dw{l}")
        for l in range(DEPTH)])

    delta, new_m, new_v = {}, {}, {}
    for n in ['ada_w'] + BIG:
        sh = a[n].shape
        two_d = lambda t: t.reshape(-1, sh[-1])
        dl, nm, nv = adamw(two_d(a[n]), two_d(grads[n]), two_d(a['m_' + n]), two_d(a['v_' + n]), f"adamw_{n}")
        delta[n], new_m[n], new_v[n] = dl.reshape(sh), nm.reshape(sh), nv.reshape(sh)
    shapes = [a[n].shape for n in SMALL]
    dl, nm, nv = adamw(_pack([a[n] for n in SMALL]), _pack([grads[n] for n in SMALL]), _pack([a['m_' + n] for n in SMALL]),
                       _pack([a['v_' + n] for n in SMALL]), "adamw_small")
    for n, t1, t2, t3 in zip(SMALL, _unpack(dl.reshape(-1), shapes), _unpack(nm.reshape(-1), shapes), _unpack(nv.reshape(-1), shapes)):
        delta[n], new_m[n], new_v[n] = t1, t2, t3

    return (loss, grad_x, *[grads[n] for n in WEIGHTS], *[delta[n] for n in WEIGHTS], *[new_m[n] for n in WEIGHTS],
            *[new_v[n] for n in WEIGHTS])
```

```python
import functools
import math

import jax
import jax.numpy as jnp
from jax import lax
from jax.experimental import pallas as pl
from jax.experimental.pallas import tpu as pltpu

F32 = jnp.float32
BF16 = jnp.bfloat16
MESH = pl.DeviceIdType.MESH

D_MODEL = 2048
SEQ = 2048
DEPTH = 4
N_DEV = 8
N_CHIP = 4
BLK = 128
LANE = 128
V7X_VMEM_LIMIT = 56 * 1024 * 1024
NORM_EPS = 1e-6
ROPE_THETA = 10000.0
LRU_C = 8.0
D_FF = 5504
A_PATTERNS = ((128, 1), (512, 4), (2048, 16))
ADAM_LR, ADAM_B1, ADAM_B2, ADAM_EPS, ADAM_WD, ADAM_STEP = 0.001, 0.9, 0.999, 1e-08, 0.01, 10
NEG = -1e30


def _params(n_grid):
    return pltpu.CompilerParams(dimension_semantics=("arbitrary",) * n_grid, vmem_limit_bytes=V7X_VMEM_LIMIT)


def _pick(dim, pref):
    best = None
    for t in range(LANE, min(dim, pref) + 1, LANE):
        if dim % t == 0:
            best = t
    return best or dim


def _sigmoid(x):
    return 1.0 / (1.0 + jnp.exp(-x))


_GELU_C = math.sqrt(2.0 / math.pi)


def _gelu(x):
    t = jnp.tanh(_GELU_C * (x + 0.044715 * (x * x * x)))
    return 0.5 * x * (1.0 + t)


def _gelu_grad(x):
    t = jnp.tanh(_GELU_C * (x + 0.044715 * (x * x * x)))
    return 0.5 * (1.0 + t) + 0.5 * x * (1.0 - t * t) * (_GELU_C * (1.0 + 3.0 * 0.044715 * (x * x)))


def _softplus(x):
    return jnp.maximum(x, 0.0) + jnp.log(1.0 + jnp.exp(-jnp.abs(x)))


def _neg_expm1(x):
    series = -x * (1.0 + x * (0.5 + x * (1.0 / 6.0 + x * (1.0 / 24.0))))
    return jnp.where(x > -0.03, series, 1.0 - jnp.exp(x))


def _shift_down(x, k):
    if k == 0:
        return x
    row = lax.broadcasted_iota(jnp.int32, x.shape, 0)
    return jnp.where(row >= k, pltpu.roll(x, k, 0), 0.0)


def _shift_up(x, k):
    if k == 0:
        return x
    n = x.shape[0]
    row = lax.broadcasted_iota(jnp.int32, x.shape, 0)
    return jnp.where(row < n - k, pltpu.roll(x, n - k, 0), 0.0)


def _dot(a, b, ca=1, cb=0):
    return lax.dot_general(a.astype(BF16), b.astype(BF16), (((ca,), (cb,)), ((), ())), preferred_element_type=F32)


def _rope(x, cos, sin_signed, half):
    c = x.shape[1]
    reps = c // cos.shape[1]
    cos_c = jnp.tile(cos, (1, reps)) if reps > 1 else cos
    sin_c = jnp.tile(sin_signed, (1, reps)) if reps > 1 else sin_signed
    lane = lax.broadcasted_iota(jnp.int32, x.shape, 1)
    first = (lane % (2 * half)) < half
    partner = jnp.where(first, pltpu.roll(x, c - half, 1), pltpu.roll(x, half, 1))
    return x * cos_c + partner * sin_c


def _rope_t(dy, cos, sin_signed, half):
    c = dy.shape[1]
    reps = c // cos.shape[1]
    cos_c = jnp.tile(cos, (1, reps)) if reps > 1 else cos
    sin_c = jnp.tile(sin_signed, (1, reps)) if reps > 1 else sin_signed
    lane = lax.broadcasted_iota(jnp.int32, dy.shape, 1)
    first = (lane % (2 * half)) < half
    ys = dy * sin_c
    partner = jnp.where(first, pltpu.roll(ys, c - half, 1), pltpu.roll(ys, half, 1))
    return dy * cos_c + partner


def _tile_call(name, fn, grid, ins, outs, acc_axis=None):
    n_in = len(ins)
    accs = [o[4] for o in outs]

    def body(*refs):
        vals = fn(*[r[...] for r in refs[:n_in]])
        if not isinstance(vals, (tuple, list)):
            vals = (vals,)
        for r, v, acc in zip(refs[n_in:], vals, accs):
            if acc:
                first = pl.program_id(acc_axis) == 0

                @pl.when(first)
                def _():
                    r[...] = v.astype(r.dtype)

                @pl.when(jnp.logical_not(first))
                def _():
                    r[...] += v.astype(r.dtype)
            else:
                r[...] = v.astype(r.dtype)

    res = pl.pallas_call(
        body, name=name, grid=grid,
        in_specs=[pl.BlockSpec(b, im) for _, b, im in ins],
        out_specs=[pl.BlockSpec(o[2], o[3]) for o in outs],
        out_shape=[jax.ShapeDtypeStruct(o[0], o[1]) for o in outs],
        compiler_params=_params(len(grid)),
    )(*[a for a, _, _ in ins])
    return res


def _row_tile(cols, n_arrays, rows):
    budget = 24 * 1024 * 1024 // (2 * 4 * max(n_arrays, 1) * cols)
    t = 8
    while t * 2 <= budget and rows % (t * 2) == 0 and t * 2 <= 1024:
        t *= 2
    return t


def _rows_call(name, fn, rows, tiled, full, outs_tiled, outs_acc=()):
    cols = max([a.shape[1] for a in tiled] + [c for c, _ in outs_tiled])
    tt = _row_tile(cols, len(tiled) + len(outs_tiled), rows)
    ins = [(a, (tt, a.shape[1]), lambda i: (i, 0)) for a in tiled]
    ins += [(a, a.shape, (lambda nd: (lambda i: (0,) * nd))(a.ndim)) for a in full]
    outs = [((rows, c), dt, (tt, c), lambda i: (i, 0), False) for c, dt in outs_tiled]
    outs += [(s, dt, s, (lambda nd: (lambda i: (0,) * nd))(len(s)), True) for s, dt in outs_acc]
    return _tile_call(name, fn, (rows // tt,), ins, outs, acc_axis=0)


def mm(a, b, *, ta=False, tb=False, out_dtype=F32, tm=None, tn=None, tk=None, name):
    m, k = (a.shape[1], a.shape[0]) if ta else a.shape
    n = b.shape[0] if tb else b.shape[1]
    tm = tm or _pick(m, 1024)
    tn = tn or _pick(n, 1024)
    tk = tk or k
    assert m % tm == 0 and n % tn == 0 and k % tk == 0, (name, m, n, k, tm, tn, tk)
    nk = k // tk
    a_spec = pl.BlockSpec((tk, tm), lambda i, j, kk: (kk, i)) if ta else pl.BlockSpec((tm, tk), lambda i, j, kk: (i, kk))
    b_spec = pl.BlockSpec((tn, tk), lambda i, j, kk: (j, kk)) if tb else pl.BlockSpec((tk, tn), lambda i, j, kk: (kk, j))
    ca, cb = (0 if ta else 1), (1 if tb else 0)

    def body(a_ref, b_ref, o_ref, *scratch):
        p = _dot(a_ref[...], b_ref[...], ca, cb)
        if nk == 1:
            o_ref[...] = p.astype(o_ref.dtype)
        else:
            acc = scratch[0]
            kk = pl.program_id(2)

            @pl.when(kk == 0)
            def _():
                acc[...] = p

            @pl.when(kk > 0)
            def _():
                acc[...] += p

            @pl.when(kk == nk - 1)
            def _():
                o_ref[...] = acc[...].astype(o_ref.dtype)

    return pl.pallas_call(
        body, name=name, grid=(m // tm, n // tn, nk),
        in_specs=[a_spec, b_spec],
        out_specs=pl.BlockSpec((tm, tn), lambda i, j, kk: (i, j)),
        out_shape=jax.ShapeDtypeStruct((m, n), out_dtype),
        scratch_shapes=[pltpu.VMEM((tm, tn), F32)] if nk > 1 else [],
        compiler_params=_params(3),
    )(a, b)


def _rstd(x):
    return lax.rsqrt(jnp.mean(x * x, axis=-1, keepdims=True) + NORM_EPS)


def norm_mod(x, gamma, sc, sh, name):
    def fn(x, gamma, sc, sh):
        return (x * _rstd(x)) * gamma * (1.0 + sc) + sh
    return _rows_call(name, fn, x.shape[0], [x], [gamma, sc, sh], [(x.shape[1], BF16)])[0]


def resid_norm_mod(x, y, g, gamma, sc, sh, name):
    def fn(x, y, g, gamma, sc, sh):
        x2 = x + g * y
        return x2, (x2 * _rstd(x2)) * gamma * (1.0 + sc) + sh
    return _rows_call(name, fn, x.shape[0], [x, y], [g, gamma, sc, sh], [(x.shape[1], F32), (x.shape[1], BF16)])


def resid_add(x, y, g, name):
    def fn(x, y, g):
        return x + g * y
    return _rows_call(name, fn, x.shape[0], [x, y], [g], [(x.shape[1], F32)])[0]


def resid_bwd(dxo, f, g, name):
    def fn(dxo, f, g):
        return dxo * g, jnp.sum(dxo * f, axis=0, keepdims=True)
    d = dxo.shape[1]
    return _rows_call(name, fn, dxo.shape[0], [dxo, f], [g], [(d, BF16)], [((1, d), F32)])


def norm_mod_bwd(dh, x, dres, gamma, sc, name):
    def fn(dh, x, dres, gamma, sc):
        rstd = _rstd(x)
        xhat = x * rstd
        dxhat = dh * (gamma * (1.0 + sc))
        dx = rstd * (dxhat - xhat * jnp.mean(dxhat * xhat, axis=-1, keepdims=True))
        dhx = dh * xhat
        return (dres + dx, jnp.sum(dh, axis=0, keepdims=True), jnp.sum(dhx * gamma, axis=0, keepdims=True),
                jnp.sum(dhx * (1.0 + sc), axis=0, keepdims=True))
    d = x.shape[1]
    return _rows_call(name, fn, x.shape[0], [dh, x, dres], [gamma, sc], [(d, F32)], [((1, d), F32)] * 3)


def final_loss(x, gamma, target, name):
    d = x.shape[1]

    def fn(x, target, gamma):
        rstd = _rstd(x)
        xhat = x * rstd
        e = xhat * gamma - target
        part = 0.5 * jnp.sum(jnp.sum(e * e, axis=-1, keepdims=True) / d, axis=0, keepdims=True)
        dy = e / d
        dxhat = dy * gamma
        dx = rstd * (dxhat - xhat * jnp.mean(dxhat * xhat, axis=-1, keepdims=True))
        return dx, jnp.broadcast_to(part, (8, LANE)), jnp.sum(dy * xhat, axis=0, keepdims=True)
    return _rows_call(name, fn, x.shape[0], [x, target], [gamma], [(d, F32)], [((8, LANE), F32), ((1, d), F32)])


def _conv3(u, w, b):
    return b + w[2:3] * u + w[1:2] * _shift_down(u, 1) + w[0:1] * _shift_down(u, 2)


def _ffn_blocks(u, cw, cb):
    t = u.shape[0]
    nb = D_FF // LANE
    return [(u, (t, LANE), lambda j: (0, j)), (u, (t, LANE), lambda j: (0, j + nb)),
            (cw, (3, LANE), lambda j: (0, j)), (cw, (3, LANE), lambda j: (0, j + nb)),
            (cb, (1, LANE), lambda j: (0, j)), (cb, (1, LANE), lambda j: (0, j + nb))]


def ffn_act(u, cw, cb, name):
    t = u.shape[0]

    def fn(ug, uv, wg, wv, bg, bv):
        return _gelu(_conv3(ug, wg, bg)) * _conv3(uv, wv, bv)
    return _tile_call(name, fn, (D_FF // LANE,), _ffn_blocks(u, cw, cb), [((t, D_FF), BF16, (t, LANE), lambda j: (0, j), False)])[0]


def ffn_act_bwd(u, dact, cw, cb, name):
    t = u.shape[0]

    def conv_t(duc, us, w):
        du = w[2:3] * duc + w[1:2] * _shift_up(duc, 1) + w[0:1] * _shift_up(duc, 2)
        dw = jnp.concatenate([jnp.sum(duc * _shift_down(us, 2), axis=0, keepdims=True),
                              jnp.sum(duc * _shift_down(us, 1), axis=0, keepdims=True),
                              jnp.sum(duc * us, axis=0, keepdims=True)], axis=0)
        return du, dw, jnp.sum(duc, axis=0, keepdims=True)

    def fn(ug, uv, wg, wv, bg, bv, da):
        g = _conv3(ug, wg, bg)
        v = _conv3(uv, wv, bv)
        du_g, dw_g, db_g = conv_t(da * v * _gelu_grad(g), ug, wg)
        du_v, dw_v, db_v = conv_t(da * _gelu(g), uv, wv)
        return du_g, du_v, dw_g, dw_v, db_g, db_v
    ins = _ffn_blocks(u, cw, cb) + [(dact, (t, LANE), lambda j: (0, j))]
    col = lambda rows, dt: ((rows, D_FF), dt, (rows, LANE), lambda j: (0, j), False)
    return _tile_call(name, fn, (D_FF // LANE,), ins, [col(t, BF16), col(t, BF16), col(3, F32), col(3, F32), col(1, F32), col(1, F32)])


def attn_fwd(q, k, v, max_dist, scale, name):
    n, r, l, dh = q.shape
    nb = l // BLK
    m_rows = r * BLK

    def body(q_ref, kc_ref, kp_ref, vc_ref, vp_ref, o_ref, lse_ref):
        b = pl.program_id(1)
        qv = q_ref[...].reshape(m_rows, dh)
        s_c = _dot(qv, kc_ref[...], 1, 1) * scale
        s_p = _dot(qv, kp_ref[...], 1, 1) * scale
        qi = lax.broadcasted_iota(jnp.int32, (m_rows, BLK), 0) % BLK
        kj = lax.broadcasted_iota(jnp.int32, (m_rows, BLK), 1)
        s_c = jnp.where(kj <= qi, s_c, NEG)
        s_p = jnp.where((kj >= qi + (BLK - max_dist)) & (b > 0), s_p, NEG)
        mx = jnp.maximum(jnp.max(s_c, axis=1, keepdims=True), jnp.max(s_p, axis=1, keepdims=True))
        p_c = jnp.exp(s_c - mx)
        p_p = jnp.exp(s_p - mx)
        den = jnp.sum(p_c, axis=1, keepdims=True) + jnp.sum(p_p, axis=1, keepdims=True)
        o = (_dot(p_c, vc_ref[...]) + _dot(p_p, vp_ref[...])) / den
        o_ref[...] = o.reshape(r, BLK, dh)
        lse_ref[...] = jnp.broadcast_to(mx + jnp.log(den), (m_rows, dh)).reshape(r, BLK, dh)

    qspec = pl.BlockSpec((None, r, BLK, dh), lambda i, b: (i, 0, b, 0))
    cur = pl.BlockSpec((None, BLK, dh), lambda i, b: (i, b, 0))
    prev = pl.BlockSpec((None, BLK, dh), lambda i, b: (i, jnp.maximum(b - 1, 0), 0))
    return pl.pallas_call(
        body, name=name, grid=(n, nb),
        in_specs=[qspec, cur, prev, cur, prev],
        out_specs=[qspec, qspec],
        out_shape=[jax.ShapeDtypeStruct((n, r, l, dh), F32)] * 2,
        compiler_params=_params(2),
    )(q, k, k, v, v)


def attn_bwd(q, k, v, do, lse, dvec, max_dist, scale, name):
    n, r, l, dh = q.shape
    nb = l // BLK
    m_rows = r * BLK

    def body(qc_ref, qn_ref, kc_ref, kp_ref, vc_ref, vp_ref, doc_ref, don_ref, lc_ref, ln_ref, dc_ref, dn_ref,
             dq_ref, dk_ref, dv_ref):
        b = pl.program_id(1)
        qi = lax.broadcasted_iota(jnp.int32, (m_rows, BLK), 0) % BLK
        kj = lax.broadcasted_iota(jnp.int32, (m_rows, BLK), 1)
        m_cur = kj <= qi
        m_prev = kj >= qi + (BLK - max_dist)

        def pair(q_ref, do_ref, l_ref, d_ref, k_ref, v_ref, mask):
            qv = q_ref[...].reshape(m_rows, dh)
            dov = do_ref[...].reshape(m_rows, dh)
            lrow = l_ref[...].reshape(m_rows, dh)[:, 0:1]
            drow = d_ref[...].reshape(m_rows, dh)[:, 0:1]
            s = _dot(qv, k_ref[...], 1, 1) * scale
            p = jnp.where(mask, jnp.exp(jnp.where(mask, s, NEG) - lrow), 0.0)
            dp = _dot(dov, v_ref[...], 1, 1)
            ds = p * (dp - drow) * scale
            return qv, dov, p, ds

        q_a, do_a, p_a, ds_a = pair(qc_ref, doc_ref, lc_ref, dc_ref, kc_ref, vc_ref, m_cur)
        _, _, _, ds_b = pair(qc_ref, doc_ref, lc_ref, dc_ref, kp_ref, vp_ref, m_prev & (b > 0))
        q_c, do_c, p_c, ds_c = pair(qn_ref, don_ref, ln_ref, dn_ref, kc_ref, vc_ref, m_prev & (b < nb - 1))
        dq = _dot(ds_a, kc_ref[...]) + _dot(ds_b, kp_ref[...])
        dq_ref[...] = dq.reshape(r, BLK, dh)
        dk_ref[...] = _dot(ds_a, q_a, 0, 0) + _dot(ds_c, q_c, 0, 0)
        dv_ref[...] = _dot(p_a, do_a, 0, 0) + _dot(p_c, do_c, 0, 0)

    qcur = pl.BlockSpec((None, r, BLK, dh), lambda i, b: (i, 0, b, 0))
    qnext = pl.BlockSpec((None, r, BLK, dh), lambda i, b: (i, 0, jnp.minimum(b + 1, nb - 1), 0))
    cur = pl.BlockSpec((None, BLK, dh), lambda i, b: (i, b, 0))
    prev = pl.BlockSpec((None, BLK, dh), lambda i, b: (i, jnp.maximum(b - 1, 0), 0))
    return pl.pallas_call(
        body, name=name, grid=(n, nb),
        in_specs=[qcur, qnext, cur, prev, cur, prev, qcur, qnext, qcur, qnext, qcur, qnext],
        out_specs=[qcur, cur, cur],
        out_shape=[jax.ShapeDtypeStruct((n, r, l, dh), F32), jax.ShapeDtypeStruct((n, l, dh), F32),
                   jax.ShapeDtypeStruct((n, l, dh), F32)],
        compiler_params=_params(2),
    )(q, q, k, k, v, v, do, do, lse, lse, dvec, dvec)


def _scan_rows(t_len, step, init, reverse=False):
    n_chunks = t_len // 8

    def chunk(ci, carry):
        c = (n_chunks - 1 - ci) if reverse else ci
        base = pl.multiple_of(c * 8, 8)
        order = range(7, -1, -1) if reverse else range(8)
        return step(base, order, carry)
    return lax.fori_loop(0, n_chunks, chunk, init)


def _put_row(acc, i, row):
    rid = lax.broadcasted_iota(jnp.int32, acc.shape, 0)
    return jnp.where(rid == i, row, acc)


def _real_scan(a_ref, b_ref, h_ref, t_len, reverse=False):
    c = a_ref.shape[1]

    def step(base, order, h):
        a8 = a_ref[pl.ds(base, 8), :]
        b8 = b_ref[pl.ds(base, 8), :]
        out = jnp.zeros((8, c), F32)
        for i in order:
            h = a8[i:i + 1, :] * h + b8[i:i + 1, :]
            out = _put_row(out, i, h)
        h_ref[pl.ds(base, 8), :] = out
        return h
    _scan_rows(t_len, step, jnp.zeros((1, c), F32), reverse)


def _complex_scan(ar, ai, br_ref, bi_ref, sr_ref, si_ref, t_len, reverse=False):
    c = br_ref.shape[1]

    def step(base, order, carry):
        sr, si = carry
        br8 = br_ref[pl.ds(base, 8), :]
        bi8 = bi_ref[pl.ds(base, 8), :]
        outr = jnp.zeros((8, c), F32)
        outi = jnp.zeros((8, c), F32)
        for i in order:
            nr = ar * sr - ai * si + br8[i:i + 1, :]
            ni = ar * si + ai * sr + bi8[i:i + 1, :]
            sr, si = nr, ni
            outr = _put_row(outr, i, sr)
            outi = _put_row(outi, i, si)
        sr_ref[pl.ds(base, 8), :] = outr
        si_ref[pl.ds(base, 8), :] = outi
        return sr, si
    _scan_rows(t_len, step, (jnp.zeros((1, c), F32), jnp.zeros((1, c), F32)), reverse)


def _rglru_pre(xb, cw, cb, gaw, gab, gxw, gxb, lam):
    xc = cb + cw[3:4] * xb + cw[2:3] * _shift_down(xb, 1) + cw[1:2] * _shift_down(xb, 2) + cw[0:1] * _shift_down(xb, 3)
    r = _sigmoid(_dot(xc, gaw) + gab)
    ig = _sigmoid(_dot(xc, gxw) + gxb)
    sp = _softplus(-lam)
    log_a = -LRU_C * r * sp
    a = jnp.exp(log_a)
    mult = jnp.sqrt(_neg_expm1(2.0 * log_a))
    return xc, r, ig, sp, a, mult


def rglru_fwd(proj, cw, cb, gaw, gab, gxw, gxb, lam, xb_col, yb_col, name):
    t = proj.shape[0]
    nh = cw.shape[1] // LANE

    def body(xb_ref, yb_ref, cw_ref, cb_ref, gaw_ref, gab_ref, gxw_ref, gxb_ref, lam_ref, out_ref, h_ref, a_scr, b_scr):
        xc, r, ig, sp, a, mult = _rglru_pre(xb_ref[...], cw_ref[...], cb_ref[...], gaw_ref[...], gab_ref[...],
                                            gxw_ref[...], gxb_ref[...], lam_ref[...])
        a_scr[...] = a
        b_scr[...] = mult * (ig * xc)
        _real_scan(a_scr, b_scr, h_ref, t)
        out_ref[...] = (h_ref[...] * _gelu(yb_ref[...])).astype(out_ref.dtype)

    col = lambda off: pl.BlockSpec((t, LANE), lambda h: (0, off + h))
    vec = lambda rows: pl.BlockSpec((rows, LANE), lambda h: (0, h))
    wsp = pl.BlockSpec((None, LANE, LANE), lambda h: (h, 0, 0))
    return pl.pallas_call(
        body, name=name, grid=(nh,),
        in_specs=[col(xb_col), col(yb_col), vec(4), vec(1), wsp, vec(1), wsp, vec(1), vec(1)],
        out_specs=[col(0), col(0)],
        out_shape=[jax.ShapeDtypeStruct((t, nh * LANE), BF16), jax.ShapeDtypeStruct((t, nh * LANE), F32)],
        scratch_shapes=[pltpu.VMEM((t, LANE), F32)] * 2,
        compiler_params=_params(1),
    )(proj, proj, cw, cb, gaw, gab, gxw, gxb, lam)


def rglru_bwd(proj, hs, dlru, dlru_col, cw, cb, gaw, gab, gxw, gxb, lam, xb_col, yb_col, name):
    t = proj.shape[0]
    nh = cw.shape[1] // LANE

    def body(xb_ref, yb_ref, h_ref, dl_ref, cw_ref, cb_ref, gaw_ref, gab_ref, gxw_ref, gxb_ref, lam_ref,
             dxb_ref, dyb_ref, dcw_ref, dcb_ref, dgaw_ref, dgab_ref, dgxw_ref, dgxb_ref, dlam_ref, an_scr, dh_scr, gh_scr):
        xb = xb_ref[...]
        yb = yb_ref[...]
        cwv = cw_ref[...]
        lam = lam_ref[...]
        xc, r, ig, sp, a, mult = _rglru_pre(xb, cwv, cb_ref[...], gaw_ref[...], gab_ref[...], gxw_ref[...], gxb_ref[...], lam)
        h = h_ref[...]
        dl = dl_ref[...]
        dyb_ref[...] = (dl * h * _gelu_grad(yb)).astype(dyb_ref.dtype)
        dh_scr[...] = dl * _gelu(yb)
        an_scr[...] = _shift_up(a, 1)
        _real_scan(an_scr, dh_scr, gh_scr, t, reverse=True)
        gh = gh_scr[...]
        da = gh * _shift_down(h, 1)
        dmult = gh * ig * xc
        dig = gh * mult * xc
        dxc = gh * mult * ig
        dla = (da - dmult * a / mult) * a
        dr = dla * (-LRU_C * sp)
        dsp = jnp.sum(dla * (-LRU_C * r), axis=0, keepdims=True)
        dlam_ref[...] = dsp * (-_sigmoid(-lam))
        dpr = dr * r * (1.0 - r)
        dpi = dig * ig * (1.0 - ig)
        dgab_ref[...] = jnp.sum(dpr, axis=0, keepdims=True)
        dgxb_ref[...] = jnp.sum(dpi, axis=0, keepdims=True)
        dgaw_ref[...] = _dot(xc, dpr, 0, 0)
        dgxw_ref[...] = _dot(xc, dpi, 0, 0)
        dxc = dxc + _dot(dpr, gaw_ref[...], 1, 1) + _dot(dpi, gxw_ref[...], 1, 1)
        dxb = cwv[3:4] * dxc + cwv[2:3] * _shift_up(dxc, 1) + cwv[1:2] * _shift_up(dxc, 2) + cwv[0:1] * _shift_up(dxc, 3)
        dxb_ref[...] = dxb.astype(dxb_ref.dtype)
        dcw_ref[...] = jnp.concatenate([jnp.sum(dxc * _shift_down(xb, 3 - i), axis=0, keepdims=True) for i in range(4)], axis=0)
        dcb_ref[...] = jnp.sum(dxc, axis=0, keepdims=True)

    col = lambda off: pl.BlockSpec((t, LANE), lambda h: (0, off + h))
    vec = lambda rows: pl.BlockSpec((rows, LANE), lambda h: (0, h))
    wsp = pl.BlockSpec((None, LANE, LANE), lambda h: (h, 0, 0))
    w = nh * LANE
    sds = jax.ShapeDtypeStruct
    return pl.pallas_call(
        body, name=name, grid=(nh,),
        in_specs=[col(xb_col), col(yb_col), col(0), col(dlru_col), vec(4), vec(1), wsp, vec(1), wsp, vec(1), vec(1)],
        out_specs=[col(0), col(0), vec(4), vec(1), wsp, vec(1), wsp, vec(1), vec(1)],
        out_shape=[sds((t, w), BF16), sds((t, w), BF16), sds((4, w), F32), sds((1, w), F32), sds((nh, LANE, LANE), F32),
                   sds((1, w), F32), sds((nh, LANE, LANE), F32), sds((1, w), F32), sds((1, w), F32)],
        scratch_shapes=[pltpu.VMEM((t, LANE), F32)] * 3,
        compiler_params=_params(1),
    )(proj, proj, hs, dlru, cw, cb, gaw, gab, gxw, gxb, lam)


S5_CHUNKS = 8
S5_STATES = 512


def s5_fwd(proj, u_col, bre, bim, are, aim, cre, cim, dsk, name):
    t = proj.shape[0]

    def body(u_ref, bre_ref, bim_ref, are_ref, aim_ref, cre_ref, cim_ref, d_ref, y_ref, sr_ref, si_ref, br_scr, bi_scr):
        u = u_ref[...]
        br_scr[...] = _dot(u, bre_ref[...])
        bi_scr[...] = _dot(u, bim_ref[...])
        _complex_scan(are_ref[...], aim_ref[...], br_scr, bi_scr, sr_ref, si_ref, t)
        y_ref[...] = _dot(sr_ref[...], cre_ref[...]) - _dot(si_ref[...], cim_ref[...]) + d_ref[...] * u

    ucol = pl.BlockSpec((t, LANE), lambda c: (0, u_col + c))
    ycol = pl.BlockSpec((t, LANE), lambda c: (0, c))
    scol = pl.BlockSpec((t, S5_STATES), lambda c: (0, c))
    bsp = pl.BlockSpec((None, LANE, S5_STATES), lambda c: (c, 0, 0))
    csp = pl.BlockSpec((None, S5_STATES, LANE), lambda c: (c, 0, 0))
    asp = pl.BlockSpec((1, S5_STATES), lambda c: (0, c))
    dsp = pl.BlockSpec((1, LANE), lambda c: (0, c))
    sds = jax.ShapeDtypeStruct
    return pl.pallas_call(
        body, name=name, grid=(S5_CHUNKS,),
        in_specs=[ucol, bsp, bsp, asp, asp, csp, csp, dsp],
        out_specs=[ycol, scol, scol],
        out_shape=[sds((t, S5_CHUNKS * LANE), F32), sds((t, S5_CHUNKS * S5_STATES), F32), sds((t, S5_CHUNKS * S5_STATES), F32)],
        scratch_shapes=[pltpu.VMEM((t, S5_STATES), F32)] * 2,
        compiler_params=_params(1),
    )(proj, bre, bim, are, aim, cre, cim, dsk)


def s5_bwd(proj, u_col, dy, sr, si, bre, bim, are, aim, cre, cim, dsk, name):
    t = proj.shape[0]
    half = S5_STATES // 2

    def body(u_ref, dy_ref, sr_ref, si_ref, bre_ref, bim_ref, are_ref, aim_ref, cre_ref, cim_ref, d_ref,
             du_ref, dbre_ref, dbim_ref, dare_ref, daim_ref, dcre_ref, dcim_ref, dd_ref, dsr_scr, dsi_scr, gr_scr, gi_scr):
        hh = pl.program_id(1)
        u = u_ref[...]
        dy = dy_ref[...]
        dsr_scr[...] = _dot(dy, cre_ref[...], 1, 1)
        dsi_scr[...] = -_dot(dy, cim_ref[...], 1, 1)
        _complex_scan(are_ref[...], -aim_ref[...], dsr_scr, dsi_scr, gr_scr, gi_scr, t, reverse=True)
        gr = gr_scr[...]
        gi = gi_scr[...]
        spr = _shift_down(sr_ref[...], 1)
        spi = _shift_down(si_ref[...], 1)
        dare_ref[...] = jnp.sum(gr * spr + gi * spi, axis=0, keepdims=True)
        daim_ref[...] = jnp.sum(gi * spr - gr * spi, axis=0, keepdims=True)
        du = _dot(gr, bre_ref[...], 1, 1) + _dot(gi, bim_ref[...], 1, 1)

        @pl.when(hh == 0)
        def _():
            du_ref[...] = du + d_ref[...] * dy

        @pl.when(hh > 0)
        def _():
            du_ref[...] += du

        dbre_ref[...] = _dot(u, gr, 0, 0)
        dbim_ref[...] = _dot(u, gi, 0, 0)
        dcre_ref[...] = _dot(sr_ref[...], dy, 0, 0)
        dcim_ref[...] = -_dot(si_ref[...], dy, 0, 0)
        dd_ref[...] = jnp.sum(dy * u, axis=0, keepdims=True)

    ucol = pl.BlockSpec((t, LANE), lambda c, h: (0, u_col + c))
    ycol = pl.BlockSpec((t, LANE), lambda c, h: (0, c))
    scol = pl.BlockSpec((t, half), lambda c, h: (0, 2 * c + h))
    bsp = pl.BlockSpec((None, LANE, half), lambda c, h: (c, 0, h))
    csp = pl.BlockSpec((None, half, LANE), lambda c, h: (c, h, 0))
    asp = pl.BlockSpec((1, half), lambda c, h: (0, 2 * c + h))
    dsp = pl.BlockSpec((1, LANE), lambda c, h: (0, c))
    sds = jax.ShapeDtypeStruct
    return pl.pallas_call(
        body, name=name, grid=(S5_CHUNKS, 2),
        in_specs=[ucol, ycol, scol, scol, bsp, bsp, asp, asp, csp, csp, dsp],
        out_specs=[ycol, bsp, bsp, asp, asp, csp, csp, dsp],
        out_shape=[sds((t, S5_CHUNKS * LANE), F32), sds((S5_CHUNKS, LANE, S5_STATES), F32), sds((S5_CHUNKS, LANE, S5_STATES), F32),
                   sds((1, S5_CHUNKS * S5_STATES), F32), sds((1, S5_CHUNKS * S5_STATES), F32),
                   sds((S5_CHUNKS, S5_STATES, LANE), F32), sds((S5_CHUNKS, S5_STATES, LANE), F32), sds((1, S5_CHUNKS * LANE), F32)],
        scratch_shapes=[pltpu.VMEM((t, half), F32)] * 4,
        compiler_params=_params(2),
    )(proj, dy, sr, si, bre, bim, are, aim, cre, cim, dsk)


def s5_prep(a_re, a_im, b_re, b_im, c_re, c_im, log_dt):
    lam = lax.complex(a_re, a_im)
    dt = jnp.exp(log_dt)[:, None]
    a_bar = jnp.exp(lam * dt)
    b_bar = ((a_bar - 1.0) / lam)[..., None] * lax.complex(b_re, b_im)
    g, p, cg = b_re.shape
    eye = jnp.eye(8, dtype=F32)

    def in_map(m):
        m = m.reshape(g // 8, 8, p, cg)
        return jnp.einsum("ab,kapc->kacbp", eye, m).reshape(g // 8, 8 * cg, 8 * p)

    def out_map(m):
        m = m.reshape(g // 8, 8, cg, p)
        return jnp.einsum("ab,kacp->kapbc", eye, m).reshape(g // 8, 8 * p, 8 * cg)

    return (jnp.real(a_bar).reshape(1, g * p), jnp.imag(a_bar).reshape(1, g * p), in_map(jnp.real(b_bar)), in_map(jnp.imag(b_bar)),
            out_map(c_re), out_map(c_im))


def rope_tables(pos, half):
    inv = ROPE_THETA ** (-jnp.arange(half, dtype=F32) / half)
    ang = pos.astype(F32)[:, None] * inv
    cos, sin = jnp.cos(ang), jnp.sin(ang)
    reps = max(LANE // (2 * half), 1)
    return jnp.tile(jnp.concatenate([cos, cos], axis=1), (1, reps)), jnp.tile(jnp.concatenate([-sin, sin], axis=1), (1, reps))


A_W = 1024
ROW_T = 256


def _tiled(a, width, col):
    return (a, (ROW_T, width), lambda i: (i, col))


def _out_tiled(t, width, dtype):
    return ((t, width), dtype, (ROW_T, width), lambda i: (i, 0), False)


def qkv_rope_even(proj, cos, sin, name):
    t = proj.shape[0]

    def fn(q, k, v, cos, sin):
        return _rope(q, cos, sin, 64), _rope(k, cos, sin, 64), v
    ins = [_tiled(proj, A_W, 0), _tiled(proj, A_W, 1), _tiled(proj, A_W, 2), _tiled(cos, LANE, 0), _tiled(sin, LANE, 0)]
    return _tile_call(name, fn, (t // ROW_T,), ins, [_out_tiled(t, A_W, BF16)] * 3)


def merge3(o, lse, name):
    t = o[0].shape[0]

    def fn(o1, o2, o3, l1, l2, l3):
        mx = jnp.maximum(jnp.maximum(l1, l2), l3)
        e1, e2, e3 = jnp.exp(l1 - mx), jnp.exp(l2 - mx), jnp.exp(l3 - mx)
        den = e1 + e2 + e3
        out = (e1 * o1 + e2 * o2 + e3 * o3) / den
        return out, out, mx + jnp.log(den)
    ins = [_tiled(a, A_W, 0) for a in list(o) + list(lse)]
    return _tile_call(name, fn, (t // ROW_T,), ins, [_out_tiled(t, A_W, BF16), _out_tiled(t, A_W, F32), _out_tiled(t, A_W, F32)])


def _segsum_bcast(x, width):
    parts = []
    for h in range(x.shape[1] // width):
        s = jnp.sum(x[:, h * width:(h + 1) * width], axis=1, keepdims=True)
        parts.append(jnp.broadcast_to(s, (x.shape[0], width)))
    return jnp.concatenate(parts, axis=1)


def even_attn_prep(dmix, attn, name):
    t = attn.shape[0]

    def fn(dout, attn):
        return dout, _segsum_bcast(dout * attn, LANE)
    ins = [_tiled(dmix, A_W, 0), _tiled(attn, A_W, 0)]
    return _tile_call(name, fn, (t // ROW_T,), ins, [_out_tiled(t, A_W, BF16), _out_tiled(t, A_W, F32)])


def even_dproj(dq, dk, dv, dxb, dyb, cos, sin, name):
    t = dxb.shape[0]

    def fn(q1, q2, q3, k1, k2, k3, v1, v2, v3, dxb, dyb, cos, sin):
        return jnp.concatenate([_rope_t(q1 + q2 + q3, cos, sin, 64).astype(BF16), _rope_t(k1 + k2 + k3, cos, sin, 64).astype(BF16),
                                (v1 + v2 + v3).astype(BF16), dxb, dyb], axis=1)
    ins = [_tiled(a, A_W, 0) for a in list(dq) + list(dk) + list(dv) + [dxb, dyb]] + [_tiled(cos, LANE, 0), _tiled(sin, LANE, 0)]
    return _tile_call(name, fn, (t // ROW_T,), ins, [_out_tiled(t, 5 * A_W, BF16)])[0]


def perm(x, d):
    t = x.shape[0]
    return x.reshape(t // d, d, 8, LANE).transpose(1, 2, 0, 3).reshape(d * 8, t // d, LANE)


def unperm(xp, d):
    n, l, _ = xp.shape
    return xp.reshape(d, 8, l, LANE).transpose(2, 0, 1, 3).reshape(l * d, 8 * LANE)


def qkv_rope_odd(proj, cos, sin, name):
    t = proj.shape[0]

    def fn(q, k, v, cos, sin):
        return _rope(q, cos, sin, 32), _rope(k, cos, sin, 32), v
    ins = [_tiled(proj, A_W, 0), _tiled(proj, LANE, 8), _tiled(proj, LANE, 9), _tiled(cos, LANE, 0), _tiled(sin, LANE, 0)]
    return _tile_call(name, fn, (t // ROW_T,), ins, [_out_tiled(t, A_W, BF16), _out_tiled(t, LANE, BF16), _out_tiled(t, LANE, BF16)])


def _head_blocks(a):
    return (a, (None, ROW_T, a.shape[2]), lambda h, i: (h, i, 0))


def sink_fwd(o, lse, sink_b, name):
    nh, t, dh = o.shape

    def fn(o, lse, s):
        return o * _sigmoid(lse - s)
    ins = [_head_blocks(o), _head_blocks(lse), (sink_b, (None, 1, dh), lambda h, i: (h, 0, 0))]
    return _tile_call(name, fn, (nh, t // ROW_T), ins, [((nh, t, dh), BF16, (None, ROW_T, dh), lambda h, i: (h, i, 0), False)])[0]


def sink_bwd(dof, o, lse, sink_b, name):
    nh, t, dh = o.shape

    def fn(dof, o, lse, s):
        keep = _sigmoid(lse - s)
        dk = jnp.sum(dof * o, axis=1, keepdims=True)
        dlse = dk * keep * (1.0 - keep)
        return dof * keep, dk * keep * keep, -jnp.sum(dlse, axis=0, keepdims=True)
    ins = [_head_blocks(dof), _head_blocks(o), _head_blocks(lse), (sink_b, (None, 1, dh), lambda h, i: (h, 0, 0))]
    outs = [((nh, t, dh), BF16, (None, ROW_T, dh), lambda h, i: (h, i, 0), False),
            ((nh, t, dh), F32, (None, ROW_T, dh), lambda h, i: (h, i, 0), False),
            ((nh, 1, dh), F32, (None, 1, dh), lambda h, i: (h, 0, 0), True)]
    return _tile_call(name, fn, (nh, t // ROW_T), ins, outs, acc_axis=1)


def odd_dproj(dq, dk, dv, du, cos, sin, name):
    t = dq.shape[0]

    def fn(dq, dk, dv, du, cos, sin):
        return jnp.concatenate([_rope_t(dq, cos, sin, 32), _rope_t(dk, cos, sin, 32), dv, du], axis=1)
    ins = [_tiled(dq, A_W, 0), _tiled(dk, LANE, 0), _tiled(dv, LANE, 0), _tiled(du, A_W, 0), _tiled(cos, LANE, 0), _tiled(sin, LANE, 0)]
    return _tile_call(name, fn, (t // ROW_T,), ins, [_out_tiled(t, 2 * A_W + 2 * LANE, BF16)])[0]


def glu_z(y, name):
    return _rows_call(name, _gelu, y.shape[0], [y], [], [(y.shape[1], BF16)])[0]


def glu_out(y, gpre, b, name):
    def fn(y, gpre, b):
        return _gelu(y) * _sigmoid(gpre + b)
    return _rows_call(name, fn, y.shape[0], [y, gpre], [b], [(y.shape[1], BF16)])[0]


def glu_bwd_gate(dmix, y, gpre, b, name):
    t = y.shape[0]

    def fn(dout, y, gpre, b):
        gate = _sigmoid(gpre + b)
        dgp = dout * _gelu(y) * gate * (1.0 - gate)
        return dgp, jnp.sum(dgp, axis=0, keepdims=True)
    ins = [_tiled(dmix, A_W, 1), _tiled(y, A_W, 0), _tiled(gpre, A_W, 0), (b, (1, A_W), lambda i: (0, 0))]
    outs = [_out_tiled(t, A_W, BF16), ((1, A_W), F32, (1, A_W), lambda i: (0, 0), True)]
    return _tile_call(name, fn, (t // ROW_T,), ins, outs, acc_axis=0)


def glu_bwd_y(dmix, y, gpre, b, dz_mm, name):
    t = y.shape[0]

    def fn(dout, y, gpre, b, dz_mm):
        return (dout * _sigmoid(gpre + b) + dz_mm) * _gelu_grad(y)
    ins = [_tiled(dmix, A_W, 1), _tiled(y, A_W, 0), _tiled(gpre, A_W, 0), (b, (1, A_W), lambda i: (0, 0)), _tiled(dz_mm, A_W, 0)]
    return _tile_call(name, fn, (t // ROW_T,), ins, [_out_tiled(t, A_W, F32)])[0]


def adamw(w, g, m, v, name):
    def fn(w, g, m, v):
        m = ADAM_B1 * m + (1.0 - ADAM_B1) * g
        v = ADAM_B2 * v + (1.0 - ADAM_B2) * (g * g)
        m_hat = m / (1.0 - ADAM_B1 ** ADAM_STEP)
        v_hat = v / (1.0 - ADAM_B2 ** ADAM_STEP)
        return -ADAM_LR * (m_hat / (jnp.sqrt(v_hat) + ADAM_EPS) + ADAM_WD * w), m, v
    c = w.shape[1]
    return _rows_call(name, fn, w.shape[0], [w, g, m, v], [], [(c, F32)] * 3)


def _sum_in_order(v):
    s = v[0].astype(F32)
    for d in range(1, v.shape[0]):
        s = s + v[d].astype(F32)
    return s


def sum_devices(parts, name):
    nd, nl, r, c = parts.shape
    tr = 8
    while tr * 2 <= 256 and r % (tr * 2) == 0 and tr * 2 * c * 4 * nd <= 4 * 1024 * 1024:
        tr *= 2
    ins = [(parts, (nd, None, tr, c), lambda l, i: (0, l, i, 0))]
    outs = [((nl, r, c), F32, (None, tr, c), lambda l, i: (l, i, 0), False)]
    return _tile_call(name, _sum_in_order, (nl, r // tr), ins, outs)[0]


def silu_rows(c_all, name):
    def fn(c):
        return c * _sigmoid(c)
    return _rows_call(name, fn, c_all.shape[0], [c_all], [], [(c_all.shape[1], F32)])[0]


def _place():
    x, y, c = lax.axis_index("x"), lax.axis_index("y"), lax.axis_index("c")
    return x, y, c


ANY = pl.BlockSpec(memory_space=pl.ANY)


def all_gather8(v, name):
    r, cdim = v.shape

    def body(x_ref, out_ref, send_sems, recv_sems, local_sem):
        x, y, c = _place()
        me, sibling = (x, y, c), (x, y, 1 - c)
        chips = [(1 - x, y), (x, 1 - y), (1 - x, 1 - y)]

        def rows(px, py, pc):
            return out_ref.at[4 * px + 2 * py + pc]

        def copy(k, block, to, src=None):
            return pltpu.make_async_remote_copy(
                src_ref=rows(*block) if src is None else src, dst_ref=rows(*block),
                send_sem=send_sems.at[k], recv_sem=recv_sems.at[k], device_id=to, device_id_type=MESH)

        mine = pltpu.make_async_copy(x_ref, rows(*me), local_sem)
        mine.start()
        first = [copy(0, me, sibling, src=x_ref)]
        first += [copy(1 + j, me, (*chip, c), src=x_ref) for j, chip in enumerate(chips)]
        for cp in first:
            cp.start()
        passed = [copy(4 + j, (*chip, c), sibling) for j, chip in enumerate(chips)]
        for j, chip in enumerate(chips):
            copy(1 + j, (*chip, c), me).wait_recv()
            passed[j].start()
        copy(0, sibling, me).wait_recv()
        for j, chip in enumerate(chips):
            copy(4 + j, (*chip, 1 - c), me).wait_recv()
        for cp in first + passed:
            cp.wait_send()
        mine.wait()

    return pl.pallas_call(
        body, name=name, out_shape=jax.ShapeDtypeStruct((N_DEV, r, cdim), v.dtype),
        in_specs=[ANY], out_specs=ANY,
        scratch_shapes=[pltpu.SemaphoreType.DMA((7,)), pltpu.SemaphoreType.DMA((7,)), pltpu.SemaphoreType.DMA],
    )(v)


def gather_weights(shards, name):
    n = len(shards)

    def body(*refs):
        ins, outs = refs[:n], refs[n:2 * n]
        send_sems, recv_sems = refs[2 * n:]
        x, y, c = _place()
        sibling = (x, y, 1 - c)
        chips = [(1 - x, y), (x, 1 - y), (1 - x, 1 - y)]
        my_chip = 2 * x + y

        def half(t, chip_slot, start):
            hr = ins[t].shape[1] // 2
            return outs[t].at[chip_slot, :, pl.ds(start, hr), :]

        def copy(t, k, src, dst, to):
            return pltpu.make_async_remote_copy(src_ref=src, dst_ref=dst, send_sem=send_sems.at[6 * t + k],
                                                recv_sem=recv_sems.at[6 * t + k], device_id=to, device_id_type=MESH)

        def lows(t):
            hr = ins[t].shape[1] // 2
            return hr, pl.multiple_of(c * hr, 16), pl.multiple_of((1 - c) * hr, 16)

        started = []
        for t in range(n):
            hr, lo, _ = lows(t)
            for j, chip in enumerate(chips):
                cp = copy(t, j, ins[t].at[:, pl.ds(lo, hr), :], half(t, my_chip, lo), (*chip, c))
                cp.start()
                started.append(cp)
        for t in range(n):
            hr, lo, _ = lows(t)
            for j, (px, py) in enumerate(chips):
                slot = 2 * px + py
                copy(t, j, half(t, slot, lo), half(t, slot, lo), (px, py, c)).wait_recv()
                fwd = copy(t, 3 + j, half(t, slot, lo), half(t, slot, lo), sibling)
                fwd.start()
                started.append(fwd)
        for t in range(n):
            hr, _, lo_sib = lows(t)
            for j, (px, py) in enumerate(chips):
                slot = 2 * px + py
                copy(t, 3 + j, half(t, slot, lo_sib), half(t, slot, lo_sib), sibling).wait_recv()
        for cp in started:
            cp.wait_send()

    got = pl.pallas_call(
        body, name=name,
        out_shape=[jax.ShapeDtypeStruct((N_CHIP,) + s.shape, s.dtype) for s in shards],
        in_specs=[ANY] * n, out_specs=[ANY] * n,
        scratch_shapes=[pltpu.SemaphoreType.DMA((6 * n,)), pltpu.SemaphoreType.DMA((6 * n,))],
    )(*shards)
    my_chip = 2 * lax.axis_index("x") + lax.axis_index("y")
    return [lax.dynamic_update_index_in_dim(g, s, my_chip, 0) for g, s in zip(got, shards)]


def pair_swap(grads, name):
    n = len(grads)

    def body(*refs):
        ins, outs = refs[:n], refs[n:2 * n]
        send_sems, recv_sems = refs[2 * n:]
        x, y, c = _place()
        sibling = (x, y, 1 - c)
        sends = []
        for t in range(n):
            for q in range(N_CHIP):
                cp = pltpu.make_async_remote_copy(src_ref=ins[t].at[1 - c, q], dst_ref=outs[t].at[q], send_sem=send_sems.at[N_CHIP * t + q],
                                                  recv_sem=recv_sems.at[N_CHIP * t + q], device_id=sibling, device_id_type=MESH)
                cp.start()
                sends.append(cp)
        for cp in sends:
            cp.wait_recv()
            cp.wait_send()

    return pl.pallas_call(
        body, name=name,
        out_shape=[jax.ShapeDtypeStruct(g.shape[1:], g.dtype) for g in grads],
        in_specs=[ANY] * n, out_specs=[ANY] * n,
        scratch_shapes=[pltpu.SemaphoreType.DMA((N_CHIP * n,)), pltpu.SemaphoreType.DMA((N_CHIP * n,))],
    )(*grads)


def pair_sum(g5, from_sibling, core, name):
    _, nq, nl, hr, c = g5.shape
    tr = 8
    while tr * 2 <= 512 and hr % (tr * 2) == 0 and tr * 2 * c * 2 * 4 <= 4 * 1024 * 1024:
        tr *= 2
    flag = jnp.broadcast_to(core.astype(F32), (8, LANE))

    def fn(g0, g1, r, flag):
        own = jnp.where(flag[0:1, 0:1] == 0.0, g0.astype(F32), g1.astype(F32))
        return own + r.astype(F32)
    ins = [(g5, (None, None, None, tr, c), lambda q, l, i: (0, q, l, i, 0)), (g5, (None, None, None, tr, c), lambda q, l, i: (1, q, l, i, 0)),
           (from_sibling, (None, None, tr, c), lambda q, l, i: (q, l, i, 0)), (flag, (8, LANE), lambda q, l, i: (0, 0))]
    outs = [((nq, nl, hr, c), BF16, (None, None, tr, c), lambda q, l, i: (q, l, i, 0), False)]
    return _tile_call(name, fn, (nq, nl, hr // tr), ins, outs)[0]


def exchange_chips(pairs, name):
    n = len(pairs)

    def body(*refs):
        ins, outs = refs[:n], refs[n:2 * n]
        send_sems, recv_sems = refs[2 * n:]
        x, y, c = _place()
        chips = [(1 - x, y), (x, 1 - y), (1 - x, 1 - y)]
        my_chip = 2 * x + y
        sends = []
        for t in range(n):
            for j, (px, py) in enumerate(chips):
                cp = pltpu.make_async_remote_copy(src_ref=ins[t].at[2 * px + py], dst_ref=outs[t].at[my_chip], send_sem=send_sems.at[3 * t + j],
                                                  recv_sem=recv_sems.at[3 * t + j], device_id=(px, py, c), device_id_type=MESH)
                cp.start()
                sends.append(cp)
        for t in range(n):
            for j, (px, py) in enumerate(chips):
                slot = outs[t].at[2 * px + py]
                pltpu.make_async_remote_copy(src_ref=slot, dst_ref=slot, send_sem=send_sems.at[3 * t + j], recv_sem=recv_sems.at[3 * t + j],
                                             device_id=(px, py, c), device_id_type=MESH).wait_recv()
        for cp in sends:
            cp.wait_send()

    got = pl.pallas_call(
        body, name=name,
        out_shape=[jax.ShapeDtypeStruct(p.shape, p.dtype) for p in pairs],
        in_specs=[ANY] * n, out_specs=[ANY] * n,
        scratch_shapes=[pltpu.SemaphoreType.DMA((3 * n,)), pltpu.SemaphoreType.DMA((3 * n,))],
    )(*pairs)
    my_chip = 2 * lax.axis_index("x") + lax.axis_index("y")
    return [lax.dynamic_update_index_in_dim(o, lax.dynamic_index_in_dim(p, my_chip, 0, keepdims=False), my_chip, 0) for o, p in zip(got, pairs)]


def join_halves(halves, name):
    n = len(halves)
    chunks = [(t, l, j) for t in range(n) for l in range(halves[t].shape[0]) for j in range(2)]

    def body(*refs):
        ins, outs = refs[:n], refs[n:2 * n]
        send_sems, recv_sems = refs[2 * n:]
        x, y, c = _place()
        sibling = (x, y, 1 - c)
        pending = []
        for k, (t, l, j) in enumerate(chunks):
            h_ref, o_ref = ins[t], outs[t]
            hr = h_ref.shape[1]
            rows = hr // 2
            lo = pl.multiple_of(c * hr + j * rows, 8)
            lo_sib = pl.multiple_of((1 - c) * hr + j * rows, 8)
            src = h_ref.at[l, pl.ds(j * rows, rows), :]
            cp = pltpu.make_async_remote_copy(src_ref=src, dst_ref=o_ref.at[l, pl.ds(lo, rows), :], send_sem=send_sems.at[k],
                                              recv_sem=recv_sems.at[k], device_id=sibling, device_id_type=MESH)
            cp.start()
            got = pltpu.make_async_remote_copy(src_ref=src, dst_ref=o_ref.at[l, pl.ds(lo_sib, rows), :], send_sem=send_sems.at[k],
                                               recv_sem=recv_sems.at[k], device_id=sibling, device_id_type=MESH)
            pending.append((cp, got))
        for cp, got in pending:
            got.wait_recv()
            cp.wait_send()

    got = pl.pallas_call(
        body, name=name,
        out_shape=[jax.ShapeDtypeStruct((h.shape[0], 2 * h.shape[1], h.shape[2]), h.dtype) for h in halves],
        in_specs=[ANY] * n, out_specs=[ANY] * n,
        scratch_shapes=[pltpu.SemaphoreType.DMA((len(chunks),)), pltpu.SemaphoreType.DMA((len(chunks),))],
    )(*halves)
    ci = lax.axis_index("c")
    return [lax.dynamic_update_slice(g, h, (0, ci * h.shape[1], 0)) for g, h in zip(got, halves)]


WEIGHTS = ['ada_w', 'ada_b', 'norm_mix', 'norm_ffn', 'norm_final', 'ev_w_in', 'ev_conv_w', 'ev_conv_b', 'ev_gate_a_w', 'ev_gate_a_b',
           'ev_gate_x_w', 'ev_gate_x_b', 'ev_lambda', 'ev_w_out', 'od_w_in', 'od_sinks', 'od_a_re', 'od_a_im', 'od_b_re', 'od_b_im',
           'od_c_re', 'od_c_im', 'od_d', 'od_log_dt', 'od_glu_w', 'od_glu_b', 'od_w_out', 'ffn_w_in', 'ffn_conv_w', 'ffn_conv_b', 'ffn_w_out']
BIG = ['ev_w_in', 'ev_w_out', 'od_w_in', 'od_glu_w', 'od_w_out', 'ffn_w_in', 'ffn_w_out']
COL_SHARDED = ('ev_w_in', 'od_w_in', 'ffn_w_in')
SMALL_SHARDED = ['ev_conv_w', 'od_d', 'od_glu_b', 'ffn_conv_w']
SMALL = [n for n in WEIGHTS if n not in BIG and n != 'ada_w']


def _pack(arrs):
    flat = jnp.concatenate([a.reshape(-1).astype(F32) for a in arrs])
    rows = -(-flat.shape[0] // (1024 * LANE)) * 1024
    return jnp.pad(flat, (0, rows * LANE - flat.shape[0])).reshape(rows, LANE)


def _unpack(flat, shapes):
    out, off = [], 0
    for s in shapes:
        n = math.prod(s)
        out.append(flat[..., off:off + n].reshape(flat.shape[:-1] + tuple(s)))
        off += n
    return out


def _ffn_fwd(l, h2, wf, cw, cb):
    u = mm(h2, wf['ffn_w_in'][l], tm=2048, tn=256, name=f"ffn_in{l}")
    act = ffn_act(u, cw, cb, f"ffn_act{l}")
    f = mm(act, wf['ffn_w_out'][l], tm=1024, tn=512, name=f"ffn_out{l}")
    return f, dict(u=u, act=act)


def _ffn_bwd(l, df, s, h2, wf, cw, cb):
    dact = mm(df, wf['ffn_w_out'][l], tb=True, tm=2048, tn=128, name=f"ffn_dact{l}")
    dwo = mm(s['act'], df, ta=True, out_dtype=BF16, tm=D_FF, tn=256, tk=512, name=f"ffn_dwo{l}")
    du_g, du_v, dcw_g, dcw_v, dcb_g, dcb_v = ffn_act_bwd(s['u'], dact, cw, cb, f"ffn_act_bwd{l}")
    du = jnp.concatenate([du_g, du_v], axis=1)
    dh2 = mm(du, wf['ffn_w_in'][l], tb=True, tm=1024, tn=512, tk=D_FF, name=f"ffn_dh{l}")
    dwi = mm(h2, du, ta=True, out_dtype=BF16, tm=2048, tn=256, name=f"ffn_dwi{l}")
    return dh2, dwi, dwo, jnp.concatenate([dcw_g, dcw_v], axis=1), jnp.concatenate([dcb_g, dcb_v], axis=1)


def _even_fwd(e, h1, a, wf, fs, tabs):
    cos, sin = tabs
    proj = mm(h1, wf['ev_w_in'][e], name=f"ev_in{e}")
    q, k, v = qkv_rope_even(proj, cos, sin, f"ev_rope{e}")
    outs, lses = [], []
    for window, d in A_PATTERNS:
        o, lse = attn_fwd(perm(q, d)[:, None], perm(k, d), perm(v, d), window // d, LANE ** -0.5, f"ev_attn{e}_{d}")
        outs.append(unperm(o[:, 0], d))
        lses.append(unperm(lse[:, 0], d))
    attn_bf, attn, lse_tot = merge3(outs, lses, f"ev_merge{e}")
    lru, hs = rglru_fwd(proj, fs['ev_conv_w'][e], a['ev_conv_b'][e][None], a['ev_gate_a_w'][e], a['ev_gate_a_b'][e][None],
                        a['ev_gate_x_w'][e], a['ev_gate_x_b'][e][None], a['ev_lambda'][e][None], 24, 32, f"ev_lru{e}")
    mix = jnp.concatenate([attn_bf, lru], axis=1)
    y = mm(mix, wf['ev_w_out'][e], name=f"ev_out{e}")
    return y, dict(proj=proj, q=q, k=k, v=v, attn=attn, lse=lse_tot, hs=hs, mix=mix)


def _even_bwd(e, dyg, s, h1, a, wf, fs, tabs, gs):
    cos, sin = tabs
    dmix = mm(dyg, wf['ev_w_out'][e], tb=True, name=f"ev_dmix{e}")
    dwo = mm(s['mix'], dyg, ta=True, out_dtype=BF16, name=f"ev_dwo{e}")
    do_bf, dvec = even_attn_prep(dmix, s['attn'], f"ev_prep{e}")
    dqs, dks, dvs = [], [], []
    for window, d in A_PATTERNS:
        dq, dk, dv = attn_bwd(perm(s['q'], d)[:, None], perm(s['k'], d), perm(s['v'], d), perm(do_bf, d)[:, None],
                              perm(s['lse'], d)[:, None], perm(dvec, d)[:, None], window // d, LANE ** -0.5, f"ev_attn_bwd{e}_{d}")
        dqs.append(unperm(dq[:, 0], d))
        dks.append(unperm(dk, d))
        dvs.append(unperm(dv, d))
    dxb, dyb, dcw, dcb, dgaw, dgab, dgxw, dgxb, dlam = rglru_bwd(
        s['proj'], s['hs'], dmix, 8, fs['ev_conv_w'][e], a['ev_conv_b'][e][None], a['ev_gate_a_w'][e], a['ev_gate_a_b'][e][None],
        a['ev_gate_x_w'][e], a['ev_gate_x_b'][e][None], a['ev_lambda'][e][None], 24, 32, f"ev_lru_bwd{e}")
    for n, g in (('ev_conv_w', dcw), ('ev_conv_b', dcb[0]), ('ev_gate_a_w', dgaw), ('ev_gate_a_b', dgab[0]), ('ev_gate_x_w', dgxw),
                 ('ev_gate_x_b', dgxb[0]), ('ev_lambda', dlam[0])):
        gs[n][e] = g
    dproj = even_dproj(dqs, dks, dvs, dxb, dyb, cos, sin, f"ev_dproj{e}")
    dh1 = mm(dproj, wf['ev_w_in'][e], tb=True, tk=2560, name=f"ev_dh{e}")
    dwi = mm(h1, dproj, ta=True, out_dtype=BF16, tm=2048, tn=512, name=f"ev_dwi{e}")
    return dh1, dwi, dwo


def _odd_fwd(o, h1, a, wf, fs, tabs):
    cos, sin = tabs
    t = h1.shape[0]
    proj = mm(h1, wf['od_w_in'][o], name=f"od_in{o}")
    qr, kr, vv = qkv_rope_odd(proj, cos, sin, f"od_rope{o}")
    qh = qr.reshape(t, 2, 8, 64).transpose(1, 2, 0, 3)
    kh = kr.reshape(t, 2, 64).transpose(1, 0, 2)
    vh = vv.reshape(t, 2, 64).transpose(1, 0, 2)
    oh, lse = attn_fwd(qh, kh, vh, 127, 64 ** -0.5, f"od_attn{o}")
    sink_b = jnp.broadcast_to(a['od_sinks'][o].reshape(16, 1, 1), (16, 1, 64))
    oh, lse = oh.reshape(16, t, 64), lse.reshape(16, t, 64)
    attn_hm = sink_fwd(oh, lse, sink_b, f"od_sink{o}")
    attn_tm = attn_hm.transpose(1, 0, 2).reshape(t, A_W)
    prep_in = tuple(a[n][o] for n in ('od_a_re', 'od_a_im', 'od_b_re', 'od_b_im', 'od_c_re', 'od_c_im', 'od_log_dt'))
    (are, aim, bre, bim, cre, cim), prep_vjp = jax.vjp(s5_prep, *prep_in)
    s5w = (bre, bim, are, aim, cre, cim, fs['od_d'][o][None])
    y, sr, si = s5_fwd(proj, 10, *s5w, f"od_s5{o}")
    z = glu_z(y, f"od_glu_z{o}")
    gpre = mm(z, wf['od_glu_w'][o], name=f"od_glu_mm{o}")
    glu_b = fs['od_glu_b'][o][None]
    ssm = glu_out(y, gpre, glu_b, f"od_glu_out{o}")
    mix = jnp.concatenate([attn_tm, ssm], axis=1)
    yo = mm(mix, wf['od_w_out'][o], name=f"od_out{o}")
    return yo, dict(proj=proj, qh=qh, kh=kh, vh=vh, oh=oh, lse=lse, sink_b=sink_b, prep_vjp=prep_vjp, s5w=s5w, y=y, sr=sr, si=si,
                    z=z, gpre=gpre, glu_b=glu_b, mix=mix)


def _odd_bwd(o, dyg, s, h1, a, wf, fs, tabs, gs):
    cos, sin = tabs
    t = h1.shape[0]
    dmix = mm(dyg, wf['od_w_out'][o], tb=True, name=f"od_dmix{o}")
    dwo = mm(s['mix'], dyg, ta=True, out_dtype=BF16, name=f"od_dwo{o}")
    dgp, dglu_b = glu_bwd_gate(dmix, s['y'], s['gpre'], s['glu_b'], f"od_glu_bwd_gate{o}")
    dz_mm = mm(dgp, wf['od_glu_w'][o], tb=True, name=f"od_glu_dz{o}")
    dglu_w = mm(s['z'], dgp, ta=True, out_dtype=BF16, name=f"od_glu_dw{o}")
    dy = glu_bwd_y(dmix, s['y'], s['gpre'], s['glu_b'], dz_mm, f"od_glu_bwd_y{o}")
    du, dbre, dbim, dare, daim, dcre, dcim, dd = s5_bwd(s['proj'], 10, dy, s['sr'], s['si'], *s['s5w'], f"od_s5_bwd{o}")
    ga = s['prep_vjp']((dare, daim, dbre, dbim, dcre, dcim))
    for n, g in zip(('od_a_re', 'od_a_im', 'od_b_re', 'od_b_im', 'od_c_re', 'od_c_im', 'od_log_dt'), ga):
        gs[n][o] = g
    gs['od_d'][o] = dd[0]
    gs['od_glu_b'][o] = dglu_b[0]
    dattn_hm = dmix[:, :A_W].reshape(t, 16, 64).transpose(1, 0, 2)
    do, dvec, dsink = sink_bwd(dattn_hm, s['oh'], s['lse'], s['sink_b'], f"od_sink_bwd{o}")
    gs['od_sinks'][o] = dsink[:, 0, 0]
    dq, dk, dv = attn_bwd(s['qh'], s['kh'], s['vh'], do.reshape(2, 8, t, 64), s['lse'].reshape(2, 8, t, 64), dvec.reshape(2, 8, t, 64),
                          127, 64 ** -0.5, f"od_attn_bwd{o}")
    dq_tm = dq.transpose(2, 0, 1, 3).reshape(t, A_W)
    dk_tm = dk.transpose(1, 0, 2).reshape(t, LANE)
    dv_tm = dv.transpose(1, 0, 2).reshape(t, LANE)
    dproj = odd_dproj(dq_tm, dk_tm, dv_tm, du, cos, sin, f"od_dproj{o}")
    dh1 = mm(dproj, wf['od_w_in'][o], tb=True, name=f"od_dh{o}")
    dwi = mm(h1, dproj, ta=True, out_dtype=BF16, tm=2048, tn=768, name=f"od_dwi{o}")
    return dh1, dwi, dwo, dglu_w


def kernel(x, c, positions, ada_w, ada_b, norm_mix, norm_ffn, norm_final, ev_w_in, ev_conv_w, ev_conv_b, ev_gate_a_w, ev_gate_a_b, ev_gate_x_w, ev_gate_x_b, ev_lambda, ev_w_out, od_w_in, od_sinks, od_a_re, od_a_im, od_b_re, od_b_im, od_c_re, od_c_im, od_d, od_log_dt, od_glu_w, od_glu_b, od_w_out, ffn_w_in, ffn_conv_w, ffn_conv_b, ffn_w_out, loss_target, m_ada_w, m_ada_b, m_norm_mix, m_norm_ffn, m_norm_final, m_ev_w_in, m_ev_conv_w, m_ev_conv_b, m_ev_gate_a_w, m_ev_gate_a_b, m_ev_gate_x_w, m_ev_gate_x_b, m_ev_lambda, m_ev_w_out, m_od_w_in, m_od_sinks, m_od_a_re, m_od_a_im, m_od_b_re, m_od_b_im, m_od_c_re, m_od_c_im, m_od_d, m_od_log_dt, m_od_glu_w, m_od_glu_b, m_od_w_out, m_ffn_w_in, m_ffn_conv_w, m_ffn_conv_b, m_ffn_w_out, v_ada_w, v_ada_b, v_norm_mix, v_norm_ffn, v_norm_final, v_ev_w_in, v_ev_conv_w, v_ev_conv_b, v_ev_gate_a_w, v_ev_gate_a_b, v_ev_gate_x_w, v_ev_gate_x_b, v_ev_lambda, v_ev_w_out, v_od_w_in, v_od_sinks, v_od_a_re, v_od_a_im, v_od_b_re, v_od_b_im, v_od_c_re, v_od_c_im, v_od_d, v_od_log_dt, v_od_glu_w, v_od_glu_b, v_od_w_out, v_ffn_w_in, v_ffn_conv_w, v_ffn_conv_b, v_ffn_w_out):
    a = dict(locals())
    xi, yi, ci = _place()
    chip = 2 * xi + yi
    me = 2 * chip + ci
    x0, target, pos = x[0], loss_target[0], positions[0]
    d = D_MODEL

    g0 = all_gather8(_pack([c] + [a[n] for n in SMALL_SHARDED]), "gather_small").reshape(N_DEV, -1)
    c_all = g0[:, :d]
    fs, off = {}, d
    for n in SMALL_SHARDED:
        sh = a[n].shape
        parts = g0[0::2, off:off + math.prod(sh)].reshape((N_CHIP,) + sh)
        fs[n] = jnp.moveaxis(parts, 0, -2).reshape(sh[:-1] + (N_CHIP * sh[-1],))
        off += math.prod(sh)
    cond_all = silu_rows(c_all, "silu")

    modp = jnp.stack([mm(cond_all, ada_w[l], tm=8, tn=512, name=f"mod{l}") for l in range(DEPTH)])
    mod_all = all_gather8(modp.reshape(-1, LANE), "gather_mod").reshape(N_DEV, DEPTH, N_DEV, 6 * d // N_CHIP)[0::2]
    mod_me = lax.dynamic_index_in_dim(mod_all, me, axis=2, keepdims=False)
    mod = jnp.transpose(mod_me, (1, 0, 2)).reshape(DEPTH, 6 * d) + ada_b
    mods = [[mod[l, i * d:(i + 1) * d][None] for i in range(6)] for l in range(DEPTH)]

    full = gather_weights([a[n].astype(BF16) for n in BIG], "gather_weights")
    wf = {}
    for n, f in zip(BIG, full):
        _, nl, r, cc = f.shape
        if n in COL_SHARDED:
            wf[n] = jnp.transpose(f, (1, 2, 0, 3)).reshape(nl, r, N_CHIP * cc)
        else:
            wf[n] = jnp.transpose(f, (1, 0, 2, 3)).reshape(nl, N_CHIP * r, cc)

    ffn_cw, ffn_cb = fs['ffn_conv_w'], ffn_conv_b

    tabs128 = rope_tables(pos, 64)
    tabs64 = rope_tables(pos, 32)

    saved = []
    xcur = x0
    for l in range(DEPTH):
        sh1, sc1, g1, sh2, sc2, g2 = mods[l]
        s = dict(x=xcur)
        s['h1'] = norm_mod(xcur, norm_mix[l][None], sc1, sh1, f"norm_mix{l}")
        if l % 2 == 0:
            s['y'], s['mixer'] = _even_fwd(l // 2, s['h1'], a, wf, fs, tabs128)
        else:
            s['y'], s['mixer'] = _odd_fwd(l // 2, s['h1'], a, wf, fs, tabs64)
        s['x2'], s['h2'] = resid_norm_mod(xcur, s['y'], g1, norm_ffn[l][None], sc2, sh2, f"norm_ffn{l}")
        s['f'], s['ffn'] = _ffn_fwd(l, s['h2'], wf, ffn_cw[l], ffn_cb[l][None])
        xcur = resid_add(s['x2'], s['f'], g2, f"resid{l}")
        saved.append(s)

    dx, loss_part, dnf = final_loss(xcur, norm_final[None], target, "loss")
    loss = lax.psum(loss_part[0, 0], ("x", "y", "c"))

    gs = {n: {} for n in SMALL}
    gbig = {n: {} for n in BIG}
    dmod = {}
    gs['norm_final'][0] = dnf[0]
    for l in reversed(range(DEPTH)):
        sh1, sc1, g1, sh2, sc2, g2 = mods[l]
        s = saved[l]
        df, dg2 = resid_bwd(dx, s['f'], g2, f"resid_bwd_ffn{l}")
        dh2, dwi, dwo, dcw, dcb = _ffn_bwd(l, df, s['ffn'], s['h2'], wf, ffn_cw[l], ffn_cb[l][None])
        gbig['ffn_w_in'][l], gbig['ffn_w_out'][l], gs['ffn_conv_w'][l], gs['ffn_conv_b'][l] = dwi, dwo, dcw, dcb[0]
        dx2, dsh2, dsc2, dgam2 = norm_mod_bwd(dh2, s['x2'], dx, norm_ffn[l][None], sc2, f"norm_ffn_bwd{l}")
        gs['norm_ffn'][l] = dgam2[0]
        dyg, dg1 = resid_bwd(dx2, s['y'], g1, f"resid_bwd_mix{l}")
        if l % 2 == 0:
            dh1, dwi, dwo = _even_bwd(l // 2, dyg, s['mixer'], s['h1'], a, wf, fs, tabs128, gs)
            gbig['ev_w_in'][l // 2], gbig['ev_w_out'][l // 2] = dwi, dwo
        else:
            dh1, dwi, dwo, dglu_w = _odd_bwd(l // 2, dyg, s['mixer'], s['h1'], a, wf, fs, tabs64, gs)
            gbig['od_w_in'][l // 2], gbig['od_w_out'][l // 2], gbig['od_glu_w'][l // 2] = dwi, dwo, dglu_w
        dx, dsh1, dsc1, dgam1 = norm_mod_bwd(dh1, s['x'], dx2, norm_mix[l][None], sc1, f"norm_mix_bwd{l}")
        gs['norm_mix'][l] = dgam1[0]
        dmod[l] = jnp.concatenate([dsh1, dsc1, dg1, dsh2, dsc2, dg2], axis=1)[0]
    grad_x = dx[None]
    gs['ada_b'] = dmod

    grads = {}
    g5 = []
    for n in BIG:
        g = jnp.stack([gbig[n][i] for i in range(len(gbig[n]))])
        nl = g.shape[0]
        if n in COL_SHARDED:
            hr, ns = g.shape[1] // 2, g.shape[2] // N_CHIP
            g5.append(g.reshape(nl, 2, hr, N_CHIP, ns).transpose(1, 3, 0, 2, 4))
        else:
            hr = g.shape[1] // N_CHIP // 2
            g5.append(g.reshape(nl, N_CHIP, 2, hr, g.shape[2]).transpose(2, 1, 0, 3, 4))
    from_sibling = pair_swap(g5, "pair_swap_grads")
    pair = [pair_sum(g, r, ci, f"pair_sum_{n}") for n, g, r in zip(BIG, g5, from_sibling)]
    pieces = exchange_chips(pair, "exchange_grads")
    halves = [sum_devices(p, f"sum_{n}") for n, p in zip(BIG, pieces)]
    for n, g in zip(BIG, join_halves(halves, "join_grads")):
        grads[n] = g.reshape(a[n].shape)

    small_full = [jnp.stack([gs[n][i] for i in range(len(gs[n]))]) if n != 'norm_final' else gs[n][0] for n in SMALL]
    small_shapes = [g.shape for g in small_full]
    gs_all = all_gather8(_pack(small_full), "gather_small_grads")
    gs_sum = sum_devices(gs_all[:, None], "sum_small").reshape(-1)
    for n, g in zip(SMALL, _unpack(gs_sum, small_shapes)):
        if n in SMALL_SHARDED:
            w = a[n].shape[-1]
            g = lax.dynamic_slice_in_dim(g, chip * w, w, axis=g.ndim - 1)
        grads[n] = g
    assert SMALL[0] == 'ada_b'
    dmod_all = gs_all.reshape(N_DEV, -1)[:, :DEPTH * 6 * d].reshape(N_DEV, DEPTH, 6 * d)
    wcols = 6 * d // N_CHIP
    grads['ada_w'] = jnp.stack([
        mm(cond_all, lax.dynamic_slice_in_dim(dmod_all[:, l], chip * wcols, wcols, axis=1), ta=True, tm=2048, tn=512, name=f"ada_dw{l}")
        for l in range(DEPTH)])

    delta, new_m, new_v = {}, {}, {}
    for n in ['ada_w'] + BIG:
        sh = a[n].shape
        two_d = lambda t: t.reshape(-1, sh[-1])
        dl, nm, nv = adamw(two_d(a[n]), two_d(grads[n]), two_d(a['m_' + n]), two_d(a['v_' + n]), f"adamw_{n}")
        delta[n], new_m[n], new_v[n] = dl.reshape(sh), nm.reshape(sh), nv.reshape(sh)
    shapes = [a[n].shape for n in SMALL]
    dl, nm, nv = adamw(_pack([a[n] for n in SMALL]), _pack([grads[n] for n in SMALL]), _pack([a['m_' + n] for n in SMALL]),
                       _pack([a['v_' + n] for n in SMALL]), "adamw_small")
    for n, t1, t2, t3 in zip(SMALL, _unpack(dl.reshape(-1), shapes), _unpack(nm.reshape(-1), shapes), _unpack(nv.reshape(-1), shapes)):
        delta[n], new_m[n], new_v[n] = t1, t2, t3

    return (loss, grad_x, *[grads[n] for n in WEIGHTS], *[delta[n] for n in WEIGHTS], *[new_m[n] for n in WEIGHTS],
            *[new_v[n] for n in WEIGHTS])
```

```python
import functools
import math

import jax
import jax.numpy as jnp
from jax import lax
from jax.experimental import pallas as pl
from jax.experimental.pallas import tpu as pltpu

F32 = jnp.float32
BF16 = jnp.bfloat16
MESH = pl.DeviceIdType.MESH

D_MODEL = 2048
SEQ = 2048
DEPTH = 4
N_DEV = 8
N_CHIP = 4
BLK = 128
LANE = 128
V7X_VMEM_LIMIT = 56 * 1024 * 1024
NORM_EPS = 1e-6
ROPE_THETA = 10000.0
LRU_C = 8.0
D_FF = 5504
A_PATTERNS = ((128, 1), (512, 4), (2048, 16))
ADAM_LR, ADAM_B1, ADAM_B2, ADAM_EPS, ADAM_WD, ADAM_STEP = 0.001, 0.9, 0.999, 1e-08, 0.01, 10
NEG = -1e30


def _params(n_grid):
    return pltpu.CompilerParams(dimension_semantics=("arbitrary",) * n_grid, vmem_limit_bytes=V7X_VMEM_LIMIT)


def _pick(dim, pref):
    best = None
    for t in range(LANE, min(dim, pref) + 1, LANE):
        if dim % t == 0:
            best = t
    return best or dim


def _sigmoid(x):
    return 1.0 / (1.0 + jnp.exp(-x))


_GELU_C = math.sqrt(2.0 / math.pi)


def _gelu(x):
    t = jnp.tanh(_GELU_C * (x + 0.044715 * (x * x * x)))
    return 0.5 * x * (1.0 + t)


def _gelu_grad(x):
    t = jnp.tanh(_GELU_C * (x + 0.044715 * (x * x * x)))
    return 0.5 * (1.0 + t) + 0.5 * x * (1.0 - t * t) * (_GELU_C * (1.0 + 3.0 * 0.044715 * (x * x)))


def _softplus(x):
    return jnp.maximum(x, 0.0) + jnp.log(1.0 + jnp.exp(-jnp.abs(x)))


def _neg_expm1(x):
    series = -x * (1.0 + x * (0.5 + x * (1.0 / 6.0 + x * (1.0 / 24.0))))
    return jnp.where(x > -0.03, series, 1.0 - jnp.exp(x))


def _shift_down(x, k):
    if k == 0:
        return x
    row = lax.broadcasted_iota(jnp.int32, x.shape, 0)
    return jnp.where(row >= k, pltpu.roll(x, k, 0), 0.0)


def _shift_up(x, k):
    if k == 0:
        return x
    n = x.shape[0]
    row = lax.broadcasted_iota(jnp.int32, x.shape, 0)
    return jnp.where(row < n - k, pltpu.roll(x, n - k, 0), 0.0)


def _dot(a, b, ca=1, cb=0):
    return lax.dot_general(a.astype(BF16), b.astype(BF16), (((ca,), (cb,)), ((), ())), preferred_element_type=F32)


def _rope(x, cos, sin_signed, half):
    c = x.shape[1]
    reps = c // cos.shape[1]
    cos_c = jnp.tile(cos, (1, reps)) if reps > 1 else cos
    sin_c = jnp.tile(sin_signed, (1, reps)) if reps > 1 else sin_signed
    lane = lax.broadcasted_iota(jnp.int32, x.shape, 1)
    first = (lane % (2 * half)) < half
    partner = jnp.where(first, pltpu.roll(x, c - half, 1), pltpu.roll(x, half, 1))
    return x * cos_c + partner * sin_c


def _rope_t(dy, cos, sin_signed, half):
    c = dy.shape[1]
    reps = c // cos.shape[1]
    cos_c = jnp.tile(cos, (1, reps)) if reps > 1 else cos
    sin_c = jnp.tile(sin_signed, (1, reps)) if reps > 1 else sin_signed
    lane = lax.broadcasted_iota(jnp.int32, dy.shape, 1)
    first = (lane % (2 * half)) < half
    ys = dy * sin_c
    partner = jnp.where(first, pltpu.roll(ys, c - half, 1), pltpu.roll(ys, half, 1))
    return dy * cos_c + partner


def _tile_call(name, fn, grid, ins, outs, acc_axis=None):
    n_in = len(ins)
    accs = [o[4] for o in outs]

    def body(*refs):
        vals = fn(*[r[...] for r in refs[:n_in]])
        if not isinstance(vals, (tuple, list)):
            vals = (vals,)
        for r, v, acc in zip(refs[n_in:], vals, accs):
            if acc:
                first = pl.program_id(acc_axis) == 0

                @pl.when(first)
                def _():
                    r[...] = v.astype(r.dtype)

                @pl.when(jnp.logical_not(first))
                def _():
                    r[...] += v.astype(r.dtype)
            else:
                r[...] = v.astype(r.dtype)

    res = pl.pallas_call(
        body, name=name, grid=grid,
        in_specs=[pl.BlockSpec(b, im) for _, b, im in ins],
        out_specs=[pl.BlockSpec(o[2], o[3]) for o in outs],
        out_shape=[jax.ShapeDtypeStruct(o[0], o[1]) for o in outs],
        compiler_params=_params(len(grid)),
    )(*[a for a, _, _ in ins])
    return res


def _row_tile(cols, n_arrays, rows):
    budget = 24 * 1024 * 1024 // (2 * 4 * max(n_arrays, 1) * cols)
    t = 8
    while t * 2 <= budget and rows % (t * 2) == 0 and t * 2 <= 1024:
        t *= 2
    return t


def _rows_call(name, fn, rows, tiled, full, outs_tiled, outs_acc=()):
    cols = max([a.shape[1] for a in tiled] + [c for c, _ in outs_tiled])
    tt = _row_tile(cols, len(tiled) + len(outs_tiled), rows)
    ins = [(a, (tt, a.shape[1]), lambda i: (i, 0)) for a in tiled]
    ins += [(a, a.shape, (lambda nd: (lambda i: (0,) * nd))(a.ndim)) for a in full]
    outs = [((rows, c), dt, (tt, c), lambda i: (i, 0), False) for c, dt in outs_tiled]
    outs += [(s, dt, s, (lambda nd: (lambda i: (0,) * nd))(len(s)), True) for s, dt in outs_acc]
    return _tile_call(name, fn, (rows // tt,), ins, outs, acc_axis=0)


def mm(a, b, *, ta=False, tb=False, out_dtype=F32, tm=None, tn=None, tk=None, name):
    m, k = (a.shape[1], a.shape[0]) if ta else a.shape
    n = b.shape[0] if tb else b.shape[1]
    tm = tm or _pick(m, 1024)
    tn = tn or _pick(n, 1024)
    tk = tk or k
    assert m % tm == 0 and n % tn == 0 and k % tk == 0, (name, m, n, k, tm, tn, tk)
    nk = k // tk
    a_spec = pl.BlockSpec((tk, tm), lambda i, j, kk: (kk, i)) if ta else pl.BlockSpec((tm, tk), lambda i, j, kk: (i, kk))
    b_spec = pl.BlockSpec((tn, tk), lambda i, j, kk: (j, kk)) if tb else pl.BlockSpec((tk, tn), lambda i, j, kk: (kk, j))
    ca, cb = (0 if ta else 1), (1 if tb else 0)

    def body(a_ref, b_ref, o_ref, *scratch):
        p = _dot(a_ref[...], b_ref[...], ca, cb)
        if nk == 1:
            o_ref[...] = p.astype(o_ref.dtype)
        else:
            acc = scratch[0]
            kk = pl.program_id(2)

            @pl.when(kk == 0)
            def _():
                acc[...] = p

            @pl.when(kk > 0)
            def _():
                acc[...] += p

            @pl.when(kk == nk - 1)
            def _():
                o_ref[...] = acc[...].astype(o_ref.dtype)

    return pl.pallas_call(
        body, name=name, grid=(m // tm, n // tn, nk),
        in_specs=[a_spec, b_spec],
        out_specs=pl.BlockSpec((tm, tn), lambda i, j, kk: (i, j)),
        out_shape=jax.ShapeDtypeStruct((m, n), out_dtype),
        scratch_shapes=[pltpu.VMEM((tm, tn), F32)] if nk > 1 else [],
        compiler_params=_params(3),
    )(a, b)


def _rstd(x):
    return lax.rsqrt(jnp.mean(x * x, axis=-1, keepdims=True) + NORM_EPS)


def norm_mod(x, gamma, sc, sh, name):
    def fn(x, gamma, sc, sh):
        return (x * _rstd(x)) * gamma * (1.0 + sc) + sh
    return _rows_call(name, fn, x.shape[0], [x], [gamma, sc, sh], [(x.shape[1], BF16)])[0]


def resid_norm_mod(x, y, g, gamma, sc, sh, name):
    def fn(x, y, g, gamma, sc, sh):
        x2 = x + g * y
        return x2, (x2 * _rstd(x2)) * gamma * (1.0 + sc) + sh
    return _rows_call(name, fn, x.shape[0], [x, y], [g, gamma, sc, sh], [(x.shape[1], F32), (x.shape[1], BF16)])


def resid_add(x, y, g, name):
    def fn(x, y, g):
        return x + g * y
    return _rows_call(name, fn, x.shape[0], [x, y], [g], [(x.shape[1], F32)])[0]


def resid_bwd(dxo, f, g, name):
    def fn(dxo, f, g):
        return dxo * g, jnp.sum(dxo * f, axis=0, keepdims=True)
    d = dxo.shape[1]
    return _rows_call(name, fn, dxo.shape[0], [dxo, f], [g], [(d, BF16)], [((1, d), F32)])


def norm_mod_bwd(dh, x, dres, gamma, sc, name):
    def fn(dh, x, dres, gamma, sc):
        rstd = _rstd(x)
        xhat = x * rstd
        dxhat = dh * (gamma * (1.0 + sc))
        dx = rstd * (dxhat - xhat * jnp.mean(dxhat * xhat, axis=-1, keepdims=True))
        dhx = dh * xhat
        return (dres + dx, jnp.sum(dh, axis=0, keepdims=True), jnp.sum(dhx * gamma, axis=0, keepdims=True),
                jnp.sum(dhx * (1.0 + sc), axis=0, keepdims=True))
    d = x.shape[1]
    return _rows_call(name, fn, x.shape[0], [dh, x, dres], [gamma, sc], [(d, F32)], [((1, d), F32)] * 3)


def final_loss(x, gamma, target, name):
    d = x.shape[1]

    def fn(x, target, gamma):
        rstd = _rstd(x)
        xhat = x * rstd
        e = xhat * gamma - target
        part = 0.5 * jnp.sum(jnp.sum(e * e, axis=-1, keepdims=True) / d, axis=0, keepdims=True)
        dy = e / d
        dxhat = dy * gamma
        dx = rstd * (dxhat - xhat * jnp.mean(dxhat * xhat, axis=-1, keepdims=True))
        return dx, jnp.broadcast_to(part, (8, LANE)), jnp.sum(dy * xhat, axis=0, keepdims=True)
    return _rows_call(name, fn, x.shape[0], [x, target], [gamma], [(d, F32)], [((8, LANE), F32), ((1, d), F32)])


def _conv3(u, w, b):
    return b + w[2:3] * u + w[1:2] * _shift_down(u, 1) + w[0:1] * _shift_down(u, 2)


def _ffn_blocks(u, cw, cb):
    t = u.shape[0]
    nb = D_FF // LANE
    return [(u, (t, LANE), lambda j: (0, j)), (u, (t, LANE), lambda j: (0, j + nb)),
            (cw, (3, LANE), lambda j: (0, j)), (cw, (3, LANE), lambda j: (0, j + nb)),
            (cb, (1, LANE), lambda j: (0, j)), (cb, (1, LANE), lambda j: (0, j + nb))]


def ffn_act(u, cw, cb, name):
    t = u.shape[0]

    def fn(ug, uv, wg, wv, bg, bv):
        return _gelu(_conv3(ug, wg, bg)) * _conv3(uv, wv, bv)
    return _tile_call(name, fn, (D_FF // LANE,), _ffn_blocks(u, cw, cb), [((t, D_FF), BF16, (t, LANE), lambda j: (0, j), False)])[0]


def ffn_act_bwd(u, dact, cw, cb, name):
    t = u.shape[0]
    nb = D_FF // LANE

    def conv_t(duc, us, w):
        du = w[2:3] * duc + w[1:2] * _shift_up(duc, 1) + w[0:1] * _shift_up(duc, 2)
        dw = jnp.concatenate([jnp.sum(duc * _shift_down(us, 2), axis=0, keepdims=True),
                              jnp.sum(duc * _shift_down(us, 1), axis=0, keepdims=True),
                              jnp.sum(duc * us, axis=0, keepdims=True)], axis=0)
        return du, dw, jnp.sum(duc, axis=0, keepdims=True)

    def body(ug_ref, uv_ref, wg_ref, wv_ref, bg_ref, bv_ref, da_ref, du_ref, dw_ref, db_ref, du_scr, dw_scr, db_scr):
        s = pl.program_id(1)

        @pl.when(s == 0)
        def _():
            ug, uv, wg, wv, da = ug_ref[...], uv_ref[...], wg_ref[...], wv_ref[...], da_ref[...]
            g = _conv3(ug, wg, bg_ref[...])
            v = _conv3(uv, wv, bv_ref[...])
            du_g, dw_g, db_g = conv_t(da * v * _gelu_grad(g), ug, wg)
            du_v, dw_v, db_v = conv_t(da * _gelu(g), uv, wv)
            du_ref[...] = du_g.astype(du_ref.dtype)
            dw_ref[...] = dw_g
            db_ref[...] = db_g
            du_scr[...] = du_v.astype(du_scr.dtype)
            dw_scr[...] = dw_v
            db_scr[...] = db_v

        @pl.when(s == 1)
        def _():
            du_ref[...] = du_scr[...]
            dw_ref[...] = dw_scr[...]
            db_ref[...] = db_scr[...]

    def blk(rows, off):
        return pl.BlockSpec((rows, LANE), lambda j, s: (0, j + off))

    def out_blk(rows):
        return pl.BlockSpec((rows, LANE), lambda j, s: (0, j + nb * s))
    sds = jax.ShapeDtypeStruct
    return pl.pallas_call(
        body, name=name, grid=(nb, 2),
        in_specs=[blk(t, 0), blk(t, nb), blk(3, 0), blk(3, nb), blk(1, 0), blk(1, nb), blk(t, 0)],
        out_specs=[out_blk(t), out_blk(3), out_blk(1)],
        out_shape=[sds((t, 2 * D_FF), BF16), sds((3, 2 * D_FF), F32), sds((1, 2 * D_FF), F32)],
        scratch_shapes=[pltpu.VMEM((t, LANE), BF16), pltpu.VMEM((3, LANE), F32), pltpu.VMEM((1, LANE), F32)],
        compiler_params=_params(2),
    )(u, u, cw, cw, cb, cb, dact)


def attn_fwd(q, k, v, max_dist, scale, name):
    n, r, l, dh = q.shape
    nb = l // BLK
    m_rows = r * BLK

    def body(q_ref, kc_ref, kp_ref, vc_ref, vp_ref, o_ref, lse_ref):
        b = pl.program_id(1)
        qv = q_ref[...].reshape(m_rows, dh)
        s_c = _dot(qv, kc_ref[...], 1, 1) * scale
        s_p = _dot(qv, kp_ref[...], 1, 1) * scale
        qi = lax.broadcasted_iota(jnp.int32, (m_rows, BLK), 0) % BLK
        kj = lax.broadcasted_iota(jnp.int32, (m_rows, BLK), 1)
        s_c = jnp.where(kj <= qi, s_c, NEG)
        s_p = jnp.where((kj >= qi + (BLK - max_dist)) & (b > 0), s_p, NEG)
        mx = jnp.maximum(jnp.max(s_c, axis=1, keepdims=True), jnp.max(s_p, axis=1, keepdims=True))
        p_c = jnp.exp(s_c - mx)
        p_p = jnp.exp(s_p - mx)
        den = jnp.sum(p_c, axis=1, keepdims=True) + jnp.sum(p_p, axis=1, keepdims=True)
        o = (_dot(p_c, vc_ref[...]) + _dot(p_p, vp_ref[...])) / den
        o_ref[...] = o.reshape(r, BLK, dh)
        lse_ref[...] = jnp.broadcast_to(mx + jnp.log(den), (m_rows, dh)).reshape(r, BLK, dh)

    qspec = pl.BlockSpec((None, r, BLK, dh), lambda i, b: (i, 0, b, 0))
    cur = pl.BlockSpec((None, BLK, dh), lambda i, b: (i, b, 0))
    prev = pl.BlockSpec((None, BLK, dh), lambda i, b: (i, jnp.maximum(b - 1, 0), 0))
    return pl.pallas_call(
        body, name=name, grid=(n, nb),
        in_specs=[qspec, cur, prev, cur, prev],
        out_specs=[qspec, qspec],
        out_shape=[jax.ShapeDtypeStruct((n, r, l, dh), F32)] * 2,
        compiler_params=_params(2),
    )(q, k, k, v, v)


def attn_bwd(q, k, v, do, lse, dvec, max_dist, scale, name):
    n, r, l, dh = q.shape
    nb = l // BLK
    m_rows = r * BLK

    def body(qc_ref, qn_ref, kc_ref, kp_ref, vc_ref, vp_ref, doc_ref, don_ref, lc_ref, ln_ref, dc_ref, dn_ref,
             dq_ref, dk_ref, dv_ref):
        b = pl.program_id(1)
        qi = lax.broadcasted_iota(jnp.int32, (m_rows, BLK), 0) % BLK
        kj = lax.broadcasted_iota(jnp.int32, (m_rows, BLK), 1)
        m_cur = kj <= qi
        m_prev = kj >= qi + (BLK - max_dist)

        def pair(q_ref, do_ref, l_ref, d_ref, k_ref, v_ref, mask):
            qv = q_ref[...].reshape(m_rows, dh)
            dov = do_ref[...].reshape(m_rows, dh)
            lrow = l_ref[...].reshape(m_rows, dh)[:, 0:1]
            drow = d_ref[...].reshape(m_rows, dh)[:, 0:1]
            s = _dot(qv, k_ref[...], 1, 1) * scale
            p = jnp.where(mask, jnp.exp(jnp.where(mask, s, NEG) - lrow), 0.0)
            dp = _dot(dov, v_ref[...], 1, 1)
            ds = p * (dp - drow) * scale
            return qv, dov, p, ds

        q_a, do_a, p_a, ds_a = pair(qc_ref, doc_ref, lc_ref, dc_ref, kc_ref, vc_ref, m_cur)
        _, _, _, ds_b = pair(qc_ref, doc_ref, lc_ref, dc_ref, kp_ref, vp_ref, m_prev & (b > 0))
        q_c, do_c, p_c, ds_c = pair(qn_ref, don_ref, ln_ref, dn_ref, kc_ref, vc_ref, m_prev & (b < nb - 1))
        dq = _dot(ds_a, kc_ref[...]) + _dot(ds_b, kp_ref[...])
        dq_ref[...] = dq.reshape(r, BLK, dh)
        dk_ref[...] = _dot(ds_a, q_a, 0, 0) + _dot(ds_c, q_c, 0, 0)
        dv_ref[...] = _dot(p_a, do_a, 0, 0) + _dot(p_c, do_c, 0, 0)

    qcur = pl.BlockSpec((None, r, BLK, dh), lambda i, b: (i, 0, b, 0))
    qnext = pl.BlockSpec((None, r, BLK, dh), lambda i, b: (i, 0, jnp.minimum(b + 1, nb - 1), 0))
    cur = pl.BlockSpec((None, BLK, dh), lambda i, b: (i, b, 0))
    prev = pl.BlockSpec((None, BLK, dh), lambda i, b: (i, jnp.maximum(b - 1, 0), 0))
    return pl.pallas_call(
        body, name=name, grid=(n, nb),
        in_specs=[qcur, qnext, cur, prev, cur, prev, qcur, qnext, qcur, qnext, qcur, qnext],
        out_specs=[qcur, cur, cur],
        out_shape=[jax.ShapeDtypeStruct((n, r, l, dh), F32), jax.ShapeDtypeStruct((n, l, dh), F32),
                   jax.ShapeDtypeStruct((n, l, dh), F32)],
        compiler_params=_params(2),
    )(q, q, k, k, v, v, do, do, lse, lse, dvec, dvec)


def _scan_rows(t_len, step, init, reverse=False):
    n_chunks = t_len // 8

    def chunk(ci, carry):
        c = (n_chunks - 1 - ci) if reverse else ci
        base = pl.multiple_of(c * 8, 8)
        order = range(7, -1, -1) if reverse else range(8)
        return step(base, order, carry)
    return lax.fori_loop(0, n_chunks, chunk, init)


def _put_row(acc, i, row):
    rid = lax.broadcasted_iota(jnp.int32, acc.shape, 0)
    return jnp.where(rid == i, row, acc)


def _real_scan(a_ref, b_ref, h_ref, t_len, reverse=False):
    c = a_ref.shape[1]

    def step(base, order, h):
        a8 = a_ref[pl.ds(base, 8), :]
        b8 = b_ref[pl.ds(base, 8), :]
        out = jnp.zeros((8, c), F32)
        for i in order:
            h = a8[i:i + 1, :] * h + b8[i:i + 1, :]
            out = _put_row(out, i, h)
        h_ref[pl.ds(base, 8), :] = out
        return h
    _scan_rows(t_len, step, jnp.zeros((1, c), F32), reverse)


def _complex_scan(ar, ai, br_ref, bi_ref, sr_ref, si_ref, t_len, reverse=False):
    c = br_ref.shape[1]

    def step(base, order, carry):
        sr, si = carry
        br8 = br_ref[pl.ds(base, 8), :]
        bi8 = bi_ref[pl.ds(base, 8), :]
        outr = jnp.zeros((8, c), F32)
        outi = jnp.zeros((8, c), F32)
        for i in order:
            nr = ar * sr - ai * si + br8[i:i + 1, :]
            ni = ar * si + ai * sr + bi8[i:i + 1, :]
            sr, si = nr, ni
            outr = _put_row(outr, i, sr)
            outi = _put_row(outi, i, si)
        sr_ref[pl.ds(base, 8), :] = outr
        si_ref[pl.ds(base, 8), :] = outi
        return sr, si
    _scan_rows(t_len, step, (jnp.zeros((1, c), F32), jnp.zeros((1, c), F32)), reverse)


def _rglru_pre(xb, cw, cb, gaw, gab, gxw, gxb, lam):
    xc = cb + cw[3:4] * xb + cw[2:3] * _shift_down(xb, 1) + cw[1:2] * _shift_down(xb, 2) + cw[0:1] * _shift_down(xb, 3)
    r = _sigmoid(_dot(xc, gaw) + gab)
    ig = _sigmoid(_dot(xc, gxw) + gxb)
    sp = _softplus(-lam)
    log_a = -LRU_C * r * sp
    a = jnp.exp(log_a)
    mult = jnp.sqrt(_neg_expm1(2.0 * log_a))
    return xc, r, ig, sp, a, mult


def rglru_fwd(proj, cw, cb, gaw, gab, gxw, gxb, lam, xb_col, yb_col, name):
    t = proj.shape[0]
    nh = cw.shape[1] // LANE

    def body(xb_ref, yb_ref, cw_ref, cb_ref, gaw_ref, gab_ref, gxw_ref, gxb_ref, lam_ref, out_ref, h_ref, a_scr, b_scr):
        xc, r, ig, sp, a, mult = _rglru_pre(xb_ref[...], cw_ref[...], cb_ref[...], gaw_ref[...], gab_ref[...],
                                            gxw_ref[...], gxb_ref[...], lam_ref[...])
        a_scr[...] = a
        b_scr[...] = mult * (ig * xc)
        _real_scan(a_scr, b_scr, h_ref, t)
        out_ref[...] = (h_ref[...] * _gelu(yb_ref[...])).astype(out_ref.dtype)

    col = lambda off: pl.BlockSpec((t, LANE), lambda h: (0, off + h))
    vec = lambda rows: pl.BlockSpec((rows, LANE), lambda h: (0, h))
    wsp = pl.BlockSpec((None, LANE, LANE), lambda h: (h, 0, 0))
    return pl.pallas_call(
        body, name=name, grid=(nh,),
        in_specs=[col(xb_col), col(yb_col), vec(4), vec(1), wsp, vec(1), wsp, vec(1), vec(1)],
        out_specs=[col(0), col(0)],
        out_shape=[jax.ShapeDtypeStruct((t, nh * LANE), BF16), jax.ShapeDtypeStruct((t, nh * LANE), F32)],
        scratch_shapes=[pltpu.VMEM((t, LANE), F32)] * 2,
        compiler_params=_params(1),
    )(proj, proj, cw, cb, gaw, gab, gxw, gxb, lam)


def rglru_bwd(proj, hs, dlru, dlru_col, cw, cb, gaw, gab, gxw, gxb, lam, xb_col, yb_col, name):
    t = proj.shape[0]
    nh = cw.shape[1] // LANE

    def body(xb_ref, yb_ref, h_ref, dl_ref, cw_ref, cb_ref, gaw_ref, gab_ref, gxw_ref, gxb_ref, lam_ref,
             dxb_ref, dyb_ref, dcw_ref, dcb_ref, dgaw_ref, dgab_ref, dgxw_ref, dgxb_ref, dlam_ref, an_scr, dh_scr, gh_scr):
        xb = xb_ref[...]
        yb = yb_ref[...]
        cwv = cw_ref[...]
        lam = lam_ref[...]
        xc, r, ig, sp, a, mult = _rglru_pre(xb, cwv, cb_ref[...], gaw_ref[...], gab_ref[...], gxw_ref[...], gxb_ref[...], lam)
        h = h_ref[...]
        dl = dl_ref[...]
        dyb_ref[...] = (dl * h * _gelu_grad(yb)).astype(dyb_ref.dtype)
        dh_scr[...] = dl * _gelu(yb)
        an_scr[...] = _shift_up(a, 1)
        _real_scan(an_scr, dh_scr, gh_scr, t, reverse=True)
        gh = gh_scr[...]
        da = gh * _shift_down(h, 1)
        dmult = gh * ig * xc
        dig = gh * mult * xc
        dxc = gh * mult * ig
        dla = (da - dmult * a / mult) * a
        dr = dla * (-LRU_C * sp)
        dsp = jnp.sum(dla * (-LRU_C * r), axis=0, keepdims=True)
        dlam_ref[...] = dsp * (-_sigmoid(-lam))
        dpr = dr * r * (1.0 - r)
        dpi = dig * ig * (1.0 - ig)
        dgab_ref[...] = jnp.sum(dpr, axis=0, keepdims=True)
        dgxb_ref[...] = jnp.sum(dpi, axis=0, keepdims=True)
        dgaw_ref[...] = _dot(xc, dpr, 0, 0)
        dgxw_ref[...] = _dot(xc, dpi, 0, 0)
        dxc = dxc + _dot(dpr, gaw_ref[...], 1, 1) + _dot(dpi, gxw_ref[...], 1, 1)
        dxb = cwv[3:4] * dxc + cwv[2:3] * _shift_up(dxc, 1) + cwv[1:2] * _shift_up(dxc, 2) + cwv[0:1] * _shift_up(dxc, 3)
        dxb_ref[...] = dxb.astype(dxb_ref.dtype)
        dcw_ref[...] = jnp.concatenate([jnp.sum(dxc * _shift_down(xb, 3 - i), axis=0, keepdims=True) for i in range(4)], axis=0)
        dcb_ref[...] = jnp.sum(dxc, axis=0, keepdims=True)

    col = lambda off: pl.BlockSpec((t, LANE), lambda h: (0, off + h))
    vec = lambda rows: pl.BlockSpec((rows, LANE), lambda h: (0, h))
    wsp = pl.BlockSpec((None, LANE, LANE), lambda h: (h, 0, 0))
    w = nh * LANE
    sds = jax.ShapeDtypeStruct
    return pl.pallas_call(
        body, name=name, grid=(nh,),
        in_specs=[col(xb_col), col(yb_col), col(0), col(dlru_col), vec(4), vec(1), wsp, vec(1), wsp, vec(1), vec(1)],
        out_specs=[col(0), col(0), vec(4), vec(1), wsp, vec(1), wsp, vec(1), vec(1)],
        out_shape=[sds((t, w), BF16), sds((t, w), BF16), sds((4, w), F32), sds((1, w), F32), sds((nh, LANE, LANE), F32),
                   sds((1, w), F32), sds((nh, LANE, LANE), F32), sds((1, w), F32), sds((1, w), F32)],
        scratch_shapes=[pltpu.VMEM((t, LANE), F32)] * 3,
        compiler_params=_params(1),
    )(proj, proj, hs, dlru, cw, cb, gaw, gab, gxw, gxb, lam)


S5_CHUNKS = 8
S5_STATES = 512


def s5_fwd(proj, u_col, bre, bim, are, aim, cre, cim, dsk, name):
    t = proj.shape[0]

    def body(u_ref, bre_ref, bim_ref, are_ref, aim_ref, cre_ref, cim_ref, d_ref, y_ref, sr_ref, si_ref, br_scr, bi_scr):
        u = u_ref[...]
        br_scr[...] = _dot(u, bre_ref[...])
        bi_scr[...] = _dot(u, bim_ref[...])
        _complex_scan(are_ref[...], aim_ref[...], br_scr, bi_scr, sr_ref, si_ref, t)
        y_ref[...] = _dot(sr_ref[...], cre_ref[...]) - _dot(si_ref[...], cim_ref[...]) + d_ref[...] * u

    ucol = pl.BlockSpec((t, LANE), lambda c: (0, u_col + c))
    ycol = pl.BlockSpec((t, LANE), lambda c: (0, c))
    scol = pl.BlockSpec((t, S5_STATES), lambda c: (0, c))
    bsp = pl.BlockSpec((None, LANE, S5_STATES), lambda c: (c, 0, 0))
    csp = pl.BlockSpec((None, S5_STATES, LANE), lambda c: (c, 0, 0))
    asp = pl.BlockSpec((1, S5_STATES), lambda c: (0, c))
    dsp = pl.BlockSpec((1, LANE), lambda c: (0, c))
    sds = jax.ShapeDtypeStruct
    return pl.pallas_call(
        body, name=name, grid=(S5_CHUNKS,),
        in_specs=[ucol, bsp, bsp, asp, asp, csp, csp, dsp],
        out_specs=[ycol, scol, scol],
        out_shape=[sds((t, S5_CHUNKS * LANE), F32), sds((t, S5_CHUNKS * S5_STATES), F32), sds((t, S5_CHUNKS * S5_STATES), F32)],
        scratch_shapes=[pltpu.VMEM((t, S5_STATES), F32)] * 2,
        compiler_params=_params(1),
    )(proj, bre, bim, are, aim, cre, cim, dsk)


def s5_bwd(proj, u_col, dy, sr, si, bre, bim, are, aim, cre, cim, dsk, name):
    t = proj.shape[0]
    half = S5_STATES // 2

    def body(u_ref, dy_ref, sr_ref, si_ref, bre_ref, bim_ref, are_ref, aim_ref, cre_ref, cim_ref, d_ref,
             du_ref, dbre_ref, dbim_ref, dare_ref, daim_ref, dcre_ref, dcim_ref, dd_ref, dsr_scr, dsi_scr, gr_scr, gi_scr):
        hh = pl.program_id(1)
        u = u_ref[...]
        dy = dy_ref[...]
        dsr_scr[...] = _dot(dy, cre_ref[...], 1, 1)
        dsi_scr[...] = -_dot(dy, cim_ref[...], 1, 1)
        _complex_scan(are_ref[...], -aim_ref[...], dsr_scr, dsi_scr, gr_scr, gi_scr, t, reverse=True)
        gr = gr_scr[...]
        gi = gi_scr[...]
        spr = _shift_down(sr_ref[...], 1)
        spi = _shift_down(si_ref[...], 1)
        dare_ref[...] = jnp.sum(gr * spr + gi * spi, axis=0, keepdims=True)
        daim_ref[...] = jnp.sum(gi * spr - gr * spi, axis=0, keepdims=True)
        du = _dot(gr, bre_ref[...], 1, 1) + _dot(gi, bim_ref[...], 1, 1)

        @pl.when(hh == 0)
        def _():
            du_ref[...] = du + d_ref[...] * dy

        @pl.when(hh > 0)
        def _():
            du_ref[...] += du

        dbre_ref[...] = _dot(u, gr, 0, 0)
        dbim_ref[...] = _dot(u, gi, 0, 0)
        dcre_ref[...] = _dot(sr_ref[...], dy, 0, 0)
        dcim_ref[...] = -_dot(si_ref[...], dy, 0, 0)
        dd_ref[...] = jnp.sum(dy * u, axis=0, keepdims=True)

    ucol = pl.BlockSpec((t, LANE), lambda c, h: (0, u_col + c))
    ycol = pl.BlockSpec((t, LANE), lambda c, h: (0, c))
    scol = pl.BlockSpec((t, half), lambda c, h: (0, 2 * c + h))
    bsp = pl.BlockSpec((None, LANE, half), lambda c, h: (c, 0, h))
    csp = pl.BlockSpec((None, half, LANE), lambda c, h: (c, h, 0))
    asp = pl.BlockSpec((1, half), lambda c, h: (0, 2 * c + h))
    dsp = pl.BlockSpec((1, LANE), lambda c, h: (0, c))
    sds = jax.ShapeDtypeStruct
    return pl.pallas_call(
        body, name=name, grid=(S5_CHUNKS, 2),
        in_specs=[ucol, ycol, scol, scol, bsp, bsp, asp, asp, csp, csp, dsp],
        out_specs=[ycol, bsp, bsp, asp, asp, csp, csp, dsp],
        out_shape=[sds((t, S5_CHUNKS * LANE), F32), sds((S5_CHUNKS, LANE, S5_STATES), F32), sds((S5_CHUNKS, LANE, S5_STATES), F32),
                   sds((1, S5_CHUNKS * S5_STATES), F32), sds((1, S5_CHUNKS * S5_STATES), F32),
                   sds((S5_CHUNKS, S5_STATES, LANE), F32), sds((S5_CHUNKS, S5_STATES, LANE), F32), sds((1, S5_CHUNKS * LANE), F32)],
        scratch_shapes=[pltpu.VMEM((t, half), F32)] * 4,
        compiler_params=_params(2),
    )(proj, dy, sr, si, bre, bim, are, aim, cre, cim, dsk)


def s5_prep(a_re, a_im, b_re, b_im, c_re, c_im, log_dt):
    lam = lax.complex(a_re, a_im)
    dt = jnp.exp(log_dt)[:, None]
    a_bar = jnp.exp(lam * dt)
    b_bar = ((a_bar - 1.0) / lam)[..., None] * lax.complex(b_re, b_im)
    g, p, cg = b_re.shape
    eye = jnp.eye(8, dtype=F32)

    def in_map(m):
        m = m.reshape(g // 8, 8, p, cg)
        return jnp.einsum("ab,kapc->kacbp", eye, m).reshape(g // 8, 8 * cg, 8 * p)

    def out_map(m):
        m = m.reshape(g // 8, 8, cg, p)
        return jnp.einsum("ab,kacp->kapbc", eye, m).reshape(g // 8, 8 * p, 8 * cg)

    return (jnp.real(a_bar).reshape(1, g * p), jnp.imag(a_bar).reshape(1, g * p), in_map(jnp.real(b_bar)), in_map(jnp.imag(b_bar)),
            out_map(c_re), out_map(c_im))


def rope_tables(pos, half):
    inv = ROPE_THETA ** (-jnp.arange(half, dtype=F32) / half)
    ang = pos.astype(F32)[:, None] * inv
    cos, sin = jnp.cos(ang), jnp.sin(ang)
    reps = max(LANE // (2 * half), 1)
    return jnp.tile(jnp.concatenate([cos, cos], axis=1), (1, reps)), jnp.tile(jnp.concatenate([-sin, sin], axis=1), (1, reps))


A_W = 1024
ROW_T = 256


def _tiled(a, width, col):
    return (a, (ROW_T, width), lambda i: (i, col))


def _out_tiled(t, width, dtype):
    return ((t, width), dtype, (ROW_T, width), lambda i: (i, 0), False)


def qkv_rope_even(proj, cos, sin, name):
    t = proj.shape[0]

    def fn(q, k, v, cos, sin):
        return _rope(q, cos, sin, 64), _rope(k, cos, sin, 64), v
    ins = [_tiled(proj, A_W, 0), _tiled(proj, A_W, 1), _tiled(proj, A_W, 2), _tiled(cos, LANE, 0), _tiled(sin, LANE, 0)]
    return _tile_call(name, fn, (t // ROW_T,), ins, [_out_tiled(t, A_W, BF16)] * 3)


def merge3(o, lse, name):
    t = o[0].shape[0]

    def fn(o1, o2, o3, l1, l2, l3):
        mx = jnp.maximum(jnp.maximum(l1, l2), l3)
        e1, e2, e3 = jnp.exp(l1 - mx), jnp.exp(l2 - mx), jnp.exp(l3 - mx)
        den = e1 + e2 + e3
        out = (e1 * o1 + e2 * o2 + e3 * o3) / den
        return out, out, mx + jnp.log(den)
    ins = [_tiled(a, A_W, 0) for a in list(o) + list(lse)]
    return _tile_call(name, fn, (t // ROW_T,), ins, [_out_tiled(t, A_W, BF16), _out_tiled(t, A_W, F32), _out_tiled(t, A_W, F32)])


def _segsum_bcast(x, width):
    parts = []
    for h in range(x.shape[1] // width):
        s = jnp.sum(x[:, h * width:(h + 1) * width], axis=1, keepdims=True)
        parts.append(jnp.broadcast_to(s, (x.shape[0], width)))
    return jnp.concatenate(parts, axis=1)


def even_attn_prep(dmix, attn, name):
    t = attn.shape[0]

    def fn(dout, attn):
        return dout, _segsum_bcast(dout * attn, LANE)
    ins = [_tiled(dmix, A_W, 0), _tiled(attn, A_W, 0)]
    return _tile_call(name, fn, (t // ROW_T,), ins, [_out_tiled(t, A_W, BF16), _out_tiled(t, A_W, F32)])


def even_dproj(dq, dk, dv, dxb, dyb, cos, sin, name):
    t = dxb.shape[0]

    def fn(q1, q2, q3, k1, k2, k3, v1, v2, v3, dxb, dyb, cos, sin):
        return jnp.concatenate([_rope_t(q1 + q2 + q3, cos, sin, 64).astype(BF16), _rope_t(k1 + k2 + k3, cos, sin, 64).astype(BF16),
                                (v1 + v2 + v3).astype(BF16), dxb, dyb], axis=1)
    ins = [_tiled(a, A_W, 0) for a in list(dq) + list(dk) + list(dv) + [dxb, dyb]] + [_tiled(cos, LANE, 0), _tiled(sin, LANE, 0)]
    return _tile_call(name, fn, (t // ROW_T,), ins, [_out_tiled(t, 5 * A_W, BF16)])[0]


def perm(x, d):
    t = x.shape[0]
    return x.reshape(t // d, d, 8, LANE).transpose(1, 2, 0, 3).reshape(d * 8, t // d, LANE)


def unperm(xp, d):
    n, l, _ = xp.shape
    return xp.reshape(d, 8, l, LANE).transpose(2, 0, 1, 3).reshape(l * d, 8 * LANE)


def qkv_rope_odd(proj, cos, sin, name):
    t = proj.shape[0]

    def fn(q, k, v, cos, sin):
        return _rope(q, cos, sin, 32), _rope(k, cos, sin, 32), v
    ins = [_tiled(proj, A_W, 0), _tiled(proj, LANE, 8), _tiled(proj, LANE, 9), _tiled(cos, LANE, 0), _tiled(sin, LANE, 0)]
    return _tile_call(name, fn, (t // ROW_T,), ins, [_out_tiled(t, A_W, BF16), _out_tiled(t, LANE, BF16), _out_tiled(t, LANE, BF16)])


HEAD_ROWS = 1024


def _head_blocks(a):
    return (a, (None, HEAD_ROWS, a.shape[2]), lambda h, i: (h, i, 0))


def sink_fwd(o, lse, sink_b, name):
    nh, t, dh = o.shape

    def fn(o, lse, s):
        return o * _sigmoid(lse - s)
    ins = [_head_blocks(o), _head_blocks(lse), (sink_b, (None, 1, dh), lambda h, i: (h, 0, 0))]
    return _tile_call(name, fn, (nh, t // HEAD_ROWS), ins, [((nh, t, dh), BF16, (None, HEAD_ROWS, dh), lambda h, i: (h, i, 0), False)])[0]


def sink_bwd(dof, o, lse, sink_b, name):
    nh, t, dh = o.shape

    def fn(dof, o, lse, s):
        keep = _sigmoid(lse - s)
        dk = jnp.sum(dof * o, axis=1, keepdims=True)
        dlse = dk * keep * (1.0 - keep)
        return dof * keep, dk * keep * keep, -jnp.sum(dlse, axis=0, keepdims=True)
    ins = [_head_blocks(dof), _head_blocks(o), _head_blocks(lse), (sink_b, (None, 1, dh), lambda h, i: (h, 0, 0))]
    outs = [((nh, t, dh), BF16, (None, HEAD_ROWS, dh), lambda h, i: (h, i, 0), False),
            ((nh, t, dh), F32, (None, HEAD_ROWS, dh), lambda h, i: (h, i, 0), False),
            ((nh, 1, dh), F32, (None, 1, dh), lambda h, i: (h, 0, 0), True)]
    return _tile_call(name, fn, (nh, t // HEAD_ROWS), ins, outs, acc_axis=1)


def odd_dproj(dq, dk, dv, du, cos, sin, name):
    t = dq.shape[0]

    def fn(dq, dk, dv, du, cos, sin):
        return jnp.concatenate([_rope_t(dq, cos, sin, 32), _rope_t(dk, cos, sin, 32), dv, du], axis=1)
    ins = [_tiled(dq, A_W, 0), _tiled(dk, LANE, 0), _tiled(dv, LANE, 0), _tiled(du, A_W, 0), _tiled(cos, LANE, 0), _tiled(sin, LANE, 0)]
    return _tile_call(name, fn, (t // ROW_T,), ins, [_out_tiled(t, 2 * A_W + 2 * LANE, BF16)])[0]


def glu_z(y, name):
    return _rows_call(name, _gelu, y.shape[0], [y], [], [(y.shape[1], BF16)])[0]


def glu_out(y, gpre, b, name):
    def fn(y, gpre, b):
        return _gelu(y) * _sigmoid(gpre + b)
    return _rows_call(name, fn, y.shape[0], [y, gpre], [b], [(y.shape[1], BF16)])[0]


def glu_bwd_gate(dmix, y, gpre, b, name):
    t = y.shape[0]

    def fn(dout, y, gpre, b):
        gate = _sigmoid(gpre + b)
        dgp = dout * _gelu(y) * gate * (1.0 - gate)
        return dgp, jnp.sum(dgp, axis=0, keepdims=True)
    ins = [_tiled(dmix, A_W, 1), _tiled(y, A_W, 0), _tiled(gpre, A_W, 0), (b, (1, A_W), lambda i: (0, 0))]
    outs = [_out_tiled(t, A_W, BF16), ((1, A_W), F32, (1, A_W), lambda i: (0, 0), True)]
    return _tile_call(name, fn, (t // ROW_T,), ins, outs, acc_axis=0)


def glu_bwd_y(dmix, y, gpre, b, dz_mm, name):
    t = y.shape[0]

    def fn(dout, y, gpre, b, dz_mm):
        return (dout * _sigmoid(gpre + b) + dz_mm) * _gelu_grad(y)
    ins = [_tiled(dmix, A_W, 1), _tiled(y, A_W, 0), _tiled(gpre, A_W, 0), (b, (1, A_W), lambda i: (0, 0)), _tiled(dz_mm, A_W, 0)]
    return _tile_call(name, fn, (t // ROW_T,), ins, [_out_tiled(t, A_W, F32)])[0]


def adamw(w, g, m, v, name):
    def fn(w, g, m, v):
        m = ADAM_B1 * m + (1.0 - ADAM_B1) * g
        v = ADAM_B2 * v + (1.0 - ADAM_B2) * (g * g)
        m_hat = m / (1.0 - ADAM_B1 ** ADAM_STEP)
        v_hat = v / (1.0 - ADAM_B2 ** ADAM_STEP)
        return -ADAM_LR * (m_hat / (jnp.sqrt(v_hat) + ADAM_EPS) + ADAM_WD * w), m, v
    c = w.shape[1]
    return _rows_call(name, fn, w.shape[0], [w, g, m, v], [], [(c, F32)] * 3)


def _row_block(rows, row_bytes, limit):
    best = 16
    for t in range(16, rows + 1, 16):
        if rows % t == 0 and t * row_bytes <= limit:
            best = t
    return best


def _sum_in_order(v):
    s = v[0].astype(F32)
    for d in range(1, v.shape[0]):
        s = s + v[d].astype(F32)
    return s


def sum_devices(parts, name):
    nd, nl, r, c = parts.shape
    tr = _row_block(r, c * (parts.dtype.itemsize * nd + 4), 18 * 1024 * 1024)
    ins = [(parts, (nd, None, tr, c), lambda l, i: (0, l, i, 0))]
    outs = [((nl, r, c), F32, (None, tr, c), lambda l, i: (l, i, 0), False)]
    return _tile_call(name, _sum_in_order, (nl, r // tr), ins, outs)[0]


def silu_rows(c_all, name):
    def fn(c):
        return c * _sigmoid(c)
    return _rows_call(name, fn, c_all.shape[0], [c_all], [], [(c_all.shape[1], F32)])[0]


def _place():
    x, y, c = lax.axis_index("x"), lax.axis_index("y"), lax.axis_index("c")
    return x, y, c


ANY = pl.BlockSpec(memory_space=pl.ANY)


def all_gather8(v, name):
    r, cdim = v.shape

    def body(x_ref, out_ref, send_sems, recv_sems, local_sem):
        x, y, c = _place()
        me, sibling = (x, y, c), (x, y, 1 - c)
        chips = [(1 - x, y), (x, 1 - y), (1 - x, 1 - y)]

        def rows(px, py, pc):
            return out_ref.at[4 * px + 2 * py + pc]

        def copy(k, block, to, src=None):
            return pltpu.make_async_remote_copy(
                src_ref=rows(*block) if src is None else src, dst_ref=rows(*block),
                send_sem=send_sems.at[k], recv_sem=recv_sems.at[k], device_id=to, device_id_type=MESH)

        mine = pltpu.make_async_copy(x_ref, rows(*me), local_sem)
        mine.start()
        first = [copy(0, me, sibling, src=x_ref)]
        first += [copy(1 + j, me, (*chip, c), src=x_ref) for j, chip in enumerate(chips)]
        for cp in first:
            cp.start()
        passed = [copy(4 + j, (*chip, c), sibling) for j, chip in enumerate(chips)]
        for j, chip in enumerate(chips):
            copy(1 + j, (*chip, c), me).wait_recv()
            passed[j].start()
        copy(0, sibling, me).wait_recv()
        for j, chip in enumerate(chips):
            copy(4 + j, (*chip, 1 - c), me).wait_recv()
        for cp in first + passed:
            cp.wait_send()
        mine.wait()

    return pl.pallas_call(
        body, name=name, out_shape=jax.ShapeDtypeStruct((N_DEV, r, cdim), v.dtype),
        in_specs=[ANY], out_specs=ANY,
        scratch_shapes=[pltpu.SemaphoreType.DMA((7,)), pltpu.SemaphoreType.DMA((7,)), pltpu.SemaphoreType.DMA],
    )(v)


def gather_weights(shards, name):
    n = len(shards)

    def body(*refs):
        ins, outs = refs[:n], refs[n:2 * n]
        send_sems, recv_sems = refs[2 * n:]
        x, y, c = _place()
        sibling = (x, y, 1 - c)
        chips = [(1 - x, y), (x, 1 - y), (1 - x, 1 - y)]
        my_chip = 2 * x + y

        def half(t, chip_slot, start):
            hr = ins[t].shape[1] // 2
            return outs[t].at[chip_slot, :, pl.ds(start, hr), :]

        def copy(t, k, src, dst, to):
            return pltpu.make_async_remote_copy(src_ref=src, dst_ref=dst, send_sem=send_sems.at[6 * t + k],
                                                recv_sem=recv_sems.at[6 * t + k], device_id=to, device_id_type=MESH)

        def lows(t):
            hr = ins[t].shape[1] // 2
            return hr, pl.multiple_of(c * hr, 16), pl.multiple_of((1 - c) * hr, 16)

        started = []
        for t in range(n):
            hr, lo, _ = lows(t)
            for j, chip in enumerate(chips):
                cp = copy(t, j, ins[t].at[:, pl.ds(lo, hr), :], half(t, my_chip, lo), (*chip, c))
                cp.start()
                started.append(cp)
        for t in range(n):
            hr, lo, _ = lows(t)
            for j, (px, py) in enumerate(chips):
                slot = 2 * px + py
                copy(t, j, half(t, slot, lo), half(t, slot, lo), (px, py, c)).wait_recv()
                fwd = copy(t, 3 + j, half(t, slot, lo), half(t, slot, lo), sibling)
                fwd.start()
                started.append(fwd)
        for t in range(n):
            hr, _, lo_sib = lows(t)
            for j, (px, py) in enumerate(chips):
                slot = 2 * px + py
                copy(t, 3 + j, half(t, slot, lo_sib), half(t, slot, lo_sib), sibling).wait_recv()
        for cp in started:
            cp.wait_send()

    got = pl.pallas_call(
        body, name=name,
        out_shape=[jax.ShapeDtypeStruct((N_CHIP,) + s.shape, s.dtype) for s in shards],
        in_specs=[ANY] * n, out_specs=[ANY] * n,
        scratch_shapes=[pltpu.SemaphoreType.DMA((6 * n,)), pltpu.SemaphoreType.DMA((6 * n,))],
    )(*shards)
    my_chip = 2 * lax.axis_index("x") + lax.axis_index("y")
    return [lax.dynamic_update_index_in_dim(g, s, my_chip, 0) for g, s in zip(got, shards)]


def pair_swap(grads, name):
    n = len(grads)

    def body(*refs):
        ins, outs = refs[:n], refs[n:2 * n]
        send_sems, recv_sems = refs[2 * n:]
        x, y, c = _place()
        sibling = (x, y, 1 - c)
        sends = []
        for t in range(n):
            for q in range(N_CHIP):
                cp = pltpu.make_async_remote_copy(src_ref=ins[t].at[1 - c, q], dst_ref=outs[t].at[q], send_sem=send_sems.at[N_CHIP * t + q],
                                                  recv_sem=recv_sems.at[N_CHIP * t + q], device_id=sibling, device_id_type=MESH)
                cp.start()
                sends.append(cp)
        for cp in sends:
            cp.wait_recv()
            cp.wait_send()

    return pl.pallas_call(
        body, name=name,
        out_shape=[jax.ShapeDtypeStruct(g.shape[1:], g.dtype) for g in grads],
        in_specs=[ANY] * n, out_specs=[ANY] * n,
        scratch_shapes=[pltpu.SemaphoreType.DMA((N_CHIP * n,)), pltpu.SemaphoreType.DMA((N_CHIP * n,))],
    )(*grads)


def pair_sum(g5, from_sibling, core, name):
    _, nq, nl, hr, c = g5.shape
    tr = _row_block(hr, c * 2, 3 * 1024 * 1024)
    flag =jnp.broadcast_to(core.astype(F32), (8, LANE))

    def fn(g0, g1, r, flag):
        own = jnp.where(flag[0:1, 0:1] == 0.0, g0.astype(F32), g1.astype(F32))
        return own + r.astype(F32)
    ins = [(g5, (None, None, None, tr, c), lambda q, l, i: (0, q, l, i, 0)), (g5, (None, None, None, tr, c), lambda q, l, i: (1, q, l, i, 0)),
           (from_sibling, (None, None, tr, c), lambda q, l, i: (q, l, i, 0)), (flag, (8, LANE), lambda q, l, i: (0, 0))]
    outs = [((nq, nl, hr, c), BF16, (None, None, tr, c), lambda q, l, i: (q, l, i, 0), False)]
    return _tile_call(name, fn, (nq, nl, hr // tr), ins, outs)[0]


def exchange_chips(pairs, name):
    n = len(pairs)

    def body(*refs):
        ins, outs = refs[:n], refs[n:2 * n]
        send_sems, recv_sems = refs[2 * n:]
        x, y, c = _place()
        chips = [(1 - x, y), (x, 1 - y), (1 - x, 1 - y)]
        my_chip = 2 * x + y
        sends = []
        for t in range(n):
            for j, (px, py) in enumerate(chips):
                cp = pltpu.make_async_remote_copy(src_ref=ins[t].at[2 * px + py], dst_ref=outs[t].at[my_chip], send_sem=send_sems.at[3 * t + j],
                                                  recv_sem=recv_sems.at[3 * t + j], device_id=(px, py, c), device_id_type=MESH)
                cp.start()
                sends.append(cp)
        for t in range(n):
            for j, (px, py) in enumerate(chips):
                slot = outs[t].at[2 * px + py]
                pltpu.make_async_remote_copy(src_ref=slot, dst_ref=slot, send_sem=send_sems.at[3 * t + j], recv_sem=recv_sems.at[3 * t + j],
                                             device_id=(px, py, c), device_id_type=MESH).wait_recv()
        for cp in sends:
            cp.wait_send()

    got = pl.pallas_call(
        body, name=name,
        out_shape=[jax.ShapeDtypeStruct(p.shape, p.dtype) for p in pairs],
        in_specs=[ANY] * n, out_specs=[ANY] * n,
        scratch_shapes=[pltpu.SemaphoreType.DMA((3 * n,)), pltpu.SemaphoreType.DMA((3 * n,))],
    )(*pairs)
    my_chip = 2 * lax.axis_index("x") + lax.axis_index("y")
    return [lax.dynamic_update_index_in_dim(o, lax.dynamic_index_in_dim(p, my_chip, 0, keepdims=False), my_chip, 0) for o, p in zip(got, pairs)]


def join_halves(halves, name):
    n = len(halves)
    chunks = [(t, l, j) for t in range(n) for l in range(halves[t].shape[0]) for j in range(2)]

    def body(*refs):
        ins, outs = refs[:n], refs[n:2 * n]
        send_sems, recv_sems = refs[2 * n:]
        x, y, c = _place()
        sibling = (x, y, 1 - c)
        pending = []
        for k, (t, l, j) in enumerate(chunks):
            h_ref, o_ref = ins[t], outs[t]
            hr = h_ref.shape[1]
            rows = hr // 2
            lo = pl.multiple_of(c * hr + j * rows, 8)
            lo_sib = pl.multiple_of((1 - c) * hr + j * rows, 8)
            src = h_ref.at[l, pl.ds(j * rows, rows), :]
            cp = pltpu.make_async_remote_copy(src_ref=src, dst_ref=o_ref.at[l, pl.ds(lo, rows), :], send_sem=send_sems.at[k],
                                              recv_sem=recv_sems.at[k], device_id=sibling, device_id_type=MESH)
            cp.start()
            got = pltpu.make_async_remote_copy(src_ref=src, dst_ref=o_ref.at[l, pl.ds(lo_sib, rows), :], send_sem=send_sems.at[k],
                                               recv_sem=recv_sems.at[k], device_id=sibling, device_id_type=MESH)
            pending.append((cp, got))
        for cp, got in pending:
            got.wait_recv()
            cp.wait_send()

    got = pl.pallas_call(
        body, name=name,
        out_shape=[jax.ShapeDtypeStruct((h.shape[0], 2 * h.shape[1], h.shape[2]), h.dtype) for h in halves],
        in_specs=[ANY] * n, out_specs=[ANY] * n,
        scratch_shapes=[pltpu.SemaphoreType.DMA((len(chunks),)), pltpu.SemaphoreType.DMA((len(chunks),))],
    )(*halves)
    ci = lax.axis_index("c")
    return [lax.dynamic_update_slice(g, h, (0, ci * h.shape[1], 0)) for g, h in zip(got, halves)]


WEIGHTS = ['ada_w', 'ada_b', 'norm_mix', 'norm_ffn', 'norm_final', 'ev_w_in', 'ev_conv_w', 'ev_conv_b', 'ev_gate_a_w', 'ev_gate_a_b',
           'ev_gate_x_w', 'ev_gate_x_b', 'ev_lambda', 'ev_w_out', 'od_w_in', 'od_sinks', 'od_a_re', 'od_a_im', 'od_b_re', 'od_b_im',
           'od_c_re', 'od_c_im', 'od_d', 'od_log_dt', 'od_glu_w', 'od_glu_b', 'od_w_out', 'ffn_w_in', 'ffn_conv_w', 'ffn_conv_b', 'ffn_w_out']
BIG = ['ev_w_in', 'ev_w_out', 'od_w_in', 'od_glu_w', 'od_w_out', 'ffn_w_in', 'ffn_w_out']
COL_SHARDED = ('ev_w_in', 'od_w_in', 'ffn_w_in')
SMALL_SHARDED = ['ev_conv_w', 'od_d', 'od_glu_b', 'ffn_conv_w']
SMALL = [n for n in WEIGHTS if n not in BIG and n != 'ada_w']


def _pack(arrs):
    flat = jnp.concatenate([a.reshape(-1).astype(F32) for a in arrs])
    rows = -(-flat.shape[0] // (1024 * LANE)) * 1024
    return jnp.pad(flat, (0, rows * LANE - flat.shape[0])).reshape(rows, LANE)


def _unpack(flat, shapes):
    out, off = [], 0
    for s in shapes:
        n = math.prod(s)
        out.append(flat[..., off:off + n].reshape(flat.shape[:-1] + tuple(s)))
        off += n
    return out


def _ffn_fwd(l, h2, wf, cw, cb):
    u = mm(h2, wf['ffn_w_in'][l], tm=2048, tn=256, name=f"ffn_in{l}")
    act = ffn_act(u, cw, cb, f"ffn_act{l}")
    f = mm(act, wf['ffn_w_out'][l], tm=1024, tn=512, name=f"ffn_out{l}")
    return f, dict(u=u, act=act)


def _ffn_bwd(l, df, s, h2, wf, cw, cb):
    dact = mm(df, wf['ffn_w_out'][l], tb=True, tm=2048, tn=128, name=f"ffn_dact{l}")
    dwo = mm(s['act'], df, ta=True, out_dtype=BF16, tm=D_FF, tn=256, tk=512, name=f"ffn_dwo{l}")
    du, dcw, dcb = ffn_act_bwd(s['u'], dact, cw, cb, f"ffn_act_bwd{l}")
    dh2 = mm(du, wf['ffn_w_in'][l], tb=True, tm=1024, tn=512, tk=D_FF, name=f"ffn_dh{l}")
    dwi = mm(h2, du, ta=True, out_dtype=BF16, tm=2048, tn=256, name=f"ffn_dwi{l}")
    return dh2, dwi, dwo, dcw, dcb


def _even_fwd(e, h1, a, wf, fs, tabs):
    cos, sin = tabs
    proj = mm(h1, wf['ev_w_in'][e], name=f"ev_in{e}")
    q, k, v = qkv_rope_even(proj, cos, sin, f"ev_rope{e}")
    outs, lses = [], []
    for window, d in A_PATTERNS:
        o, lse = attn_fwd(perm(q, d)[:, None], perm(k, d), perm(v, d), window // d, LANE ** -0.5, f"ev_attn{e}_{d}")
        outs.append(unperm(o[:, 0], d))
        lses.append(unperm(lse[:, 0], d))
    attn_bf, attn, lse_tot = merge3(outs, lses, f"ev_merge{e}")
    lru, hs = rglru_fwd(proj, fs['ev_conv_w'][e], a['ev_conv_b'][e][None], a['ev_gate_a_w'][e], a['ev_gate_a_b'][e][None],
                        a['ev_gate_x_w'][e], a['ev_gate_x_b'][e][None], a['ev_lambda'][e][None], 24, 32, f"ev_lru{e}")
    mix = jnp.concatenate([attn_bf, lru], axis=1)
    y = mm(mix, wf['ev_w_out'][e], name=f"ev_out{e}")
    return y, dict(proj=proj, q=q, k=k, v=v, attn=attn, lse=lse_tot, hs=hs, mix=mix)


def _even_bwd(e, dyg, s, h1, a, wf, fs, tabs, gs):
    cos, sin = tabs
    dmix = mm(dyg, wf['ev_w_out'][e], tb=True, name=f"ev_dmix{e}")
    dwo = mm(s['mix'], dyg, ta=True, out_dtype=BF16, name=f"ev_dwo{e}")
    do_bf, dvec = even_attn_prep(dmix, s['attn'], f"ev_prep{e}")
    dqs, dks, dvs = [], [], []
    for window, d in A_PATTERNS:
        dq, dk, dv = attn_bwd(perm(s['q'], d)[:, None], perm(s['k'], d), perm(s['v'], d), perm(do_bf, d)[:, None],
                              perm(s['lse'], d)[:, None], perm(dvec, d)[:, None], window // d, LANE ** -0.5, f"ev_attn_bwd{e}_{d}")
        dqs.append(unperm(dq[:, 0], d))
        dks.append(unperm(dk, d))
        dvs.append(unperm(dv, d))
    dxb, dyb, dcw, dcb, dgaw, dgab, dgxw, dgxb, dlam = rglru_bwd(
        s['proj'], s['hs'], dmix, 8, fs['ev_conv_w'][e], a['ev_conv_b'][e][None], a['ev_gate_a_w'][e], a['ev_gate_a_b'][e][None],
        a['ev_gate_x_w'][e], a['ev_gate_x_b'][e][None], a['ev_lambda'][e][None], 24, 32, f"ev_lru_bwd{e}")
    for n, g in (('ev_conv_w', dcw), ('ev_conv_b', dcb[0]), ('ev_gate_a_w', dgaw), ('ev_gate_a_b', dgab[0]), ('ev_gate_x_w', dgxw),
                 ('ev_gate_x_b', dgxb[0]), ('ev_lambda', dlam[0])):
        gs[n][e] = g
    dproj = even_dproj(dqs, dks, dvs, dxb, dyb, cos, sin, f"ev_dproj{e}")
    dh1 = mm(dproj, wf['ev_w_in'][e], tb=True, tk=2560, name=f"ev_dh{e}")
    dwi = mm(h1, dproj, ta=True, out_dtype=BF16, tm=2048, tn=512, name=f"ev_dwi{e}")
    return dh1, dwi, dwo


def _odd_fwd(o, h1, a, wf, fs, tabs):
    cos, sin = tabs
    t = h1.shape[0]
    proj = mm(h1, wf['od_w_in'][o], name=f"od_in{o}")
    qr, kr, vv = qkv_rope_odd(proj, cos, sin, f"od_rope{o}")
    qh = qr.reshape(t, 2, 8, 64).transpose(1, 2, 0, 3)
    kh = kr.reshape(t, 2, 64).transpose(1, 0, 2)
    vh = vv.reshape(t, 2, 64).transpose(1, 0, 2)
    oh, lse = attn_fwd(qh, kh, vh, 127, 64 ** -0.5, f"od_attn{o}")
    sink_b = jnp.broadcast_to(a['od_sinks'][o].reshape(16, 1, 1), (16, 1, 64))
    oh, lse = oh.reshape(16, t, 64), lse.reshape(16, t, 64)
    attn_hm = sink_fwd(oh, lse, sink_b, f"od_sink{o}")
    attn_tm = attn_hm.transpose(1, 0, 2).reshape(t, A_W)
    prep_in = tuple(a[n][o] for n in ('od_a_re', 'od_a_im', 'od_b_re', 'od_b_im', 'od_c_re', 'od_c_im', 'od_log_dt'))
    (are, aim, bre, bim, cre, cim), prep_vjp = jax.vjp(s5_prep, *prep_in)
    s5w = (bre, bim, are, aim, cre, cim, fs['od_d'][o][None])
    y, sr, si = s5_fwd(proj, 10, *s5w, f"od_s5{o}")
    z = glu_z(y, f"od_glu_z{o}")
    gpre = mm(z, wf['od_glu_w'][o], name=f"od_glu_mm{o}")
    glu_b = fs['od_glu_b'][o][None]
    ssm = glu_out(y, gpre, glu_b, f"od_glu_out{o}")
    mix = jnp.concatenate([attn_tm, ssm], axis=1)
    yo = mm(mix, wf['od_w_out'][o], name=f"od_out{o}")
    return yo, dict(proj=proj, qh=qh, kh=kh, vh=vh, oh=oh, lse=lse, sink_b=sink_b, prep_vjp=prep_vjp, s5w=s5w, y=y, sr=sr, si=si,
                    z=z, gpre=gpre, glu_b=glu_b, mix=mix)


def _odd_bwd(o, dyg, s, h1, a, wf, fs, tabs, gs):
    cos, sin = tabs
    t = h1.shape[0]
    dmix = mm(dyg, wf['od_w_out'][o], tb=True, name=f"od_dmix{o}")
    dwo = mm(s['mix'], dyg, ta=True, out_dtype=BF16, name=f"od_dwo{o}")
    dgp, dglu_b = glu_bwd_gate(dmix, s['y'], s['gpre'], s['glu_b'], f"od_glu_bwd_gate{o}")
    dz_mm = mm(dgp, wf['od_glu_w'][o], tb=True, name=f"od_glu_dz{o}")
    dglu_w = mm(s['z'], dgp, ta=True, out_dtype=BF16, name=f"od_glu_dw{o}")
    dy = glu_bwd_y(dmix, s['y'], s['gpre'], s['glu_b'], dz_mm, f"od_glu_bwd_y{o}")
    du, dbre, dbim, dare, daim, dcre, dcim, dd = s5_bwd(s['proj'], 10, dy, s['sr'], s['si'], *s['s5w'], f"od_s5_bwd{o}")
    ga = s['prep_vjp']((dare, daim, dbre, dbim, dcre, dcim))
    for n, g in zip(('od_a_re', 'od_a_im', 'od_b_re', 'od_b_im', 'od_c_re', 'od_c_im', 'od_log_dt'), ga):
        gs[n][o] = g
    gs['od_d'][o] = dd[0]
    gs['od_glu_b'][o] = dglu_b[0]
    dattn_hm = dmix[:, :A_W].reshape(t, 16, 64).transpose(1, 0, 2)
    do, dvec, dsink = sink_bwd(dattn_hm, s['oh'], s['lse'], s['sink_b'], f"od_sink_bwd{o}")
    gs['od_sinks'][o] = dsink[:, 0, 0]
    dq, dk, dv = attn_bwd(s['qh'], s['kh'], s['vh'], do.reshape(2, 8, t, 64), s['lse'].reshape(2, 8, t, 64), dvec.reshape(2, 8, t, 64),
                          127, 64 ** -0.5, f"od_attn_bwd{o}")
    dq_tm = dq.transpose(2, 0, 1, 3).reshape(t, A_W)
    dk_tm = dk.transpose(1, 0, 2).reshape(t, LANE)
    dv_tm = dv.transpose(1, 0, 2).reshape(t, LANE)
    dproj = odd_dproj(dq_tm, dk_tm, dv_tm, du, cos, sin, f"od_dproj{o}")
    dh1 = mm(dproj, wf['od_w_in'][o], tb=True, name=f"od_dh{o}")
    dwi = mm(h1, dproj, ta=True, out_dtype=BF16, tm=2048, tn=768, name=f"od_dwi{o}")
    return dh1, dwi, dwo, dglu_w


def kernel(x, c, positions, ada_w, ada_b, norm_mix, norm_ffn, norm_final, ev_w_in, ev_conv_w, ev_conv_b, ev_gate_a_w, ev_gate_a_b, ev_gate_x_w, ev_gate_x_b, ev_lambda, ev_w_out, od_w_in, od_sinks, od_a_re, od_a_im, od_b_re, od_b_im, od_c_re, od_c_im, od_d, od_log_dt, od_glu_w, od_glu_b, od_w_out, ffn_w_in, ffn_conv_w, ffn_conv_b, ffn_w_out, loss_target, m_ada_w, m_ada_b, m_norm_mix, m_norm_ffn, m_norm_final, m_ev_w_in, m_ev_conv_w, m_ev_conv_b, m_ev_gate_a_w, m_ev_gate_a_b, m_ev_gate_x_w, m_ev_gate_x_b, m_ev_lambda, m_ev_w_out, m_od_w_in, m_od_sinks, m_od_a_re, m_od_a_im, m_od_b_re, m_od_b_im, m_od_c_re, m_od_c_im, m_od_d, m_od_log_dt, m_od_glu_w, m_od_glu_b, m_od_w_out, m_ffn_w_in, m_ffn_conv_w, m_ffn_conv_b, m_ffn_w_out, v_ada_w, v_ada_b, v_norm_mix, v_norm_ffn, v_norm_final, v_ev_w_in, v_ev_conv_w, v_ev_conv_b, v_ev_gate_a_w, v_ev_gate_a_b, v_ev_gate_x_w, v_ev_gate_x_b, v_ev_lambda, v_ev_w_out, v_od_w_in, v_od_sinks, v_od_a_re, v_od_a_im, v_od_b_re, v_od_b_im, v_od_c_re, v_od_c_im, v_od_d, v_od_log_dt, v_od_glu_w, v_od_glu_b, v_od_w_out, v_ffn_w_in, v_ffn_conv_w, v_ffn_conv_b, v_ffn_w_out):
    a = dict(locals())
    xi, yi, ci = _place()
    chip = 2 * xi + yi
    me = 2 * chip + ci
    x0, target, pos = x[0], loss_target[0], positions[0]
    d = D_MODEL

    g0 = all_gather8(_pack([c] + [a[n] for n in SMALL_SHARDED]), "gather_small").reshape(N_DEV, -1)
    c_all = g0[:, :d]
    fs, off = {}, d
    for n in SMALL_SHARDED:
        sh = a[n].shape
        parts = g0[0::2, off:off + math.prod(sh)].reshape((N_CHIP,) + sh)
        fs[n] = jnp.moveaxis(parts, 0, -2).reshape(sh[:-1] + (N_CHIP * sh[-1],))
        off += math.prod(sh)
    cond_all = silu_rows(c_all, "silu")

    modp = jnp.stack([mm(cond_all, ada_w[l], tm=8, tn=512, name=f"mod{l}") for l in range(DEPTH)])
    mod_all = all_gather8(modp.reshape(-1, LANE), "gather_mod").reshape(N_DEV, DEPTH, N_DEV, 6 * d // N_CHIP)[0::2]
    mod_me = lax.dynamic_index_in_dim(mod_all, me, axis=2, keepdims=False)
    mod = jnp.transpose(mod_me, (1, 0, 2)).reshape(DEPTH, 6 * d) + ada_b
    mods = [[mod[l, i * d:(i + 1) * d][None] for i in range(6)] for l in range(DEPTH)]

    full = gather_weights([a[n].astype(BF16) for n in BIG], "gather_weights")
    wf = {}
    for n, f in zip(BIG, full):
        _, nl, r, cc = f.shape
        if n in COL_SHARDED:
            wf[n] = jnp.transpose(f, (1, 2, 0, 3)).reshape(nl, r, N_CHIP * cc)
        else:
            wf[n] = jnp.transpose(f, (1, 0, 2, 3)).reshape(nl, N_CHIP * r, cc)

    ffn_cw, ffn_cb = fs['ffn_conv_w'], ffn_conv_b

    tabs128 = rope_tables(pos, 64)
    tabs64 = rope_tables(pos, 32)

    saved = []
    xcur = x0
    for l in range(DEPTH):
        sh1, sc1, g1, sh2, sc2, g2 = mods[l]
        s = dict(x=xcur)
        s['h1'] = norm_mod(xcur, norm_mix[l][None], sc1, sh1, f"norm_mix{l}")
        if l % 2 == 0:
            s['y'], s['mixer'] = _even_fwd(l // 2, s['h1'], a, wf, fs, tabs128)
        else:
            s['y'], s['mixer'] = _odd_fwd(l // 2, s['h1'], a, wf, fs, tabs64)
        s['x2'], s['h2'] = resid_norm_mod(xcur, s['y'], g1, norm_ffn[l][None], sc2, sh2, f"norm_ffn{l}")
        s['f'], s['ffn'] = _ffn_fwd(l, s['h2'], wf, ffn_cw[l], ffn_cb[l][None])
        xcur = resid_add(s['x2'], s['f'], g2, f"resid{l}")
        saved.append(s)

    dx, loss_part, dnf = final_loss(xcur, norm_final[None], target, "loss")
    loss = lax.psum(loss_part[0, 0], ("x", "y", "c"))

    gs = {n: {} for n in SMALL}
    gbig = {n: {} for n in BIG}
    dmod = {}
    gs['norm_final'][0] = dnf[0]
    for l in reversed(range(DEPTH)):
        sh1, sc1, g1, sh2, sc2, g2 = mods[l]
        s = saved[l]
        df, dg2 = resid_bwd(dx, s['f'], g2, f"resid_bwd_ffn{l}")
        dh2, dwi, dwo, dcw, dcb = _ffn_bwd(l, df, s['ffn'], s['h2'], wf, ffn_cw[l], ffn_cb[l][None])
        gbig['ffn_w_in'][l], gbig['ffn_w_out'][l], gs['ffn_conv_w'][l], gs['ffn_conv_b'][l] = dwi, dwo, dcw, dcb[0]
        dx2, dsh2, dsc2, dgam2 = norm_mod_bwd(dh2, s['x2'], dx, norm_ffn[l][None], sc2, f"norm_ffn_bwd{l}")
        gs['norm_ffn'][l] = dgam2[0]
        dyg, dg1 = resid_bwd(dx2, s['y'], g1, f"resid_bwd_mix{l}")
        if l % 2 == 0:
            dh1, dwi, dwo = _even_bwd(l // 2, dyg, s['mixer'], s['h1'], a, wf, fs, tabs128, gs)
            gbig['ev_w_in'][l // 2], gbig['ev_w_out'][l // 2] = dwi, dwo
        else:
            dh1, dwi, dwo, dglu_w = _odd_bwd(l // 2, dyg, s['mixer'], s['h1'], a, wf, fs, tabs64, gs)
            gbig['od_w_in'][l // 2], gbig['od_w_out'][l // 2], gbig['od_glu_w'][l // 2] = dwi, dwo, dglu_w
        dx, dsh1, dsc1, dgam1 = norm_mod_bwd(dh1, s['x'], dx2, norm_mix[l][None], sc1, f"norm_mix_bwd{l}")
        gs['norm_mix'][l] = dgam1[0]
        dmod[l] = jnp.concatenate([dsh1, dsc1, dg1, dsh2, dsc2, dg2], axis=1)[0]
    grad_x = dx[None]
    gs['ada_b'] = dmod

    grads = {}
    g5 = []
    for n in BIG:
        g = jnp.stack([gbig[n][i] for i in range(len(gbig[n]))])
        nl = g.shape[0]
        if n in COL_SHARDED:
            hr, ns = g.shape[1] // 2, g.shape[2] // N_CHIP
            g5.append(g.reshape(nl, 2, hr, N_CHIP, ns).transpose(1, 3, 0, 2, 4))
        else:
            hr = g.shape[1] // N_CHIP // 2
            g5.append(g.reshape(nl, N_CHIP, 2, hr, g.shape[2]).transpose(2, 1, 0, 3, 4))
    from_sibling = pair_swap(g5, "pair_swap_grads")
    pair = [pair_sum(g, r, ci, f"pair_sum_{n}") for n, g, r in zip(BIG, g5, from_sibling)]
    pieces = exchange_chips(pair, "exchange_grads")
    halves = [sum_devices(p, f"sum_{n}") for n, p in zip(BIG, pieces)]
    for n, g in zip(BIG, join_halves(halves, "join_grads")):
        grads[n] = g.reshape(a[n].shape)

    small_full = [jnp.stack([gs[n][i] for i in range(len(gs[n]))]) if n != 'norm_final' else gs[n][0] for n in SMALL]
    small_shapes = [g.shape for g in small_full]
    gs_all = all_gather8(_pack(small_full), "gather_small_grads")
    gs_sum = sum_devices(gs_all[:, None], "sum_small").reshape(-1)
    for n, g in zip(SMALL, _unpack(gs_sum, small_shapes)):
        if n in SMALL_SHARDED:
            w = a[n].shape[-1]
            g = lax.dynamic_slice_in_dim(g, chip * w, w, axis=g.ndim - 1)
        grads[n] = g
    assert SMALL[0] == 'ada_b'
    dmod_all = gs_all.reshape(N_DEV, -1)[:, :DEPTH * 6 * d].reshape(N_DEV, DEPTH, 6 * d)
    wcols = 6 * d // N_CHIP
    grads['ada_w'] = jnp.stack([
        mm(cond_all, lax.dynamic_slice_in_dim(dmod_all[:, l], chip * wcols, wcols, axis=1), ta=True, tm=2048, tn=512, name=f"ada_dw{l}")
        for l in range(DEPTH)])

    delta, new_m, new_v = {}, {}, {}
    for n in ['ada_w'] + BIG:
        sh = a[n].shape
        two_d = lambda t: t.reshape(-1, sh[-1])
        dl, nm, nv = adamw(two_d(a[n]), two_d(grads[n]), two_d(a['m_' + n]), two_d(a['v_' + n]), f"adamw_{n}")
        delta[n], new_m[n], new_v[n] = dl.reshape(sh), nm.reshape(sh), nv.reshape(sh)
    shapes = [a[n].shape for n in SMALL]
    dl, nm, nv = adamw(_pack([a[n] for n in SMALL]), _pack([grads[n] for n in SMALL]), _pack([a['m_' + n] for n in SMALL]),
                       _pack([a['v_' + n] for n in SMALL]), "adamw_small")
    for n, t1, t2, t3 in zip(SMALL, _unpack(dl.reshape(-1), shapes), _unpack(nm.reshape(-1), shapes), _unpack(nv.reshape(-1), shapes)):
        delta[n], new_m[n], new_v[n] = t1, t2, t3

    return (loss, grad_x, *[grads[n] for n in WEIGHTS], *[delta[n] for n in WEIGHTS], *[new_m[n] for n in WEIGHTS],
            *[new_v[n] for n in WEIGHTS])
```

```python
import functools
import math

import jax
import jax.numpy as jnp
from jax import lax
from jax.experimental import pallas as pl
from jax.experimental.pallas import tpu as pltpu

F32 = jnp.float32
BF16 = jnp.bfloat16
MESH = pl.DeviceIdType.MESH

D_MODEL = 2048
SEQ = 2048
DEPTH = 4
N_DEV = 8
N_CHIP = 4
BLK = 128
LANE = 128
V7X_VMEM_LIMIT = 56 * 1024 * 1024
NORM_EPS = 1e-6
ROPE_THETA = 10000.0
LRU_C = 8.0
D_FF = 5504
A_PATTERNS = ((128, 1), (512, 4), (2048, 16))
ADAM_LR, ADAM_B1, ADAM_B2, ADAM_EPS, ADAM_WD, ADAM_STEP = 0.001, 0.9, 0.999, 1e-08, 0.01, 10
NEG = -1e30


def _params(n_grid):
    return pltpu.CompilerParams(dimension_semantics=("arbitrary",) * n_grid, vmem_limit_bytes=V7X_VMEM_LIMIT)


def _pick(dim, pref):
    best = None
    for t in range(LANE, min(dim, pref) + 1, LANE):
        if dim % t == 0:
            best = t
    return best or dim


def _sigmoid(x):
    return 1.0 / (1.0 + jnp.exp(-x))


_GELU_C = math.sqrt(2.0 / math.pi)


def _gelu(x):
    t = jnp.tanh(_GELU_C * (x + 0.044715 * (x * x * x)))
    return 0.5 * x * (1.0 + t)


def _gelu_grad(x):
    t = jnp.tanh(_GELU_C * (x + 0.044715 * (x * x * x)))
    return 0.5 * (1.0 + t) + 0.5 * x * (1.0 - t * t) * (_GELU_C * (1.0 + 3.0 * 0.044715 * (x * x)))


def _softplus(x):
    return jnp.maximum(x, 0.0) + jnp.log(1.0 + jnp.exp(-jnp.abs(x)))


def _neg_expm1(x):
    series = -x * (1.0 + x * (0.5 + x * (1.0 / 6.0 + x * (1.0 / 24.0))))
    return jnp.where(x > -0.03, series, 1.0 - jnp.exp(x))


def _shift_down(x, k):
    if k == 0:
        return x
    row = lax.broadcasted_iota(jnp.int32, x.shape, 0)
    return jnp.where(row >= k, pltpu.roll(x, k, 0), 0.0)


def _shift_up(x, k):
    if k == 0:
        return x
    n = x.shape[0]
    row = lax.broadcasted_iota(jnp.int32, x.shape, 0)
    return jnp.where(row < n - k, pltpu.roll(x, n - k, 0), 0.0)


def _dot(a, b, ca=1, cb=0):
    return lax.dot_general(a.astype(BF16), b.astype(BF16), (((ca,), (cb,)), ((), ())), preferred_element_type=F32)


def _rope(x, cos, sin_signed, half):
    c = x.shape[1]
    reps = c // cos.shape[1]
    cos_c = jnp.tile(cos, (1, reps)) if reps > 1 else cos
    sin_c = jnp.tile(sin_signed, (1, reps)) if reps > 1 else sin_signed
    lane = lax.broadcasted_iota(jnp.int32, x.shape, 1)
    first = (lane % (2 * half)) < half
    partner = jnp.where(first, pltpu.roll(x, c - half, 1), pltpu.roll(x, half, 1))
    return x * cos_c + partner * sin_c


def _rope_t(dy, cos, sin_signed, half):
    c = dy.shape[1]
    reps = c // cos.shape[1]
    cos_c = jnp.tile(cos, (1, reps)) if reps > 1 else cos
    sin_c = jnp.tile(sin_signed, (1, reps)) if reps > 1 else sin_signed
    lane = lax.broadcasted_iota(jnp.int32, dy.shape, 1)
    first = (lane % (2 * half)) < half
    ys = dy * sin_c
    partner = jnp.where(first, pltpu.roll(ys, c - half, 1), pltpu.roll(ys, half, 1))
    return dy * cos_c + partner


def _tile_call(name, fn, grid, ins, outs, acc_axis=None):
    n_in = len(ins)
    accs = [o[4] for o in outs]

    def body(*refs):
        vals = fn(*[r[...] for r in refs[:n_in]])
        if not isinstance(vals, (tuple, list)):
            vals = (vals,)
        for r, v, acc in zip(refs[n_in:], vals, accs):
            if acc:
                first = pl.program_id(acc_axis) == 0

                @pl.when(first)
                def _():
                    r[...] = v.astype(r.dtype)

                @pl.when(jnp.logical_not(first))
                def _():
                    r[...] += v.astype(r.dtype)
            else:
                r[...] = v.astype(r.dtype)

    res = pl.pallas_call(
        body, name=name, grid=grid,
        in_specs=[pl.BlockSpec(b, im) for _, b, im in ins],
        out_specs=[pl.BlockSpec(o[2], o[3]) for o in outs],
        out_shape=[jax.ShapeDtypeStruct(o[0], o[1]) for o in outs],
        compiler_params=_params(len(grid)),
    )(*[a for a, _, _ in ins])
    return res


def _row_tile(cols, n_arrays, rows):
    budget = 24 * 1024 * 1024 // (2 * 4 * max(n_arrays, 1) * cols)
    t = 8
    while t * 2 <= budget and rows % (t * 2) == 0 and t * 2 <= 1024:
        t *= 2
    return t


def _rows_call(name, fn, rows, tiled, full, outs_tiled, outs_acc=()):
    cols = max([a.shape[1] for a in tiled] + [c for c, _ in outs_tiled])
    tt = _row_tile(cols, len(tiled) + len(outs_tiled), rows)
    ins = [(a, (tt, a.shape[1]), lambda i: (i, 0)) for a in tiled]
    ins += [(a, a.shape, (lambda nd: (lambda i: (0,) * nd))(a.ndim)) for a in full]
    outs = [((rows, c), dt, (tt, c), lambda i: (i, 0), False) for c, dt in outs_tiled]
    outs += [(s, dt, s, (lambda nd: (lambda i: (0,) * nd))(len(s)), True) for s, dt in outs_acc]
    return _tile_call(name, fn, (rows // tt,), ins, outs, acc_axis=0)


def mm(a, b, *, layer=None, ta=False, tb=False, out_dtype=F32, tm=None, tn=None, tk=None, name):
    m, k = (a.shape[1], a.shape[0]) if ta else a.shape
    b_dims = b.shape if layer is None else b.shape[1:]
    n = b_dims[0] if tb else b_dims[1]
    tm = tm or _pick(m, 1024)
    tn = tn or _pick(n, 1024)
    tk = tk or k
    assert m % tm == 0 and n % tn == 0 and k % tk == 0, (name, m, n, k, tm, tn, tk)
    nk = k // tk
    a_spec = pl.BlockSpec((tk, tm), lambda i, j, kk: (kk, i)) if ta else pl.BlockSpec((tm, tk), lambda i, j, kk: (i, kk))
    if layer is None:
        b_spec = pl.BlockSpec((tn, tk), lambda i, j, kk: (j, kk)) if tb else pl.BlockSpec((tk, tn), lambda i, j, kk: (kk, j))
    elif tb:
        b_spec = pl.BlockSpec((None, tn, tk), lambda i, j, kk: (layer, j, kk))
    else:
        b_spec = pl.BlockSpec((None, tk, tn), lambda i, j, kk: (layer, kk, j))
    ca, cb = (0 if ta else 1), (1 if tb else 0)

    def body(a_ref, b_ref, o_ref, *scratch):
        p = _dot(a_ref[...], b_ref[...], ca, cb)
        if nk == 1:
            o_ref[...] = p.astype(o_ref.dtype)
        else:
            acc = scratch[0]
            kk = pl.program_id(2)

            @pl.when(kk == 0)
            def _():
                acc[...] = p

            @pl.when(kk > 0)
            def _():
                acc[...] += p

            @pl.when(kk == nk - 1)
            def _():
                o_ref[...] = acc[...].astype(o_ref.dtype)

    return pl.pallas_call(
        body, name=name, grid=(m // tm, n // tn, nk),
        in_specs=[a_spec, b_spec],
        out_specs=pl.BlockSpec((tm, tn), lambda i, j, kk: (i, j)),
        out_shape=jax.ShapeDtypeStruct((m, n), out_dtype),
        scratch_shapes=[pltpu.VMEM((tm, tn), F32)] if nk > 1 else [],
        compiler_params=_params(3),
    )(a, b)


def _rstd(x):
    return lax.rsqrt(jnp.mean(x * x, axis=-1, keepdims=True) + NORM_EPS)


def norm_mod(x, gamma, sc, sh, name):
    def fn(x, gamma, sc, sh):
        return (x * _rstd(x)) * gamma * (1.0 + sc) + sh
    return _rows_call(name, fn, x.shape[0], [x], [gamma, sc, sh], [(x.shape[1], BF16)])[0]


def resid_norm_mod(x, y, g, gamma, sc, sh, name):
    def fn(x, y, g, gamma, sc, sh):
        x2 = x + g * y
        return x2, (x2 * _rstd(x2)) * gamma * (1.0 + sc) + sh
    return _rows_call(name, fn, x.shape[0], [x, y], [g, gamma, sc, sh], [(x.shape[1], F32), (x.shape[1], BF16)])


def resid_add(x, y, g, name):
    def fn(x, y, g):
        return x + g * y
    return _rows_call(name, fn, x.shape[0], [x, y], [g], [(x.shape[1], F32)])[0]


def resid_bwd(dxo, f, g, name):
    def fn(dxo, f, g):
        return dxo * g, jnp.sum(dxo * f, axis=0, keepdims=True)
    d = dxo.shape[1]
    return _rows_call(name, fn, dxo.shape[0], [dxo, f], [g], [(d, BF16)], [((1, d), F32)])


def norm_mod_bwd(dh, x, dres, gamma, sc, name):
    def fn(dh, x, dres, gamma, sc):
        rstd = _rstd(x)
        xhat = x * rstd
        dxhat = dh * (gamma * (1.0 + sc))
        dx = rstd * (dxhat - xhat * jnp.mean(dxhat * xhat, axis=-1, keepdims=True))
        dhx = dh * xhat
        return (dres + dx, jnp.sum(dh, axis=0, keepdims=True), jnp.sum(dhx * gamma, axis=0, keepdims=True),
                jnp.sum(dhx * (1.0 + sc), axis=0, keepdims=True))
    d = x.shape[1]
    return _rows_call(name, fn, x.shape[0], [dh, x, dres], [gamma, sc], [(d, F32)], [((1, d), F32)] * 3)


def final_loss(x, gamma, target, name):
    d = x.shape[1]

    def fn(x, target, gamma):
        rstd = _rstd(x)
        xhat = x * rstd
        e = xhat * gamma - target
        part = 0.5 * jnp.sum(jnp.sum(e * e, axis=-1, keepdims=True) / d, axis=0, keepdims=True)
        dy = e / d
        dxhat = dy * gamma
        dx = rstd * (dxhat - xhat * jnp.mean(dxhat * xhat, axis=-1, keepdims=True))
        return dx, jnp.broadcast_to(part, (8, LANE)), jnp.sum(dy * xhat, axis=0, keepdims=True)
    return _rows_call(name, fn, x.shape[0], [x, target], [gamma], [(d, F32)], [((8, LANE), F32), ((1, d), F32)])


def _conv3(u, w, b):
    return b + w[2:3] * u + w[1:2] * _shift_down(u, 1) + w[0:1] * _shift_down(u, 2)


def _ffn_blocks(u, cw, cb):
    t = u.shape[0]
    nb = D_FF // LANE
    return [(u, (t, LANE), lambda j: (0, j)), (u, (t, LANE), lambda j: (0, j + nb)),
            (cw, (3, LANE), lambda j: (0, j)), (cw, (3, LANE), lambda j: (0, j + nb)),
            (cb, (1, LANE), lambda j: (0, j)), (cb, (1, LANE), lambda j: (0, j + nb))]


def ffn_act(u, cw, cb, name):
    t = u.shape[0]

    def fn(ug, uv, wg, wv, bg, bv):
        return _gelu(_conv3(ug, wg, bg)) * _conv3(uv, wv, bv)
    return _tile_call(name, fn, (D_FF // LANE,), _ffn_blocks(u, cw, cb), [((t, D_FF), BF16, (t, LANE), lambda j: (0, j), False)])[0]


def ffn_act_bwd(u, dact, cw, cb, name):
    t = u.shape[0]

    def conv_t(duc, us, w):
        du = w[2:3] * duc + w[1:2] * _shift_up(duc, 1) + w[0:1] * _shift_up(duc, 2)
        dw = jnp.concatenate([jnp.sum(duc * _shift_down(us, 2), axis=0, keepdims=True),
                              jnp.sum(duc * _shift_down(us, 1), axis=0, keepdims=True),
                              jnp.sum(duc * us, axis=0, keepdims=True)], axis=0)
        return du, dw, jnp.sum(duc, axis=0, keepdims=True)

    def fn(ug, uv, wg, wv, bg, bv, da):
        g = _conv3(ug, wg, bg)
        v = _conv3(uv, wv, bv)
        du_g, dw_g, db_g = conv_t(da * v * _gelu_grad(g), ug, wg)
        du_v, dw_v, db_v = conv_t(da * _gelu(g), uv, wv)
        return du_g, du_v, dw_g, dw_v, db_g, db_v
    ins = _ffn_blocks(u, cw, cb) + [(dact, (t, LANE), lambda j: (0, j))]
    col = lambda rows, dt: ((rows, D_FF), dt, (rows, LANE), lambda j: (0, j), False)
    du_g, du_v, dw_g, dw_v, db_g, db_v = _tile_call(name, fn, (D_FF // LANE,), ins,
                                                    [col(t, BF16), col(t, BF16), col(3, F32), col(3, F32), col(1, F32), col(1, F32)])
    return jnp.concatenate([du_g, du_v], axis=1), jnp.concatenate([dw_g, dw_v], axis=1), jnp.concatenate([db_g, db_v], axis=1)


def attn_fwd(q, k, v, max_dist, scale, name):
    n, r, l, dh = q.shape
    nb = l // BLK
    m_rows = r * BLK

    def body(q_ref, kc_ref, kp_ref, vc_ref, vp_ref, o_ref, lse_ref):
        b = pl.program_id(1)
        qv = q_ref[...].reshape(m_rows, dh)
        s_c = _dot(qv, kc_ref[...], 1, 1) * scale
        s_p = _dot(qv, kp_ref[...], 1, 1) * scale
        qi = lax.broadcasted_iota(jnp.int32, (m_rows, BLK), 0) % BLK
        kj = lax.broadcasted_iota(jnp.int32, (m_rows, BLK), 1)
        s_c = jnp.where(kj <= qi, s_c, NEG)
        s_p = jnp.where((kj >= qi + (BLK - max_dist)) & (b > 0), s_p, NEG)
        mx = jnp.maximum(jnp.max(s_c, axis=1, keepdims=True), jnp.max(s_p, axis=1, keepdims=True))
        p_c = jnp.exp(s_c - mx)
        p_p = jnp.exp(s_p - mx)
        den = jnp.sum(p_c, axis=1, keepdims=True) + jnp.sum(p_p, axis=1, keepdims=True)
        o = (_dot(p_c, vc_ref[...]) + _dot(p_p, vp_ref[...])) / den
        o_ref[...] = o.reshape(r, BLK, dh)
        lse_ref[...] = jnp.broadcast_to(mx + jnp.log(den), (m_rows, dh)).reshape(r, BLK, dh)

    qspec = pl.BlockSpec((None, r, BLK, dh), lambda i, b: (i, 0, b, 0))
    cur = pl.BlockSpec((None, BLK, dh), lambda i, b: (i, b, 0))
    prev = pl.BlockSpec((None, BLK, dh), lambda i, b: (i, jnp.maximum(b - 1, 0), 0))
    return pl.pallas_call(
        body, name=name, grid=(n, nb),
        in_specs=[qspec, cur, prev, cur, prev],
        out_specs=[qspec, qspec],
        out_shape=[jax.ShapeDtypeStruct((n, r, l, dh), F32)] * 2,
        compiler_params=_params(2),
    )(q, k, k, v, v)


def attn_bwd(q, k, v, do, lse, dvec, max_dist, scale, name):
    n, r, l, dh = q.shape
    nb = l // BLK
    m_rows = r * BLK

    def body(qc_ref, qn_ref, kc_ref, kp_ref, vc_ref, vp_ref, doc_ref, don_ref, lc_ref, ln_ref, dc_ref, dn_ref,
             dq_ref, dk_ref, dv_ref):
        b = pl.program_id(1)
        qi = lax.broadcasted_iota(jnp.int32, (m_rows, BLK), 0) % BLK
        kj = lax.broadcasted_iota(jnp.int32, (m_rows, BLK), 1)
        m_cur = kj <= qi
        m_prev = kj >= qi + (BLK - max_dist)

        def pair(q_ref, do_ref, l_ref, d_ref, k_ref, v_ref, mask):
            qv = q_ref[...].reshape(m_rows, dh)
            dov = do_ref[...].reshape(m_rows, dh)
            lrow = l_ref[...].reshape(m_rows, dh)[:, 0:1]
            drow = d_ref[...].reshape(m_rows, dh)[:, 0:1]
            s = _dot(qv, k_ref[...], 1, 1) * scale
            p = jnp.where(mask, jnp.exp(jnp.where(mask, s, NEG) - lrow), 0.0)
            dp = _dot(dov, v_ref[...], 1, 1)
            ds = p * (dp - drow) * scale
            return qv, dov, p, ds

        q_a, do_a, p_a, ds_a = pair(qc_ref, doc_ref, lc_ref, dc_ref, kc_ref, vc_ref, m_cur)
        _, _, _, ds_b = pair(qc_ref, doc_ref, lc_ref, dc_ref, kp_ref, vp_ref, m_prev & (b > 0))
        q_c, do_c, p_c, ds_c = pair(qn_ref, don_ref, ln_ref, dn_ref, kc_ref, vc_ref, m_prev & (b < nb - 1))
        dq = _dot(ds_a, kc_ref[...]) + _dot(ds_b, kp_ref[...])
        dq_ref[...] = dq.reshape(r, BLK, dh)
        dk_ref[...] = _dot(ds_a, q_a, 0, 0) + _dot(ds_c, q_c, 0, 0)
        dv_ref[...] = _dot(p_a, do_a, 0, 0) + _dot(p_c, do_c, 0, 0)

    qcur = pl.BlockSpec((None, r, BLK, dh), lambda i, b: (i, 0, b, 0))
    qnext = pl.BlockSpec((None, r, BLK, dh), lambda i, b: (i, 0, jnp.minimum(b + 1, nb - 1), 0))
    cur = pl.BlockSpec((None, BLK, dh), lambda i, b: (i, b, 0))
    prev = pl.BlockSpec((None, BLK, dh), lambda i, b: (i, jnp.maximum(b - 1, 0), 0))
    return pl.pallas_call(
        body, name=name, grid=(n, nb),
        in_specs=[qcur, qnext, cur, prev, cur, prev, qcur, qnext, qcur, qnext, qcur, qnext],
        out_specs=[qcur, cur, cur],
        out_shape=[jax.ShapeDtypeStruct((n, r, l, dh), F32), jax.ShapeDtypeStruct((n, l, dh), F32),
                   jax.ShapeDtypeStruct((n, l, dh), F32)],
        compiler_params=_params(2),
    )(q, q, k, k, v, v, do, do, lse, lse, dvec, dvec)


def _scan_rows(t_len, step, init, reverse=False):
    n_chunks = t_len // 8

    def chunk(ci, carry):
        c = (n_chunks - 1 - ci) if reverse else ci
        base = pl.multiple_of(c * 8, 8)
        order = range(7, -1, -1) if reverse else range(8)
        return step(base, order, carry)
    return lax.fori_loop(0, n_chunks, chunk, init)


def _put_row(acc, i, row):
    rid = lax.broadcasted_iota(jnp.int32, acc.shape, 0)
    return jnp.where(rid == i, row, acc)


def _real_scan(a_ref, b_ref, h_ref, t_len, reverse=False):
    c = a_ref.shape[1]

    def step(base, order, h):
        a8 = a_ref[pl.ds(base, 8), :]
        b8 = b_ref[pl.ds(base, 8), :]
        out = jnp.zeros((8, c), F32)
        for i in order:
            h = a8[i:i + 1, :] * h + b8[i:i + 1, :]
            out = _put_row(out, i, h)
        h_ref[pl.ds(base, 8), :] = out
        return h
    _scan_rows(t_len, step, jnp.zeros((1, c), F32), reverse)


def _complex_scan(ar, ai, br_ref, bi_ref, sr_ref, si_ref, t_len, reverse=False):
    c = br_ref.shape[1]

    def step(base, order, carry):
        sr, si = carry
        br8 = br_ref[pl.ds(base, 8), :]
        bi8 = bi_ref[pl.ds(base, 8), :]
        outr = jnp.zeros((8, c), F32)
        outi = jnp.zeros((8, c), F32)
        for i in order:
            nr = ar * sr - ai * si + br8[i:i + 1, :]
            ni = ar * si + ai * sr + bi8[i:i + 1, :]
            sr, si = nr, ni
            outr = _put_row(outr, i, sr)
            outi = _put_row(outi, i, si)
        sr_ref[pl.ds(base, 8), :] = outr
        si_ref[pl.ds(base, 8), :] = outi
        return sr, si
    _scan_rows(t_len, step, (jnp.zeros((1, c), F32), jnp.zeros((1, c), F32)), reverse)


def _rglru_pre(xb, cw, cb, gaw, gab, gxw, gxb, lam):
    xc = cb + cw[3:4] * xb + cw[2:3] * _shift_down(xb, 1) + cw[1:2] * _shift_down(xb, 2) + cw[0:1] * _shift_down(xb, 3)
    r = _sigmoid(_dot(xc, gaw) + gab)
    ig = _sigmoid(_dot(xc, gxw) + gxb)
    sp = _softplus(-lam)
    log_a = -LRU_C * r * sp
    a = jnp.exp(log_a)
    mult = jnp.sqrt(_neg_expm1(2.0 * log_a))
    return xc, r, ig, sp, a, mult


def rglru_fwd(proj, cw, cb, gaw, gab, gxw, gxb, lam, xb_col, yb_col, name):
    t = proj.shape[0]
    nh = cw.shape[1] // LANE

    def body(xb_ref, yb_ref, cw_ref, cb_ref, gaw_ref, gab_ref, gxw_ref, gxb_ref, lam_ref, out_ref, h_ref, a_scr, b_scr):
        xc, r, ig, sp, a, mult = _rglru_pre(xb_ref[...], cw_ref[...], cb_ref[...], gaw_ref[...], gab_ref[...],
                                            gxw_ref[...], gxb_ref[...], lam_ref[...])
        a_scr[...] = a
        b_scr[...] = mult * (ig * xc)
        _real_scan(a_scr, b_scr, h_ref, t)
        out_ref[...] = (h_ref[...] * _gelu(yb_ref[...])).astype(out_ref.dtype)

    col = lambda off: pl.BlockSpec((t, LANE), lambda h: (0, off + h))
    vec = lambda rows: pl.BlockSpec((rows, LANE), lambda h: (0, h))
    wsp = pl.BlockSpec((None, LANE, LANE), lambda h: (h, 0, 0))
    return pl.pallas_call(
        body, name=name, grid=(nh,),
        in_specs=[col(xb_col), col(yb_col), vec(4), vec(1), wsp, vec(1), wsp, vec(1), vec(1)],
        out_specs=[col(0), col(0)],
        out_shape=[jax.ShapeDtypeStruct((t, nh * LANE), BF16), jax.ShapeDtypeStruct((t, nh * LANE), F32)],
        scratch_shapes=[pltpu.VMEM((t, LANE), F32)] * 2,
        compiler_params=_params(1),
    )(proj, proj, cw, cb, gaw, gab, gxw, gxb, lam)


def rglru_bwd(proj, hs, dlru, dlru_col, cw, cb, gaw, gab, gxw, gxb, lam, xb_col, yb_col, name):
    t = proj.shape[0]
    nh = cw.shape[1] // LANE

    def body(xb_ref, yb_ref, h_ref, dl_ref, cw_ref, cb_ref, gaw_ref, gab_ref, gxw_ref, gxb_ref, lam_ref,
             dxb_ref, dyb_ref, dcw_ref, dcb_ref, dgaw_ref, dgab_ref, dgxw_ref, dgxb_ref, dlam_ref, an_scr, dh_scr, gh_scr):
        xb = xb_ref[...]
        yb = yb_ref[...]
        cwv = cw_ref[...]
        lam = lam_ref[...]
        xc, r, ig, sp, a, mult = _rglru_pre(xb, cwv, cb_ref[...], gaw_ref[...], gab_ref[...], gxw_ref[...], gxb_ref[...], lam)
        h = h_ref[...]
        dl = dl_ref[...]
        dyb_ref[...] = (dl * h * _gelu_grad(yb)).astype(dyb_ref.dtype)
        dh_scr[...] = dl * _gelu(yb)
        an_scr[...] = _shift_up(a, 1)
        _real_scan(an_scr, dh_scr, gh_scr, t, reverse=True)
        gh = gh_scr[...]
        da = gh * _shift_down(h, 1)
        dmult = gh * ig * xc
        dig = gh * mult * xc
        dxc = gh * mult * ig
        dla = (da - dmult * a / mult) * a
        dr = dla * (-LRU_C * sp)
        dsp = jnp.sum(dla * (-LRU_C * r), axis=0, keepdims=True)
        dlam_ref[...] = dsp * (-_sigmoid(-lam))
        dpr = dr * r * (1.0 - r)
        dpi = dig * ig * (1.0 - ig)
        dgab_ref[...] = jnp.sum(dpr, axis=0, keepdims=True)
        dgxb_ref[...] = jnp.sum(dpi, axis=0, keepdims=True)
        dgaw_ref[...] = _dot(xc, dpr, 0, 0)
        dgxw_ref[...] = _dot(xc, dpi, 0, 0)
        dxc = dxc + _dot(dpr, gaw_ref[...], 1, 1) + _dot(dpi, gxw_ref[...], 1, 1)
        dxb = cwv[3:4] * dxc + cwv[2:3] * _shift_up(dxc, 1) + cwv[1:2] * _shift_up(dxc, 2) + cwv[0:1] * _shift_up(dxc, 3)
        dxb_ref[...] = dxb.astype(dxb_ref.dtype)
        dcw_ref[...] = jnp.concatenate([jnp.sum(dxc * _shift_down(xb, 3 - i), axis=0, keepdims=True) for i in range(4)], axis=0)
        dcb_ref[...] = jnp.sum(dxc, axis=0, keepdims=True)

    col = lambda off: pl.BlockSpec((t, LANE), lambda h: (0, off + h))
    vec = lambda rows: pl.BlockSpec((rows, LANE), lambda h: (0, h))
    wsp = pl.BlockSpec((None, LANE, LANE), lambda h: (h, 0, 0))
    w = nh * LANE
    sds = jax.ShapeDtypeStruct
    return pl.pallas_call(
        body, name=name, grid=(nh,),
        in_specs=[col(xb_col), col(yb_col), col(0), col(dlru_col), vec(4), vec(1), wsp, vec(1), wsp, vec(1), vec(1)],
        out_specs=[col(0), col(0), vec(4), vec(1), wsp, vec(1), wsp, vec(1), vec(1)],
        out_shape=[sds((t, w), BF16), sds((t, w), BF16), sds((4, w), F32), sds((1, w), F32), sds((nh, LANE, LANE), F32),
                   sds((1, w), F32), sds((nh, LANE, LANE), F32), sds((1, w), F32), sds((1, w), F32)],
        scratch_shapes=[pltpu.VMEM((t, LANE), F32)] * 3,
        compiler_params=_params(1),
    )(proj, proj, hs, dlru, cw, cb, gaw, gab, gxw, gxb, lam)


S5_CHUNKS = 8
S5_STATES = 512


def s5_fwd(proj, u_col, bre, bim, are, aim, cre, cim, dsk, name):
    t = proj.shape[0]

    def body(u_ref, bre_ref, bim_ref, are_ref, aim_ref, cre_ref, cim_ref, d_ref, y_ref, sr_ref, si_ref, br_scr, bi_scr):
        u = u_ref[...]
        br_scr[...] = _dot(u, bre_ref[...])
        bi_scr[...] = _dot(u, bim_ref[...])
        _complex_scan(are_ref[...], aim_ref[...], br_scr, bi_scr, sr_ref, si_ref, t)
        y_ref[...] = _dot(sr_ref[...], cre_ref[...]) - _dot(si_ref[...], cim_ref[...]) + d_ref[...] * u

    ucol = pl.BlockSpec((t, LANE), lambda c: (0, u_col + c))
    ycol = pl.BlockSpec((t, LANE), lambda c: (0, c))
    scol = pl.BlockSpec((t, S5_STATES), lambda c: (0, c))
    bsp = pl.BlockSpec((None, LANE, S5_STATES), lambda c: (c, 0, 0))
    csp = pl.BlockSpec((None, S5_STATES, LANE), lambda c: (c, 0, 0))
    asp = pl.BlockSpec((1, S5_STATES), lambda c: (0, c))
    dsp = pl.BlockSpec((1, LANE), lambda c: (0, c))
    sds = jax.ShapeDtypeStruct
    return pl.pallas_call(
        body, name=name, grid=(S5_CHUNKS,),
        in_specs=[ucol, bsp, bsp, asp, asp, csp, csp, dsp],
        out_specs=[ycol, scol, scol],
        out_shape=[sds((t, S5_CHUNKS * LANE), F32), sds((t, S5_CHUNKS * S5_STATES), F32), sds((t, S5_CHUNKS * S5_STATES), F32)],
        scratch_shapes=[pltpu.VMEM((t, S5_STATES), F32)] * 2,
        compiler_params=_params(1),
    )(proj, bre, bim, are, aim, cre, cim, dsk)


def s5_bwd(proj, u_col, dy, sr, si, bre, bim, are, aim, cre, cim, dsk, name):
    t = proj.shape[0]
    half = S5_STATES // 2

    def body(u_ref, dy_ref, sr_ref, si_ref, bre_ref, bim_ref, are_ref, aim_ref, cre_ref, cim_ref, d_ref,
             du_ref, dbre_ref, dbim_ref, dare_ref, daim_ref, dcre_ref, dcim_ref, dd_ref, dsr_scr, dsi_scr, gr_scr, gi_scr):
        hh = pl.program_id(1)
        u = u_ref[...]
        dy = dy_ref[...]
        dsr_scr[...] = _dot(dy, cre_ref[...], 1, 1)
        dsi_scr[...] = -_dot(dy, cim_ref[...], 1, 1)
        _complex_scan(are_ref[...], -aim_ref[...], dsr_scr, dsi_scr, gr_scr, gi_scr, t, reverse=True)
        gr = gr_scr[...]
        gi = gi_scr[...]
        spr = _shift_down(sr_ref[...], 1)
        spi = _shift_down(si_ref[...], 1)
        dare_ref[...] = jnp.sum(gr * spr + gi * spi, axis=0, keepdims=True)
        daim_ref[...] = jnp.sum(gi * spr - gr * spi, axis=0, keepdims=True)
        du = _dot(gr, bre_ref[...], 1, 1) + _dot(gi, bim_ref[...], 1, 1)

        @pl.when(hh == 0)
        def _():
            du_ref[...] = du + d_ref[...] * dy

        @pl.when(hh > 0)
        def _():
            du_ref[...] += du

        dbre_ref[...] = _dot(u, gr, 0, 0)
        dbim_ref[...] = _dot(u, gi, 0, 0)
        dcre_ref[...] = _dot(sr_ref[...], dy, 0, 0)
        dcim_ref[...] = -_dot(si_ref[...], dy, 0, 0)
        dd_ref[...] = jnp.sum(dy * u, axis=0, keepdims=True)

    ucol = pl.BlockSpec((t, LANE), lambda c, h: (0, u_col + c))
    ycol = pl.BlockSpec((t, LANE), lambda c, h: (0, c))
    scol = pl.BlockSpec((t, half), lambda c, h: (0, 2 * c + h))
    bsp = pl.BlockSpec((None, LANE, half), lambda c, h: (c, 0, h))
    csp = pl.BlockSpec((None, half, LANE), lambda c, h: (c, h, 0))
    asp = pl.BlockSpec((1, half), lambda c, h: (0, 2 * c + h))
    dsp = pl.BlockSpec((1, LANE), lambda c, h: (0, c))
    sds = jax.ShapeDtypeStruct
    return pl.pallas_call(
        body, name=name, grid=(S5_CHUNKS, 2),
        in_specs=[ucol, ycol, scol, scol, bsp, bsp, asp, asp, csp, csp, dsp],
        out_specs=[ycol, bsp, bsp, asp, asp, csp, csp, dsp],
        out_shape=[sds((t, S5_CHUNKS * LANE), F32), sds((S5_CHUNKS, LANE, S5_STATES), F32), sds((S5_CHUNKS, LANE, S5_STATES), F32),
                   sds((1, S5_CHUNKS * S5_STATES), F32), sds((1, S5_CHUNKS * S5_STATES), F32),
                   sds((S5_CHUNKS, S5_STATES, LANE), F32), sds((S5_CHUNKS, S5_STATES, LANE), F32), sds((1, S5_CHUNKS * LANE), F32)],
        scratch_shapes=[pltpu.VMEM((t, half), F32)] * 4,
        compiler_params=_params(2),
    )(proj, dy, sr, si, bre, bim, are, aim, cre, cim, dsk)


def s5_prep(a_re, a_im, b_re, b_im, c_re, c_im, log_dt):
    lam = lax.complex(a_re, a_im)
    dt = jnp.exp(log_dt)[:, None]
    a_bar = jnp.exp(lam * dt)
    b_bar = ((a_bar - 1.0) / lam)[..., None] * lax.complex(b_re, b_im)
    g, p, cg = b_re.shape
    eye = jnp.eye(8, dtype=F32)

    def in_map(m):
        m = m.reshape(g // 8, 8, p, cg)
        return jnp.einsum("ab,kapc->kacbp", eye, m).reshape(g // 8, 8 * cg, 8 * p)

    def out_map(m):
        m = m.reshape(g // 8, 8, cg, p)
        return jnp.einsum("ab,kacp->kapbc", eye, m).reshape(g // 8, 8 * p, 8 * cg)

    return (jnp.real(a_bar).reshape(1, g * p), jnp.imag(a_bar).reshape(1, g * p), in_map(jnp.real(b_bar)), in_map(jnp.imag(b_bar)),
            out_map(c_re), out_map(c_im))


def rope_tables(pos, half):
    inv = ROPE_THETA ** (-jnp.arange(half, dtype=F32) / half)
    ang = pos.astype(F32)[:, None] * inv
    cos, sin = jnp.cos(ang), jnp.sin(ang)
    reps = max(LANE // (2 * half), 1)
    return jnp.tile(jnp.concatenate([cos, cos], axis=1), (1, reps)), jnp.tile(jnp.concatenate([-sin, sin], axis=1), (1, reps))


A_W = 1024
ROW_T = 256


def _tiled(a, width, col):
    return (a, (ROW_T, width), lambda i: (i, col))


def _out_tiled(t, width, dtype):
    return ((t, width), dtype, (ROW_T, width), lambda i: (i, 0), False)


def qkv_rope_even(proj, cos, sin, name):
    t = proj.shape[0]

    def fn(q, k, v, cos, sin):
        return _rope(q, cos, sin, 64), _rope(k, cos, sin, 64), v
    ins = [_tiled(proj, A_W, 0), _tiled(proj, A_W, 1), _tiled(proj, A_W, 2), _tiled(cos, LANE, 0), _tiled(sin, LANE, 0)]
    return _tile_call(name, fn, (t // ROW_T,), ins, [_out_tiled(t, A_W, BF16)] * 3)


def merge3(o, lse, name):
    t = o[0].shape[0]

    def fn(o1, o2, o3, l1, l2, l3):
        mx = jnp.maximum(jnp.maximum(l1, l2), l3)
        e1, e2, e3 = jnp.exp(l1 - mx), jnp.exp(l2 - mx), jnp.exp(l3 - mx)
        den = e1 + e2 + e3
        out = (e1 * o1 + e2 * o2 + e3 * o3) / den
        return out, out, mx + jnp.log(den)
    ins = [_tiled(a, A_W, 0) for a in list(o) + list(lse)]
    return _tile_call(name, fn, (t // ROW_T,), ins, [_out_tiled(t, A_W, BF16), _out_tiled(t, A_W, F32), _out_tiled(t, A_W, F32)])


def _segsum_bcast(x, width):
    parts = []
    for h in range(x.shape[1] // width):
        s = jnp.sum(x[:, h * width:(h + 1) * width], axis=1, keepdims=True)
        parts.append(jnp.broadcast_to(s, (x.shape[0], width)))
    return jnp.concatenate(parts, axis=1)


def even_attn_prep(dmix, attn, name):
    t = attn.shape[0]

    def fn(dout, attn):
        return dout, _segsum_bcast(dout * attn, LANE)
    ins = [_tiled(dmix, A_W, 0), _tiled(attn, A_W, 0)]
    return _tile_call(name, fn, (t // ROW_T,), ins, [_out_tiled(t, A_W, BF16), _out_tiled(t, A_W, F32)])


def even_dproj(dq, dk, dv, dxb, dyb, cos, sin, name):
    t = dxb.shape[0]

    def fn(q1, q2, q3, k1, k2, k3, v1, v2, v3, dxb, dyb, cos, sin):
        return jnp.concatenate([_rope_t(q1 + q2 + q3, cos, sin, 64).astype(BF16), _rope_t(k1 + k2 + k3, cos, sin, 64).astype(BF16),
                                (v1 + v2 + v3).astype(BF16), dxb, dyb], axis=1)
    ins = [_tiled(a, A_W, 0) for a in list(dq) + list(dk) + list(dv) + [dxb, dyb]] + [_tiled(cos, LANE, 0), _tiled(sin, LANE, 0)]
    return _tile_call(name, fn, (t // ROW_T,), ins, [_out_tiled(t, 5 * A_W, BF16)])[0]


def perm(x, d):
    t = x.shape[0]
    return x.reshape(t // d, d, 8, LANE).transpose(1, 2, 0, 3).reshape(d * 8, t // d, LANE)


def unperm(xp, d):
    n, l, _ = xp.shape
    return xp.reshape(d, 8, l, LANE).transpose(2, 0, 1, 3).reshape(l * d, 8 * LANE)


def qkv_rope_odd(proj, cos, sin, name):
    t = proj.shape[0]

    def fn(q, k, v, cos, sin):
        return _rope(q, cos, sin, 32), _rope(k, cos, sin, 32), v
    ins = [_tiled(proj, A_W, 0), _tiled(proj, LANE, 8), _tiled(proj, LANE, 9), _tiled(cos, LANE, 0), _tiled(sin, LANE, 0)]
    return _tile_call(name, fn, (t // ROW_T,), ins, [_out_tiled(t, A_W, BF16), _out_tiled(t, LANE, BF16), _out_tiled(t, LANE, BF16)])


HEAD_ROWS = 1024


def _head_blocks(a):
    return (a, (None, HEAD_ROWS, a.shape[2]), lambda h, i: (h, i, 0))


def sink_fwd(o, lse, sink_b, name):
    nh, t, dh = o.shape

    def fn(o, lse, s):
        return o * _sigmoid(lse - s)
    ins = [_head_blocks(o), _head_blocks(lse), (sink_b, (None, 1, dh), lambda h, i: (h, 0, 0))]
    return _tile_call(name, fn, (nh, t // HEAD_ROWS), ins, [((nh, t, dh), BF16, (None, HEAD_ROWS, dh), lambda h, i: (h, i, 0), False)])[0]


def sink_bwd(dof, o, lse, sink_b, name):
    nh, t, dh = o.shape

    def fn(dof, o, lse, s):
        keep = _sigmoid(lse - s)
        dk = jnp.sum(dof * o, axis=1, keepdims=True)
        dlse = dk * keep * (1.0 - keep)
        return dof * keep, dk * keep * keep, -jnp.sum(dlse, axis=0, keepdims=True)
    ins = [_head_blocks(dof), _head_blocks(o), _head_blocks(lse), (sink_b, (None, 1, dh), lambda h, i: (h, 0, 0))]
    outs = [((nh, t, dh), BF16, (None, HEAD_ROWS, dh), lambda h, i: (h, i, 0), False),
            ((nh, t, dh), F32, (None, HEAD_ROWS, dh), lambda h, i: (h, i, 0), False),
            ((nh, 1, dh), F32, (None, 1, dh), lambda h, i: (h, 0, 0), True)]
    return _tile_call(name, fn, (nh, t // HEAD_ROWS), ins, outs, acc_axis=1)


def odd_dproj(dq, dk, dv, du, cos, sin, name):
    t = dq.shape[0]

    def fn(dq, dk, dv, du, cos, sin):
        return jnp.concatenate([_rope_t(dq, cos, sin, 32), _rope_t(dk, cos, sin, 32), dv, du], axis=1)
    ins = [_tiled(dq, A_W, 0), _tiled(dk, LANE, 0), _tiled(dv, LANE, 0), _tiled(du, A_W, 0), _tiled(cos, LANE, 0), _tiled(sin, LANE, 0)]
    return _tile_call(name, fn, (t // ROW_T,), ins, [_out_tiled(t, 2 * A_W + 2 * LANE, BF16)])[0]


def glu_z(y, name):
    return _rows_call(name, _gelu, y.shape[0], [y], [], [(y.shape[1], BF16)])[0]


def glu_out(y, gpre, b, name):
    def fn(y, gpre, b):
        return _gelu(y) * _sigmoid(gpre + b)
    return _rows_call(name, fn, y.shape[0], [y, gpre], [b], [(y.shape[1], BF16)])[0]


def glu_bwd_gate(dmix, y, gpre, b, name):
    t = y.shape[0]

    def fn(dout, y, gpre, b):
        gate = _sigmoid(gpre + b)
        dgp = dout * _gelu(y) * gate * (1.0 - gate)
        return dgp, jnp.sum(dgp, axis=0, keepdims=True)
    ins = [_tiled(dmix, A_W, 1), _tiled(y, A_W, 0), _tiled(gpre, A_W, 0), (b, (1, A_W), lambda i: (0, 0))]
    outs = [_out_tiled(t, A_W, BF16), ((1, A_W), F32, (1, A_W), lambda i: (0, 0), True)]
    return _tile_call(name, fn, (t // ROW_T,), ins, outs, acc_axis=0)


def glu_bwd_y(dmix, y, gpre, b, dz_mm, name):
    t = y.shape[0]

    def fn(dout, y, gpre, b, dz_mm):
        return (dout * _sigmoid(gpre + b) + dz_mm) * _gelu_grad(y)
    ins = [_tiled(dmix, A_W, 1), _tiled(y, A_W, 0), _tiled(gpre, A_W, 0), (b, (1, A_W), lambda i: (0, 0)), _tiled(dz_mm, A_W, 0)]
    return _tile_call(name, fn, (t // ROW_T,), ins, [_out_tiled(t, A_W, F32)])[0]


def adamw(w, g, m, v, name):
    def fn(w, g, m, v):
        m = ADAM_B1 * m + (1.0 - ADAM_B1) * g
        v = ADAM_B2 * v + (1.0 - ADAM_B2) * (g * g)
        m_hat = m / (1.0 - ADAM_B1 ** ADAM_STEP)
        v_hat = v / (1.0 - ADAM_B2 ** ADAM_STEP)
        return -ADAM_LR * (m_hat / (jnp.sqrt(v_hat) + ADAM_EPS) + ADAM_WD * w), m, v
    c = w.shape[1]
    return _rows_call(name, fn, w.shape[0], [w, g, m, v], [], [(c, F32)] * 3)


def _row_block(rows, row_bytes, limit):
    best = 16
    for t in range(16, rows + 1, 16):
        if rows % t == 0 and t * row_bytes <= limit:
            best = t
    return best


def _sum_in_order(v):
    s = v[0].astype(F32)
    for d in range(1, v.shape[0]):
        s = s + v[d].astype(F32)
    return s


def sum_devices(parts, name):
    nd, nl, r, c = parts.shape
    tr = _row_block(r, c * (parts.dtype.itemsize * nd + 4), 18 * 1024 * 1024)
    ins = [(parts, (nd, None, tr, c), lambda l, i: (0, l, i, 0))]
    outs = [((nl, r, c), F32, (None, tr, c), lambda l, i: (l, i, 0), False)]
    return _tile_call(name, _sum_in_order, (nl, r // tr), ins, outs)[0]


def silu_rows(c_all, name):
    def fn(c):
        return c * _sigmoid(c)
    return _rows_call(name, fn, c_all.shape[0], [c_all], [], [(c_all.shape[1], F32)])[0]


def _place():
    x, y, c = lax.axis_index("x"), lax.axis_index("y"), lax.axis_index("c")
    return x, y, c


ANY = pl.BlockSpec(memory_space=pl.ANY)


def all_gather8(v, name):
    r, cdim = v.shape

    def body(x_ref, out_ref, send_sems, recv_sems, local_sem):
        x, y, c = _place()
        me, sibling = (x, y, c), (x, y, 1 - c)
        chips = [(1 - x, y), (x, 1 - y), (1 - x, 1 - y)]

        def rows(px, py, pc):
            return out_ref.at[4 * px + 2 * py + pc]

        def copy(k, block, to, src=None):
            return pltpu.make_async_remote_copy(
                src_ref=rows(*block) if src is None else src, dst_ref=rows(*block),
                send_sem=send_sems.at[k], recv_sem=recv_sems.at[k], device_id=to, device_id_type=MESH)

        mine = pltpu.make_async_copy(x_ref, rows(*me), local_sem)
        mine.start()
        first = [copy(0, me, sibling, src=x_ref)]
        first += [copy(1 + j, me, (*chip, c), src=x_ref) for j, chip in enumerate(chips)]
        for cp in first:
            cp.start()
        passed = [copy(4 + j, (*chip, c), sibling) for j, chip in enumerate(chips)]
        for j, chip in enumerate(chips):
            copy(1 + j, (*chip, c), me).wait_recv()
            passed[j].start()
        copy(0, sibling, me).wait_recv()
        for j, chip in enumerate(chips):
            copy(4 + j, (*chip, 1 - c), me).wait_recv()
        for cp in first + passed:
            cp.wait_send()
        mine.wait()

    return pl.pallas_call(
        body, name=name, out_shape=jax.ShapeDtypeStruct((N_DEV, r, cdim), v.dtype),
        in_specs=[ANY], out_specs=ANY,
        scratch_shapes=[pltpu.SemaphoreType.DMA((7,)), pltpu.SemaphoreType.DMA((7,)), pltpu.SemaphoreType.DMA],
    )(v)


def gather_weights(shards, name):
    n = len(shards)

    def body(*refs):
        ins, outs = refs[:n], refs[n:2 * n]
        send_sems, recv_sems = refs[2 * n:]
        x, y, c = _place()
        sibling = (x, y, 1 - c)
        chips = [(1 - x, y), (x, 1 - y), (1 - x, 1 - y)]
        my_chip = 2 * x + y

        def half(t, chip_slot, start):
            hr = ins[t].shape[1] // 2
            return outs[t].at[chip_slot, :, pl.ds(start, hr), :]

        def copy(t, k, src, dst, to):
            return pltpu.make_async_remote_copy(src_ref=src, dst_ref=dst, send_sem=send_sems.at[6 * t + k],
                                                recv_sem=recv_sems.at[6 * t + k], device_id=to, device_id_type=MESH)

        def lows(t):
            hr = ins[t].shape[1] // 2
            return hr, pl.multiple_of(c * hr, 16), pl.multiple_of((1 - c) * hr, 16)

        started = []
        for t in range(n):
            hr, lo, _ = lows(t)
            for j, chip in enumerate(chips):
                cp = copy(t, j, ins[t].at[:, pl.ds(lo, hr), :], half(t, my_chip, lo), (*chip, c))
                cp.start()
                started.append(cp)
        for t in range(n):
            hr, lo, _ = lows(t)
            for j, (px, py) in enumerate(chips):
                slot = 2 * px + py
                copy(t, j, half(t, slot, lo), half(t, slot, lo), (px, py, c)).wait_recv()
                fwd = copy(t, 3 + j, half(t, slot, lo), half(t, slot, lo), sibling)
                fwd.start()
                started.append(fwd)
        for t in range(n):
            hr, _, lo_sib = lows(t)
            for j, (px, py) in enumerate(chips):
                slot = 2 * px + py
                copy(t, 3 + j, half(t, slot, lo_sib), half(t, slot, lo_sib), sibling).wait_recv()
        for cp in started:
            cp.wait_send()

    got = pl.pallas_call(
        body, name=name,
        out_shape=[jax.ShapeDtypeStruct((N_CHIP,) + s.shape, s.dtype) for s in shards],
        in_specs=[ANY] * n, out_specs=[ANY] * n,
        scratch_shapes=[pltpu.SemaphoreType.DMA((6 * n,)), pltpu.SemaphoreType.DMA((6 * n,))],
    )(*shards)
    my_chip = 2 * lax.axis_index("x") + lax.axis_index("y")
    return [lax.dynamic_update_index_in_dim(g, s, my_chip, 0) for g, s in zip(got, shards)]


def pair_swap(grads, name):
    n = len(grads)

    def body(*refs):
        ins, outs = refs[:n], refs[n:2 * n]
        send_sems, recv_sems = refs[2 * n:]
        x, y, c = _place()
        sibling = (x, y, 1 - c)
        sends = []
        for t in range(n):
            for q in range(N_CHIP):
                cp = pltpu.make_async_remote_copy(src_ref=ins[t].at[1 - c, q], dst_ref=outs[t].at[q], send_sem=send_sems.at[N_CHIP * t + q],
                                                  recv_sem=recv_sems.at[N_CHIP * t + q], device_id=sibling, device_id_type=MESH)
                cp.start()
                sends.append(cp)
        for cp in sends:
            cp.wait_recv()
            cp.wait_send()

    return pl.pallas_call(
        body, name=name,
        out_shape=[jax.ShapeDtypeStruct(g.shape[1:], g.dtype) for g in grads],
        in_specs=[ANY] * n, out_specs=[ANY] * n,
        scratch_shapes=[pltpu.SemaphoreType.DMA((N_CHIP * n,)), pltpu.SemaphoreType.DMA((N_CHIP * n,))],
    )(*grads)


def pair_sum(g5, from_sibling, core, name):
    _, nq, nl, hr, c = g5.shape
    tr = _row_block(hr, c * 2, 3 * 1024 * 1024)
    flag =jnp.broadcast_to(core.astype(F32), (8, LANE))

    def fn(g0, g1, r, flag):
        own = jnp.where(flag[0:1, 0:1] == 0.0, g0.astype(F32), g1.astype(F32))
        return own + r.astype(F32)
    ins = [(g5, (None, None, None, tr, c), lambda q, l, i: (0, q, l, i, 0)), (g5, (None, None, None, tr, c), lambda q, l, i: (1, q, l, i, 0)),
           (from_sibling, (None, None, tr, c), lambda q, l, i: (q, l, i, 0)), (flag, (8, LANE), lambda q, l, i: (0, 0))]
    outs = [((nq, nl, hr, c), BF16, (None, None, tr, c), lambda q, l, i: (q, l, i, 0), False)]
    return _tile_call(name, fn, (nq, nl, hr // tr), ins, outs)[0]


def exchange_chips(pairs, name):
    n = len(pairs)

    def body(*refs):
        ins, outs = refs[:n], refs[n:2 * n]
        send_sems, recv_sems = refs[2 * n:]
        x, y, c = _place()
        chips = [(1 - x, y), (x, 1 - y), (1 - x, 1 - y)]
        my_chip = 2 * x + y
        sends = []
        for t in range(n):
            for j, (px, py) in enumerate(chips):
                cp = pltpu.make_async_remote_copy(src_ref=ins[t].at[2 * px + py], dst_ref=outs[t].at[my_chip], send_sem=send_sems.at[3 * t + j],
                                                  recv_sem=recv_sems.at[3 * t + j], device_id=(px, py, c), device_id_type=MESH)
                cp.start()
                sends.append(cp)
        for t in range(n):
            for j, (px, py) in enumerate(chips):
                slot = outs[t].at[2 * px + py]
                pltpu.make_async_remote_copy(src_ref=slot, dst_ref=slot, send_sem=send_sems.at[3 * t + j], recv_sem=recv_sems.at[3 * t + j],
                                             device_id=(px, py, c), device_id_type=MESH).wait_recv()
        for cp in sends:
            cp.wait_send()

    got = pl.pallas_call(
        body, name=name,
        out_shape=[jax.ShapeDtypeStruct(p.shape, p.dtype) for p in pairs],
        in_specs=[ANY] * n, out_specs=[ANY] * n,
        scratch_shapes=[pltpu.SemaphoreType.DMA((3 * n,)), pltpu.SemaphoreType.DMA((3 * n,))],
    )(*pairs)
    my_chip = 2 * lax.axis_index("x") + lax.axis_index("y")
    return [lax.dynamic_update_index_in_dim(o, lax.dynamic_index_in_dim(p, my_chip, 0, keepdims=False), my_chip, 0) for o, p in zip(got, pairs)]


def join_halves(halves, name):
    n = len(halves)
    chunks = [(t, l, j) for t in range(n) for l in range(halves[t].shape[0]) for j in range(2)]

    def body(*refs):
        ins, outs = refs[:n], refs[n:2 * n]
        send_sems, recv_sems = refs[2 * n:]
        x, y, c = _place()
        sibling = (x, y, 1 - c)
        pending = []
        for k, (t, l, j) in enumerate(chunks):
            h_ref, o_ref = ins[t], outs[t]
            hr = h_ref.shape[1]
            rows = hr // 2
            lo = pl.multiple_of(c * hr + j * rows, 8)
            lo_sib = pl.multiple_of((1 - c) * hr + j * rows, 8)
            src = h_ref.at[l, pl.ds(j * rows, rows), :]
            cp = pltpu.make_async_remote_copy(src_ref=src, dst_ref=o_ref.at[l, pl.ds(lo, rows), :], send_sem=send_sems.at[k],
                                              recv_sem=recv_sems.at[k], device_id=sibling, device_id_type=MESH)
            cp.start()
            got = pltpu.make_async_remote_copy(src_ref=src, dst_ref=o_ref.at[l, pl.ds(lo_sib, rows), :], send_sem=send_sems.at[k],
                                               recv_sem=recv_sems.at[k], device_id=sibling, device_id_type=MESH)
            pending.append((cp, got))
        for cp, got in pending:
            got.wait_recv()
            cp.wait_send()

    got = pl.pallas_call(
        body, name=name,
        out_shape=[jax.ShapeDtypeStruct((h.shape[0], 2 * h.shape[1], h.shape[2]), h.dtype) for h in halves],
        in_specs=[ANY] * n, out_specs=[ANY] * n,
        scratch_shapes=[pltpu.SemaphoreType.DMA((len(chunks),)), pltpu.SemaphoreType.DMA((len(chunks),))],
    )(*halves)
    ci = lax.axis_index("c")
    return [lax.dynamic_update_slice(g, h, (0, ci * h.shape[1], 0)) for g, h in zip(got, halves)]


WEIGHTS = ['ada_w', 'ada_b', 'norm_mix', 'norm_ffn', 'norm_final', 'ev_w_in', 'ev_conv_w', 'ev_conv_b', 'ev_gate_a_w', 'ev_gate_a_b',
           'ev_gate_x_w', 'ev_gate_x_b', 'ev_lambda', 'ev_w_out', 'od_w_in', 'od_sinks', 'od_a_re', 'od_a_im', 'od_b_re', 'od_b_im',
           'od_c_re', 'od_c_im', 'od_d', 'od_log_dt', 'od_glu_w', 'od_glu_b', 'od_w_out', 'ffn_w_in', 'ffn_conv_w', 'ffn_conv_b', 'ffn_w_out']
BIG = ['ev_w_in', 'ev_w_out', 'od_w_in', 'od_glu_w', 'od_w_out', 'ffn_w_in', 'ffn_w_out']
COL_SHARDED = ('ev_w_in', 'od_w_in', 'ffn_w_in')
SMALL_SHARDED = ['ev_conv_w', 'od_d', 'od_glu_b', 'ffn_conv_w']
SMALL = [n for n in WEIGHTS if n not in BIG and n != 'ada_w']


def _pack(arrs):
    flat = jnp.concatenate([a.reshape(-1).astype(F32) for a in arrs])
    rows = -(-flat.shape[0] // (1024 * LANE)) * 1024
    return jnp.pad(flat, (0, rows * LANE - flat.shape[0])).reshape(rows, LANE)


def _unpack(flat, shapes):
    out, off = [], 0
    for s in shapes:
        n = math.prod(s)
        out.append(flat[..., off:off + n].reshape(flat.shape[:-1] + tuple(s)))
        off += n
    return out


def _ffn_fwd(l, h2, wf, cw, cb):
    u = mm(h2, wf['ffn_w_in'], layer=l, tm=2048, tn=256, name=f"ffn_in{l}")
    act = ffn_act(u, cw, cb, f"ffn_act{l}")
    f = mm(act, wf['ffn_w_out'], layer=l, tm=1024, tn=512, name=f"ffn_out{l}")
    return f, dict(u=u, act=act)


def _ffn_bwd(l, df, s, h2, wf, cw, cb):
    dact = mm(df, wf['ffn_w_out'], layer=l, tb=True, tm=2048, tn=128, name=f"ffn_dact{l}")
    dwo = mm(s['act'], df, ta=True, out_dtype=BF16, tm=D_FF, tn=256, tk=512, name=f"ffn_dwo{l}")
    du, dcw, dcb = ffn_act_bwd(s['u'], dact, cw, cb, f"ffn_act_bwd{l}")
    dh2 = mm(du, wf['ffn_w_in'], layer=l, tb=True, tm=1024, tn=512, tk=D_FF, name=f"ffn_dh{l}")
    dwi = mm(h2, du, ta=True, out_dtype=BF16, tm=2048, tn=256, name=f"ffn_dwi{l}")
    return dh2, dwi, dwo, dcw, dcb


def _even_fwd(e, h1, a, wf, fs, tabs):
    cos, sin = tabs
    proj = mm(h1, wf['ev_w_in'], layer=e, name=f"ev_in{e}")
    q, k, v = qkv_rope_even(proj, cos, sin, f"ev_rope{e}")
    outs, lses = [], []
    for window, d in A_PATTERNS:
        o, lse = attn_fwd(perm(q, d)[:, None], perm(k, d), perm(v, d), window // d, LANE ** -0.5, f"ev_attn{e}_{d}")
        outs.append(unperm(o[:, 0], d))
        lses.append(unperm(lse[:, 0], d))
    attn_bf, attn, lse_tot = merge3(outs, lses, f"ev_merge{e}")
    lru, hs = rglru_fwd(proj, fs['ev_conv_w'][e], a['ev_conv_b'][e][None], a['ev_gate_a_w'][e], a['ev_gate_a_b'][e][None],
                        a['ev_gate_x_w'][e], a['ev_gate_x_b'][e][None], a['ev_lambda'][e][None], 24, 32, f"ev_lru{e}")
    mix = jnp.concatenate([attn_bf, lru], axis=1)
    y = mm(mix, wf['ev_w_out'], layer=e, name=f"ev_out{e}")
    return y, dict(proj=proj, q=q, k=k, v=v, attn=attn, lse=lse_tot, hs=hs, mix=mix)


def _even_bwd(e, dyg, s, h1, a, wf, fs, tabs, gs):
    cos, sin = tabs
    dmix = mm(dyg, wf['ev_w_out'], layer=e, tb=True, name=f"ev_dmix{e}")
    dwo = mm(s['mix'], dyg, ta=True, out_dtype=BF16, name=f"ev_dwo{e}")
    do_bf, dvec = even_attn_prep(dmix, s['attn'], f"ev_prep{e}")
    dqs, dks, dvs = [], [], []
    for window, d in A_PATTERNS:
        dq, dk, dv = attn_bwd(perm(s['q'], d)[:, None], perm(s['k'], d), perm(s['v'], d), perm(do_bf, d)[:, None],
                              perm(s['lse'], d)[:, None], perm(dvec, d)[:, None], window // d, LANE ** -0.5, f"ev_attn_bwd{e}_{d}")
        dqs.append(unperm(dq[:, 0], d))
        dks.append(unperm(dk, d))
        dvs.append(unperm(dv, d))
    dxb, dyb, dcw, dcb, dgaw, dgab, dgxw, dgxb, dlam = rglru_bwd(
        s['proj'], s['hs'], dmix, 8, fs['ev_conv_w'][e], a['ev_conv_b'][e][None], a['ev_gate_a_w'][e], a['ev_gate_a_b'][e][None],
        a['ev_gate_x_w'][e], a['ev_gate_x_b'][e][None], a['ev_lambda'][e][None], 24, 32, f"ev_lru_bwd{e}")
    for n, g in (('ev_conv_w', dcw), ('ev_conv_b', dcb[0]), ('ev_gate_a_w', dgaw), ('ev_gate_a_b', dgab[0]), ('ev_gate_x_w', dgxw),
                 ('ev_gate_x_b', dgxb[0]), ('ev_lambda', dlam[0])):
        gs[n][e] = g
    dproj = even_dproj(dqs, dks, dvs, dxb, dyb, cos, sin, f"ev_dproj{e}")
    dh1 = mm(dproj, wf['ev_w_in'], layer=e, tb=True, tk=2560, name=f"ev_dh{e}")
    dwi = mm(h1, dproj, ta=True, out_dtype=BF16, tm=2048, tn=512, name=f"ev_dwi{e}")
    return dh1, dwi, dwo


def _odd_fwd(o, h1, a, wf, fs, tabs):
    cos, sin = tabs
    t = h1.shape[0]
    proj = mm(h1, wf['od_w_in'], layer=o, name=f"od_in{o}")
    qr, kr, vv = qkv_rope_odd(proj, cos, sin, f"od_rope{o}")
    qh = qr.reshape(t, 2, 8, 64).transpose(1, 2, 0, 3)
    kh = kr.reshape(t, 2, 64).transpose(1, 0, 2)
    vh = vv.reshape(t, 2, 64).transpose(1, 0, 2)
    oh, lse = attn_fwd(qh, kh, vh, 127, 64 ** -0.5, f"od_attn{o}")
    sink_b = jnp.broadcast_to(a['od_sinks'][o].reshape(16, 1, 1), (16, 1, 64))
    oh, lse = oh.reshape(16, t, 64), lse.reshape(16, t, 64)
    attn_hm = sink_fwd(oh, lse, sink_b, f"od_sink{o}")
    attn_tm = attn_hm.transpose(1, 0, 2).reshape(t, A_W)
    prep_in = tuple(a[n][o] for n in ('od_a_re', 'od_a_im', 'od_b_re', 'od_b_im', 'od_c_re', 'od_c_im', 'od_log_dt'))
    (are, aim, bre, bim, cre, cim), prep_vjp = jax.vjp(s5_prep, *prep_in)
    s5w = (bre, bim, are, aim, cre, cim, fs['od_d'][o][None])
    y, sr, si = s5_fwd(proj, 10, *s5w, f"od_s5{o}")
    z = glu_z(y, f"od_glu_z{o}")
    gpre = mm(z, wf['od_glu_w'], layer=o, name=f"od_glu_mm{o}")
    glu_b = fs['od_glu_b'][o][None]
    ssm = glu_out(y, gpre, glu_b, f"od_glu_out{o}")
    mix = jnp.concatenate([attn_tm, ssm], axis=1)
    yo = mm(mix, wf['od_w_out'], layer=o, name=f"od_out{o}")
    return yo, dict(proj=proj, qh=qh, kh=kh, vh=vh, oh=oh, lse=lse, sink_b=sink_b, prep_vjp=prep_vjp, s5w=s5w, y=y, sr=sr, si=si,
                    z=z, gpre=gpre, glu_b=glu_b, mix=mix)


def _odd_bwd(o, dyg, s, h1, a, wf, fs, tabs, gs):
    cos, sin = tabs
    t = h1.shape[0]
    dmix = mm(dyg, wf['od_w_out'], layer=o, tb=True, name=f"od_dmix{o}")
    dwo = mm(s['mix'], dyg, ta=True, out_dtype=BF16, name=f"od_dwo{o}")
    dgp, dglu_b = glu_bwd_gate(dmix, s['y'], s['gpre'], s['glu_b'], f"od_glu_bwd_gate{o}")
    dz_mm = mm(dgp, wf['od_glu_w'], layer=o, tb=True, name=f"od_glu_dz{o}")
    dglu_w = mm(s['z'], dgp, ta=True, out_dtype=BF16, name=f"od_glu_dw{o}")
    dy = glu_bwd_y(dmix, s['y'], s['gpre'], s['glu_b'], dz_mm, f"od_glu_bwd_y{o}")
    du, dbre, dbim, dare, daim, dcre, dcim, dd = s5_bwd(s['proj'], 10, dy, s['sr'], s['si'], *s['s5w'], f"od_s5_bwd{o}")
    ga = s['prep_vjp']((dare, daim, dbre, dbim, dcre, dcim))
    for n, g in zip(('od_a_re', 'od_a_im', 'od_b_re', 'od_b_im', 'od_c_re', 'od_c_im', 'od_log_dt'), ga):
        gs[n][o] = g
    gs['od_d'][o] = dd[0]
    gs['od_glu_b'][o] = dglu_b[0]
    dattn_hm = dmix[:, :A_W].reshape(t, 16, 64).transpose(1, 0, 2)
    do, dvec, dsink = sink_bwd(dattn_hm, s['oh'], s['lse'], s['sink_b'], f"od_sink_bwd{o}")
    gs['od_sinks'][o] = dsink[:, 0, 0]
    dq, dk, dv = attn_bwd(s['qh'], s['kh'], s['vh'], do.reshape(2, 8, t, 64), s['lse'].reshape(2, 8, t, 64), dvec.reshape(2, 8, t, 64),
                          127, 64 ** -0.5, f"od_attn_bwd{o}")
    dq_tm = dq.transpose(2, 0, 1, 3).reshape(t, A_W)
    dk_tm = dk.transpose(1, 0, 2).reshape(t, LANE)
    dv_tm = dv.transpose(1, 0, 2).reshape(t, LANE)
    dproj = odd_dproj(dq_tm, dk_tm, dv_tm, du, cos, sin, f"od_dproj{o}")
    dh1 = mm(dproj, wf['od_w_in'], layer=o, tb=True, name=f"od_dh{o}")
    dwi = mm(h1, dproj, ta=True, out_dtype=BF16, tm=2048, tn=768, name=f"od_dwi{o}")
    return dh1, dwi, dwo, dglu_w


def kernel(x, c, positions, ada_w, ada_b, norm_mix, norm_ffn, norm_final, ev_w_in, ev_conv_w, ev_conv_b, ev_gate_a_w, ev_gate_a_b, ev_gate_x_w, ev_gate_x_b, ev_lambda, ev_w_out, od_w_in, od_sinks, od_a_re, od_a_im, od_b_re, od_b_im, od_c_re, od_c_im, od_d, od_log_dt, od_glu_w, od_glu_b, od_w_out, ffn_w_in, ffn_conv_w, ffn_conv_b, ffn_w_out, loss_target, m_ada_w, m_ada_b, m_norm_mix, m_norm_ffn, m_norm_final, m_ev_w_in, m_ev_conv_w, m_ev_conv_b, m_ev_gate_a_w, m_ev_gate_a_b, m_ev_gate_x_w, m_ev_gate_x_b, m_ev_lambda, m_ev_w_out, m_od_w_in, m_od_sinks, m_od_a_re, m_od_a_im, m_od_b_re, m_od_b_im, m_od_c_re, m_od_c_im, m_od_d, m_od_log_dt, m_od_glu_w, m_od_glu_b, m_od_w_out, m_ffn_w_in, m_ffn_conv_w, m_ffn_conv_b, m_ffn_w_out, v_ada_w, v_ada_b, v_norm_mix, v_norm_ffn, v_norm_final, v_ev_w_in, v_ev_conv_w, v_ev_conv_b, v_ev_gate_a_w, v_ev_gate_a_b, v_ev_gate_x_w, v_ev_gate_x_b, v_ev_lambda, v_ev_w_out, v_od_w_in, v_od_sinks, v_od_a_re, v_od_a_im, v_od_b_re, v_od_b_im, v_od_c_re, v_od_c_im, v_od_d, v_od_log_dt, v_od_glu_w, v_od_glu_b, v_od_w_out, v_ffn_w_in, v_ffn_conv_w, v_ffn_conv_b, v_ffn_w_out):
    a = dict(locals())
    xi, yi, ci = _place()
    chip = 2 * xi + yi
    me = 2 * chip + ci
    x0, target, pos = x[0], loss_target[0], positions[0]
    d = D_MODEL

    g0 = all_gather8(_pack([c] + [a[n] for n in SMALL_SHARDED]), "gather_small").reshape(N_DEV, -1)
    c_all = g0[:, :d]
    fs, off = {}, d
    for n in SMALL_SHARDED:
        sh = a[n].shape
        parts = g0[0::2, off:off + math.prod(sh)].reshape((N_CHIP,) + sh)
        fs[n] = jnp.moveaxis(parts, 0, -2).reshape(sh[:-1] + (N_CHIP * sh[-1],))
        off += math.prod(sh)
    cond_all = silu_rows(c_all, "silu")

    modp = jnp.stack([mm(cond_all, ada_w, layer=l, tm=8, tn=512, name=f"mod{l}") for l in range(DEPTH)])
    mod_all = all_gather8(modp.reshape(-1, LANE), "gather_mod").reshape(N_DEV, DEPTH, N_DEV, 6 * d // N_CHIP)[0::2]
    mod_me = lax.dynamic_index_in_dim(mod_all, me, axis=2, keepdims=False)
    mod = jnp.transpose(mod_me, (1, 0, 2)).reshape(DEPTH, 6 * d) + ada_b
    mods = [[mod[l, i * d:(i + 1) * d][None] for i in range(6)] for l in range(DEPTH)]

    full = gather_weights([a[n].astype(BF16) for n in BIG], "gather_weights")
    wf = {}
    for n, f in zip(BIG, full):
        _, nl, r, cc = f.shape
        if n in COL_SHARDED:
            wf[n] = jnp.transpose(f, (1, 2, 0, 3)).reshape(nl, r, N_CHIP * cc)
        else:
            wf[n] = jnp.transpose(f, (1, 0, 2, 3)).reshape(nl, N_CHIP * r, cc)

    ffn_cw, ffn_cb = fs['ffn_conv_w'], ffn_conv_b

    tabs128 = rope_tables(pos, 64)
    tabs64 = rope_tables(pos, 32)

    saved = []
    xcur = x0
    for l in range(DEPTH):
        sh1, sc1, g1, sh2, sc2, g2 = mods[l]
        s = dict(x=xcur)
        s['h1'] = norm_mod(xcur, norm_mix[l][None], sc1, sh1, f"norm_mix{l}")
        if l % 2 == 0:
            s['y'], s['mixer'] = _even_fwd(l // 2, s['h1'], a, wf, fs, tabs128)
        else:
            s['y'], s['mixer'] = _odd_fwd(l // 2, s['h1'], a, wf, fs, tabs64)
        s['x2'], s['h2'] = resid_norm_mod(xcur, s['y'], g1, norm_ffn[l][None], sc2, sh2, f"norm_ffn{l}")
        s['f'], s['ffn'] = _ffn_fwd(l, s['h2'], wf, ffn_cw[l], ffn_cb[l][None])
        xcur = resid_add(s['x2'], s['f'], g2, f"resid{l}")
        saved.append(s)

    dx, loss_part, dnf = final_loss(xcur, norm_final[None], target, "loss")
    loss = lax.psum(loss_part[0, 0], ("x", "y", "c"))

    gs = {n: {} for n in SMALL}
    gbig = {n: {} for n in BIG}
    dmod = {}
    gs['norm_final'][0] = dnf[0]
    for l in reversed(range(DEPTH)):
        sh1, sc1, g1, sh2, sc2, g2 = mods[l]
        s = saved[l]
        df, dg2 = resid_bwd(dx, s['f'], g2, f"resid_bwd_ffn{l}")
        dh2, dwi, dwo, dcw, dcb = _ffn_bwd(l, df, s['ffn'], s['h2'], wf, ffn_cw[l], ffn_cb[l][None])
        gbig['ffn_w_in'][l], gbig['ffn_w_out'][l], gs['ffn_conv_w'][l], gs['ffn_conv_b'][l] = dwi, dwo, dcw, dcb[0]
        dx2, dsh2, dsc2, dgam2 = norm_mod_bwd(dh2, s['x2'], dx, norm_ffn[l][None], sc2, f"norm_ffn_bwd{l}")
        gs['norm_ffn'][l] = dgam2[0]
        dyg, dg1 = resid_bwd(dx2, s['y'], g1, f"resid_bwd_mix{l}")
        if l % 2 == 0:
            dh1, dwi, dwo = _even_bwd(l // 2, dyg, s['mixer'], s['h1'], a, wf, fs, tabs128, gs)
            gbig['ev_w_in'][l // 2], gbig['ev_w_out'][l // 2] = dwi, dwo
        else:
            dh1, dwi, dwo, dglu_w = _odd_bwd(l // 2, dyg, s['mixer'], s['h1'], a, wf, fs, tabs64, gs)
            gbig['od_w_in'][l // 2], gbig['od_w_out'][l // 2], gbig['od_glu_w'][l // 2] = dwi, dwo, dglu_w
        dx, dsh1, dsc1, dgam1 = norm_mod_bwd(dh1, s['x'], dx2, norm_mix[l][None], sc1, f"norm_mix_bwd{l}")
        gs['norm_mix'][l] = dgam1[0]
        dmod[l] = jnp.concatenate([dsh1, dsc1, dg1, dsh2, dsc2, dg2], axis=1)[0]
    grad_x = dx[None]
    gs['ada_b'] = dmod

    grads = {}
    g5 = []
    for n in BIG:
        g = jnp.stack([gbig[n][i] for i in range(len(gbig[n]))])
        nl = g.shape[0]
        if n in COL_SHARDED:
            hr, ns = g.shape[1] // 2, g.shape[2] // N_CHIP
            g5.append(g.reshape(nl, 2, hr, N_CHIP, ns).transpose(1, 3, 0, 2, 4))
        else:
            hr = g.shape[1] // N_CHIP // 2
            g5.append(g.reshape(nl, N_CHIP, 2, hr, g.shape[2]).transpose(2, 1, 0, 3, 4))
    from_sibling = pair_swap(g5, "pair_swap_grads")
    pair = [pair_sum(g, r, ci, f"pair_sum_{n}") for n, g, r in zip(BIG, g5, from_sibling)]
    pieces = exchange_chips(pair, "exchange_grads")
    halves = [sum_devices(p, f"sum_{n}") for n, p in zip(BIG, pieces)]
    for n, g in zip(BIG, join_halves(halves, "join_grads")):
        grads[n] = g.reshape(a[n].shape)

    small_full = [jnp.stack([gs[n][i] for i in range(len(gs[n]))]) if n != 'norm_final' else gs[n][0] for n in SMALL]
    small_shapes = [g.shape for g in small_full]
    gs_all = all_gather8(_pack(small_full), "gather_small_grads")
    gs_sum = sum_devices(gs_all[:, None], "sum_small").reshape(-1)
    for n, g in zip(SMALL, _unpack(gs_sum, small_shapes)):
        if n in SMALL_SHARDED:
            w = a[n].shape[-1]
            g = lax.dynamic_slice_in_dim(g, chip * w, w, axis=g.ndim - 1)
        grads[n] = g
    assert SMALL[0] == 'ada_b'
    dmod_all = gs_all.reshape(N_DEV, -1)[:, :DEPTH * 6 * d].reshape(N_DEV, DEPTH, 6 * d)
    wcols = 6 * d // N_CHIP
    grads['ada_w'] = jnp.stack([
        mm(cond_all, lax.dynamic_slice_in_dim(dmod_all[:, l], chip * wcols, wcols, axis=1), ta=True, tm=2048, tn=512, name=f"ada_dw{l}")
        for l in range(DEPTH)])

    delta, new_m, new_v = {}, {}, {}
    for n in ['ada_w'] + BIG:
        sh = a[n].shape
        two_d = lambda t: t.reshape(-1, sh[-1])
        dl, nm, nv = adamw(two_d(a[n]), two_d(grads[n]), two_d(a['m_' + n]), two_d(a['v_' + n]), f"adamw_{n}")
        delta[n], new_m[n], new_v[n] = dl.reshape(sh), nm.reshape(sh), nv.reshape(sh)
    shapes = [a[n].shape for n in SMALL]
    dl, nm, nv = adamw(_pack([a[n] for n in SMALL]), _pack([grads[n] for n in SMALL]), _pack([a['m_' + n] for n in SMALL]),
                       _pack([a['v_' + n] for n in SMALL]), "adamw_small")
    for n, t1, t2, t3 in zip(SMALL, _unpack(dl.reshape(-1), shapes), _unpack(nm.reshape(-1), shapes), _unpack(nv.reshape(-1), shapes)):
        delta[n], new_m[n], new_v[n] = t1, t2, t3

    return (loss, grad_x, *[grads[n] for n in WEIGHTS], *[delta[n] for n in WEIGHTS], *[new_m[n] for n in WEIGHTS],
            *[new_v[n] for n in WEIGHTS])
```

```python
import functools
import math

import jax
import jax.numpy as jnp
from jax import lax
from jax.experimental import pallas as pl
from jax.experimental.pallas import tpu as pltpu

F32 = jnp.float32
BF16 = jnp.bfloat16
MESH = pl.DeviceIdType.MESH

D_MODEL = 2048
SEQ = 2048
DEPTH = 4
N_DEV = 8
N_CHIP = 4
BLK = 128
LANE = 128
V7X_VMEM_LIMIT = 56 * 1024 * 1024
NORM_EPS = 1e-6
ROPE_THETA = 10000.0
LRU_C = 8.0
D_FF = 5504
A_PATTERNS = ((128, 1), (512, 4), (2048, 16))
ADAM_LR, ADAM_B1, ADAM_B2, ADAM_EPS, ADAM_WD, ADAM_STEP = 0.001, 0.9, 0.999, 1e-08, 0.01, 10
NEG = -1e30


def _params(n_grid):
    return pltpu.CompilerParams(dimension_semantics=("arbitrary",) * n_grid, vmem_limit_bytes=V7X_VMEM_LIMIT)


def _pick(dim, pref):
    best = None
    for t in range(LANE, min(dim, pref) + 1, LANE):
        if dim % t == 0:
            best = t
    return best or dim


def _sigmoid(x):
    return 1.0 / (1.0 + jnp.exp(-x))


_GELU_C = math.sqrt(2.0 / math.pi)


def _gelu(x):
    t = jnp.tanh(_GELU_C * (x + 0.044715 * (x * x * x)))
    return 0.5 * x * (1.0 + t)


def _gelu_grad(x):
    t = jnp.tanh(_GELU_C * (x + 0.044715 * (x * x * x)))
    return 0.5 * (1.0 + t) + 0.5 * x * (1.0 - t * t) * (_GELU_C * (1.0 + 3.0 * 0.044715 * (x * x)))


def _softplus(x):
    return jnp.maximum(x, 0.0) + jnp.log(1.0 + jnp.exp(-jnp.abs(x)))


def _neg_expm1(x):
    series = -x * (1.0 + x * (0.5 + x * (1.0 / 6.0 + x * (1.0 / 24.0))))
    return jnp.where(x > -0.03, series, 1.0 - jnp.exp(x))


def _shift_down(x, k):
    if k == 0:
        return x
    row = lax.broadcasted_iota(jnp.int32, x.shape, 0)
    return jnp.where(row >= k, pltpu.roll(x, k, 0), 0.0)


def _shift_up(x, k):
    if k == 0:
        return x
    n = x.shape[0]
    row = lax.broadcasted_iota(jnp.int32, x.shape, 0)
    return jnp.where(row < n - k, pltpu.roll(x, n - k, 0), 0.0)


def _dot(a, b, ca=1, cb=0):
    return lax.dot_general(a.astype(BF16), b.astype(BF16), (((ca,), (cb,)), ((), ())), preferred_element_type=F32)


def _rope(x, cos, sin_signed, half):
    c = x.shape[1]
    reps = c // cos.shape[1]
    cos_c = jnp.tile(cos, (1, reps)) if reps > 1 else cos
    sin_c = jnp.tile(sin_signed, (1, reps)) if reps > 1 else sin_signed
    lane = lax.broadcasted_iota(jnp.int32, x.shape, 1)
    first = (lane % (2 * half)) < half
    partner = jnp.where(first, pltpu.roll(x, c - half, 1), pltpu.roll(x, half, 1))
    return x * cos_c + partner * sin_c


def _rope_t(dy, cos, sin_signed, half):
    c = dy.shape[1]
    reps = c // cos.shape[1]
    cos_c = jnp.tile(cos, (1, reps)) if reps > 1 else cos
    sin_c = jnp.tile(sin_signed, (1, reps)) if reps > 1 else sin_signed
    lane = lax.broadcasted_iota(jnp.int32, dy.shape, 1)
    first = (lane % (2 * half)) < half
    ys = dy * sin_c
    partner = jnp.where(first, pltpu.roll(ys, c - half, 1), pltpu.roll(ys, half, 1))
    return dy * cos_c + partner


def _tile_call(name, fn, grid, ins, outs, acc_axis=None):
    n_in = len(ins)
    accs = [o[4] for o in outs]

    def body(*refs):
        vals = fn(*[r[...] for r in refs[:n_in]])
        if not isinstance(vals, (tuple, list)):
            vals = (vals,)
        for r, v, acc in zip(refs[n_in:], vals, accs):
            if acc:
                first = pl.program_id(acc_axis) == 0

                @pl.when(first)
                def _():
                    r[...] = v.astype(r.dtype)

                @pl.when(jnp.logical_not(first))
                def _():
                    r[...] += v.astype(r.dtype)
            else:
                r[...] = v.astype(r.dtype)

    res = pl.pallas_call(
        body, name=name, grid=grid,
        in_specs=[pl.BlockSpec(b, im) for _, b, im in ins],
        out_specs=[pl.BlockSpec(o[2], o[3]) for o in outs],
        out_shape=[jax.ShapeDtypeStruct(o[0], o[1]) for o in outs],
        compiler_params=_params(len(grid)),
    )(*[a for a, _, _ in ins])
    return res


def _row_tile(cols, n_arrays, rows):
    budget = 24 * 1024 * 1024 // (2 * 4 * max(n_arrays, 1) * cols)
    t = 8
    while t * 2 <= budget and rows % (t * 2) == 0 and t * 2 <= 1024:
        t *= 2
    return t


def _rows_call(name, fn, rows, tiled, full, outs_tiled, outs_acc=()):
    cols = max([a.shape[1] for a in tiled] + [c for c, _ in outs_tiled])
    tt = _row_tile(cols, len(tiled) + len(outs_tiled), rows)
    ins = [(a, (tt, a.shape[1]), lambda i: (i, 0)) for a in tiled]
    ins += [(a, a.shape, (lambda nd: (lambda i: (0,) * nd))(a.ndim)) for a in full]
    outs = [((rows, c), dt, (tt, c), lambda i: (i, 0), False) for c, dt in outs_tiled]
    outs += [(s, dt, s, (lambda nd: (lambda i: (0,) * nd))(len(s)), True) for s, dt in outs_acc]
    return _tile_call(name, fn, (rows // tt,), ins, outs, acc_axis=0)


def mm(a, b, *, layer=None, ta=False, tb=False, out_dtype=F32, tm=None, tn=None, tk=None, name):
    m, k = (a.shape[1], a.shape[0]) if ta else a.shape
    b_dims = b.shape if layer is None else b.shape[1:]
    n = b_dims[0] if tb else b_dims[1]
    tm = tm or _pick(m, 1024)
    tn = tn or _pick(n, 1024)
    tk = tk or k
    assert m % tm == 0 and n % tn == 0 and k % tk == 0, (name, m, n, k, tm, tn, tk)
    nk = k // tk
    a_spec = pl.BlockSpec((tk, tm), lambda i, j, kk: (kk, i)) if ta else pl.BlockSpec((tm, tk), lambda i, j, kk: (i, kk))
    if layer is None:
        b_spec = pl.BlockSpec((tn, tk), lambda i, j, kk: (j, kk)) if tb else pl.BlockSpec((tk, tn), lambda i, j, kk: (kk, j))
    elif tb:
        b_spec = pl.BlockSpec((None, tn, tk), lambda i, j, kk: (layer, j, kk))
    else:
        b_spec = pl.BlockSpec((None, tk, tn), lambda i, j, kk: (layer, kk, j))
    ca, cb = (0 if ta else 1), (1 if tb else 0)

    def body(a_ref, b_ref, o_ref, *scratch):
        p = _dot(a_ref[...], b_ref[...], ca, cb)
        if nk == 1:
            o_ref[...] = p.astype(o_ref.dtype)
        else:
            acc = scratch[0]
            kk = pl.program_id(2)

            @pl.when(kk == 0)
            def _():
                acc[...] = p

            @pl.when(kk > 0)
            def _():
                acc[...] += p

            @pl.when(kk == nk - 1)
            def _():
                o_ref[...] = acc[...].astype(o_ref.dtype)

    return pl.pallas_call(
        body, name=name, grid=(m // tm, n // tn, nk),
        in_specs=[a_spec, b_spec],
        out_specs=pl.BlockSpec((tm, tn), lambda i, j, kk: (i, j)),
        out_shape=jax.ShapeDtypeStruct((m, n), out_dtype),
        scratch_shapes=[pltpu.VMEM((tm, tn), F32)] if nk > 1 else [],
        compiler_params=_params(3),
    )(a, b)


def _rstd(x):
    return lax.rsqrt(jnp.mean(x * x, axis=-1, keepdims=True) + NORM_EPS)


def norm_mod(x, gamma, sc, sh, name):
    def fn(x, gamma, sc, sh):
        return (x * _rstd(x)) * gamma * (1.0 + sc) + sh
    return _rows_call(name, fn, x.shape[0], [x], [gamma, sc, sh], [(x.shape[1], BF16)])[0]


def resid_norm_mod(x, y, g, gamma, sc, sh, name):
    def fn(x, y, g, gamma, sc, sh):
        x2 = x + g * y
        return x2, (x2 * _rstd(x2)) * gamma * (1.0 + sc) + sh
    return _rows_call(name, fn, x.shape[0], [x, y], [g, gamma, sc, sh], [(x.shape[1], F32), (x.shape[1], BF16)])


def resid_add(x, y, g, name):
    def fn(x, y, g):
        return x + g * y
    return _rows_call(name, fn, x.shape[0], [x, y], [g], [(x.shape[1], F32)])[0]


def resid_bwd(dxo, f, g, name):
    def fn(dxo, f, g):
        return dxo * g, jnp.sum(dxo * f, axis=0, keepdims=True)
    d = dxo.shape[1]
    return _rows_call(name, fn, dxo.shape[0], [dxo, f], [g], [(d, BF16)], [((1, d), F32)])


def norm_mod_bwd(dh, x, dres, gamma, sc, name):
    def fn(dh, x, dres, gamma, sc):
        rstd = _rstd(x)
        xhat = x * rstd
        dxhat = dh * (gamma * (1.0 + sc))
        dx = rstd * (dxhat - xhat * jnp.mean(dxhat * xhat, axis=-1, keepdims=True))
        dhx = dh * xhat
        return (dres + dx, jnp.sum(dh, axis=0, keepdims=True), jnp.sum(dhx * gamma, axis=0, keepdims=True),
                jnp.sum(dhx * (1.0 + sc), axis=0, keepdims=True))
    d = x.shape[1]
    return _rows_call(name, fn, x.shape[0], [dh, x, dres], [gamma, sc], [(d, F32)], [((1, d), F32)] * 3)


def final_loss(x, gamma, target, name):
    d = x.shape[1]

    def fn(x, target, gamma):
        rstd = _rstd(x)
        xhat = x * rstd
        e = xhat * gamma - target
        part = 0.5 * jnp.sum(jnp.sum(e * e, axis=-1, keepdims=True) / d, axis=0, keepdims=True)
        dy = e / d
        dxhat = dy * gamma
        dx = rstd * (dxhat - xhat * jnp.mean(dxhat * xhat, axis=-1, keepdims=True))
        return dx, jnp.broadcast_to(part, (8, LANE)), jnp.sum(dy * xhat, axis=0, keepdims=True)
    return _rows_call(name, fn, x.shape[0], [x, target], [gamma], [(d, F32)], [((8, LANE), F32), ((1, d), F32)])


def _conv3(u, w, b):
    return b + w[2:3] * u + w[1:2] * _shift_down(u, 1) + w[0:1] * _shift_down(u, 2)


def _ffn_blocks(u, cw, cb):
    t = u.shape[0]
    nb = D_FF // LANE
    return [(u, (t, LANE), lambda j: (0, j)), (u, (t, LANE), lambda j: (0, j + nb)),
            (cw, (3, LANE), lambda j: (0, j)), (cw, (3, LANE), lambda j: (0, j + nb)),
            (cb, (1, LANE), lambda j: (0, j)), (cb, (1, LANE), lambda j: (0, j + nb))]


def ffn_act(u, cw, cb, name):
    t = u.shape[0]

    def fn(ug, uv, wg, wv, bg, bv):
        return _gelu(_conv3(ug, wg, bg)) * _conv3(uv, wv, bv)
    return _tile_call(name, fn, (D_FF // LANE,), _ffn_blocks(u, cw, cb), [((t, D_FF), BF16, (t, LANE), lambda j: (0, j), False)])[0]


def ffn_act_bwd(u, dact, cw, cb, name):
    t = u.shape[0]

    def conv_t(duc, us, w):
        du = w[2:3] * duc + w[1:2] * _shift_up(duc, 1) + w[0:1] * _shift_up(duc, 2)
        dw = jnp.concatenate([jnp.sum(duc * _shift_down(us, 2), axis=0, keepdims=True),
                              jnp.sum(duc * _shift_down(us, 1), axis=0, keepdims=True),
                              jnp.sum(duc * us, axis=0, keepdims=True)], axis=0)
        return du, dw, jnp.sum(duc, axis=0, keepdims=True)

    def fn(ug, uv, wg, wv, bg, bv, da):
        g = _conv3(ug, wg, bg)
        v = _conv3(uv, wv, bv)
        du_g, dw_g, db_g = conv_t(da * v * _gelu_grad(g), ug, wg)
        du_v, dw_v, db_v = conv_t(da * _gelu(g), uv, wv)
        return du_g, du_v, dw_g, dw_v, db_g, db_v
    ins = _ffn_blocks(u, cw, cb) + [(dact, (t, LANE), lambda j: (0, j))]
    col = lambda rows, dt: ((rows, D_FF), dt, (rows, LANE), lambda j: (0, j), False)
    du_g, du_v, dw_g, dw_v, db_g, db_v = _tile_call(name, fn, (D_FF // LANE,), ins,
                                                    [col(t, BF16), col(t, BF16), col(3, F32), col(3, F32), col(1, F32), col(1, F32)])
    return jnp.concatenate([du_g, du_v], axis=1), jnp.concatenate([dw_g, dw_v], axis=1), jnp.concatenate([db_g, db_v], axis=1)


def attn_fwd(q, k, v, max_dist, scale, name):
    n, r, l, dh = q.shape
    nb = l // BLK
    m_rows = r * BLK

    def body(q_ref, kc_ref, kp_ref, vc_ref, vp_ref, o_ref, lse_ref):
        b = pl.program_id(1)
        qv = q_ref[...].reshape(m_rows, dh)
        s_c = _dot(qv, kc_ref[...], 1, 1) * scale
        s_p = _dot(qv, kp_ref[...], 1, 1) * scale
        qi = lax.broadcasted_iota(jnp.int32, (m_rows, BLK), 0) % BLK
        kj = lax.broadcasted_iota(jnp.int32, (m_rows, BLK), 1)
        s_c = jnp.where(kj <= qi, s_c, NEG)
        s_p = jnp.where((kj >= qi + (BLK - max_dist)) & (b > 0), s_p, NEG)
        mx = jnp.maximum(jnp.max(s_c, axis=1, keepdims=True), jnp.max(s_p, axis=1, keepdims=True))
        p_c = jnp.exp(s_c - mx)
        p_p = jnp.exp(s_p - mx)
        den = jnp.sum(p_c, axis=1, keepdims=True) + jnp.sum(p_p, axis=1, keepdims=True)
        o = (_dot(p_c, vc_ref[...]) + _dot(p_p, vp_ref[...])) / den
        o_ref[...] = o.reshape(r, BLK, dh)
        lse_ref[...] = jnp.broadcast_to(mx + jnp.log(den), (m_rows, dh)).reshape(r, BLK, dh)

    qspec = pl.BlockSpec((None, r, BLK, dh), lambda i, b: (i, 0, b, 0))
    cur = pl.BlockSpec((None, BLK, dh), lambda i, b: (i, b, 0))
    prev = pl.BlockSpec((None, BLK, dh), lambda i, b: (i, jnp.maximum(b - 1, 0), 0))
    return pl.pallas_call(
        body, name=name, grid=(n, nb),
        in_specs=[qspec, cur, prev, cur, prev],
        out_specs=[qspec, qspec],
        out_shape=[jax.ShapeDtypeStruct((n, r, l, dh), F32)] * 2,
        compiler_params=_params(2),
    )(q, k, k, v, v)


def attn_bwd(q, k, v, do, lse, dvec, max_dist, scale, name):
    n, r, l, dh = q.shape
    nb = l // BLK
    m_rows = r * BLK

    def body(qc_ref, qn_ref, kc_ref, kp_ref, vc_ref, vp_ref, doc_ref, don_ref, lc_ref, ln_ref, dc_ref, dn_ref,
             dq_ref, dk_ref, dv_ref):
        b = pl.program_id(1)
        qi = lax.broadcasted_iota(jnp.int32, (m_rows, BLK), 0) % BLK
        kj = lax.broadcasted_iota(jnp.int32, (m_rows, BLK), 1)
        m_cur = kj <= qi
        m_prev = kj >= qi + (BLK - max_dist)

        def pair(q_ref, do_ref, l_ref, d_ref, k_ref, v_ref, mask):
            qv = q_ref[...].reshape(m_rows, dh)
            dov = do_ref[...].reshape(m_rows, dh)
            lrow = l_ref[...].reshape(m_rows, dh)[:, 0:1]
            drow = d_ref[...].reshape(m_rows, dh)[:, 0:1]
            s = _dot(qv, k_ref[...], 1, 1) * scale
            p = jnp.where(mask, jnp.exp(jnp.where(mask, s, NEG) - lrow), 0.0)
            dp = _dot(dov, v_ref[...], 1, 1)
            ds = p * (dp - drow) * scale
            return qv, dov, p, ds

        q_a, do_a, p_a, ds_a = pair(qc_ref, doc_ref, lc_ref, dc_ref, kc_ref, vc_ref, m_cur)
        _, _, _, ds_b = pair(qc_ref, doc_ref, lc_ref, dc_ref, kp_ref, vp_ref, m_prev & (b > 0))
        q_c, do_c, p_c, ds_c = pair(qn_ref, don_ref, ln_ref, dn_ref, kc_ref, vc_ref, m_prev & (b < nb - 1))
        dq = _dot(ds_a, kc_ref[...]) + _dot(ds_b, kp_ref[...])
        dq_ref[...] = dq.reshape(r, BLK, dh)
        dk_ref[...] = _dot(ds_a, q_a, 0, 0) + _dot(ds_c, q_c, 0, 0)
        dv_ref[...] = _dot(p_a, do_a, 0, 0) + _dot(p_c, do_c, 0, 0)

    qcur = pl.BlockSpec((None, r, BLK, dh), lambda i, b: (i, 0, b, 0))
    qnext = pl.BlockSpec((None, r, BLK, dh), lambda i, b: (i, 0, jnp.minimum(b + 1, nb - 1), 0))
    cur = pl.BlockSpec((None, BLK, dh), lambda i, b: (i, b, 0))
    prev = pl.BlockSpec((None, BLK, dh), lambda i, b: (i, jnp.maximum(b - 1, 0), 0))
    return pl.pallas_call(
        body, name=name, grid=(n, nb),
        in_specs=[qcur, qnext, cur, prev, cur, prev, qcur, qnext, qcur, qnext, qcur, qnext],
        out_specs=[qcur, cur, cur],
        out_shape=[jax.ShapeDtypeStruct((n, r, l, dh), F32), jax.ShapeDtypeStruct((n, l, dh), F32),
                   jax.ShapeDtypeStruct((n, l, dh), F32)],
        compiler_params=_params(2),
    )(q, q, k, k, v, v, do, do, lse, lse, dvec, dvec)


def _scan_rows(t_len, step, init, reverse=False):
    n_chunks = t_len // 8

    def chunk(ci, carry):
        c = (n_chunks - 1 - ci) if reverse else ci
        base = pl.multiple_of(c * 8, 8)
        order = range(7, -1, -1) if reverse else range(8)
        return step(base, order, carry)
    return lax.fori_loop(0, n_chunks, chunk, init)


def _put_row(acc, i, row):
    rid = lax.broadcasted_iota(jnp.int32, acc.shape, 0)
    return jnp.where(rid == i, row, acc)


def _real_scan(a_ref, b_ref, h_ref, t_len, reverse=False):
    c = a_ref.shape[1]

    def step(base, order, h):
        a8 = a_ref[pl.ds(base, 8), :]
        b8 = b_ref[pl.ds(base, 8), :]
        out = jnp.zeros((8, c), F32)
        for i in order:
            h = a8[i:i + 1, :] * h + b8[i:i + 1, :]
            out = _put_row(out, i, h)
        h_ref[pl.ds(base, 8), :] = out
        return h
    _scan_rows(t_len, step, jnp.zeros((1, c), F32), reverse)


def _complex_scan(ar, ai, br_ref, bi_ref, sr_ref, si_ref, t_len, reverse=False):
    c = br_ref.shape[1]

    def step(base, order, carry):
        sr, si = carry
        br8 = br_ref[pl.ds(base, 8), :]
        bi8 = bi_ref[pl.ds(base, 8), :]
        outr = jnp.zeros((8, c), F32)
        outi = jnp.zeros((8, c), F32)
        for i in order:
            nr = ar * sr - ai * si + br8[i:i + 1, :]
            ni = ar * si + ai * sr + bi8[i:i + 1, :]
            sr, si = nr, ni
            outr = _put_row(outr, i, sr)
            outi = _put_row(outi, i, si)
        sr_ref[pl.ds(base, 8), :] = outr
        si_ref[pl.ds(base, 8), :] = outi
        return sr, si
    _scan_rows(t_len, step, (jnp.zeros((1, c), F32), jnp.zeros((1, c), F32)), reverse)


def _rglru_pre(xb, cw, cb, gaw, gab, gxw, gxb, lam):
    xc = cb + cw[3:4] * xb + cw[2:3] * _shift_down(xb, 1) + cw[1:2] * _shift_down(xb, 2) + cw[0:1] * _shift_down(xb, 3)
    r = _sigmoid(_dot(xc, gaw) + gab)
    ig = _sigmoid(_dot(xc, gxw) + gxb)
    sp = _softplus(-lam)
    log_a = -LRU_C * r * sp
    a = jnp.exp(log_a)
    mult = jnp.sqrt(_neg_expm1(2.0 * log_a))
    return xc, r, ig, sp, a, mult


def rglru_fwd(proj, cw, cb, gaw, gab, gxw, gxb, lam, xb_col, yb_col, name):
    t = proj.shape[0]
    nh = cw.shape[1] // LANE

    def body(xb_ref, yb_ref, cw_ref, cb_ref, gaw_ref, gab_ref, gxw_ref, gxb_ref, lam_ref, out_ref, h_ref, a_scr, b_scr):
        xc, r, ig, sp, a, mult = _rglru_pre(xb_ref[...], cw_ref[...], cb_ref[...], gaw_ref[...], gab_ref[...],
                                            gxw_ref[...], gxb_ref[...], lam_ref[...])
        a_scr[...] = a
        b_scr[...] = mult * (ig * xc)
        _real_scan(a_scr, b_scr, h_ref, t)
        out_ref[...] = (h_ref[...] * _gelu(yb_ref[...])).astype(out_ref.dtype)

    col = lambda off: pl.BlockSpec((t, LANE), lambda h: (0, off + h))
    vec = lambda rows: pl.BlockSpec((rows, LANE), lambda h: (0, h))
    wsp = pl.BlockSpec((None, LANE, LANE), lambda h: (h, 0, 0))
    return pl.pallas_call(
        body, name=name, grid=(nh,),
        in_specs=[col(xb_col), col(yb_col), vec(4), vec(1), wsp, vec(1), wsp, vec(1), vec(1)],
        out_specs=[col(0), col(0)],
        out_shape=[jax.ShapeDtypeStruct((t, nh * LANE), BF16), jax.ShapeDtypeStruct((t, nh * LANE), F32)],
        scratch_shapes=[pltpu.VMEM((t, LANE), F32)] * 2,
        compiler_params=_params(1),
    )(proj, proj, cw, cb, gaw, gab, gxw, gxb, lam)


def rglru_bwd(proj, hs, dlru, dlru_col, cw, cb, gaw, gab, gxw, gxb, lam, xb_col, yb_col, name):
    t = proj.shape[0]
    nh = cw.shape[1] // LANE

    def body(xb_ref, yb_ref, h_ref, dl_ref, cw_ref, cb_ref, gaw_ref, gab_ref, gxw_ref, gxb_ref, lam_ref,
             dxb_ref, dyb_ref, dcw_ref, dcb_ref, dgaw_ref, dgab_ref, dgxw_ref, dgxb_ref, dlam_ref, an_scr, dh_scr, gh_scr):
        xb = xb_ref[...]
        yb = yb_ref[...]
        cwv = cw_ref[...]
        lam = lam_ref[...]
        xc, r, ig, sp, a, mult = _rglru_pre(xb, cwv, cb_ref[...], gaw_ref[...], gab_ref[...], gxw_ref[...], gxb_ref[...], lam)
        h = h_ref[...]
        dl = dl_ref[...]
        dyb_ref[...] = (dl * h * _gelu_grad(yb)).astype(dyb_ref.dtype)
        dh_scr[...] = dl * _gelu(yb)
        an_scr[...] = _shift_up(a, 1)
        _real_scan(an_scr, dh_scr, gh_scr, t, reverse=True)
        gh = gh_scr[...]
        da = gh * _shift_down(h, 1)
        dmult = gh * ig * xc
        dig = gh * mult * xc
        dxc = gh * mult * ig
        dla = (da - dmult * a / mult) * a
        dr = dla * (-LRU_C * sp)
        dsp = jnp.sum(dla * (-LRU_C * r), axis=0, keepdims=True)
        dlam_ref[...] = dsp * (-_sigmoid(-lam))
        dpr = dr * r * (1.0 - r)
        dpi = dig * ig * (1.0 - ig)
        dgab_ref[...] = jnp.sum(dpr, axis=0, keepdims=True)
        dgxb_ref[...] = jnp.sum(dpi, axis=0, keepdims=True)
        dgaw_ref[...] = _dot(xc, dpr, 0, 0)
        dgxw_ref[...] = _dot(xc, dpi, 0, 0)
        dxc = dxc + _dot(dpr, gaw_ref[...], 1, 1) + _dot(dpi, gxw_ref[...], 1, 1)
        dxb = cwv[3:4] * dxc + cwv[2:3] * _shift_up(dxc, 1) + cwv[1:2] * _shift_up(dxc, 2) + cwv[0:1] * _shift_up(dxc, 3)
        dxb_ref[...] = dxb.astype(dxb_ref.dtype)
        dcw_ref[...] = jnp.concatenate([jnp.sum(dxc * _shift_down(xb, 3 - i), axis=0, keepdims=True) for i in range(4)], axis=0)
        dcb_ref[...] = jnp.sum(dxc, axis=0, keepdims=True)

    col = lambda off: pl.BlockSpec((t, LANE), lambda h: (0, off + h))
    vec = lambda rows: pl.BlockSpec((rows, LANE), lambda h: (0, h))
    wsp = pl.BlockSpec((None, LANE, LANE), lambda h: (h, 0, 0))
    w = nh * LANE
    sds = jax.ShapeDtypeStruct
    return pl.pallas_call(
        body, name=name, grid=(nh,),
        in_specs=[col(xb_col), col(yb_col), col(0), col(dlru_col), vec(4), vec(1), wsp, vec(1), wsp, vec(1), vec(1)],
        out_specs=[col(0), col(0), vec(4), vec(1), wsp, vec(1), wsp, vec(1), vec(1)],
        out_shape=[sds((t, w), BF16), sds((t, w), BF16), sds((4, w), F32), sds((1, w), F32), sds((nh, LANE, LANE), F32),
                   sds((1, w), F32), sds((nh, LANE, LANE), F32), sds((1, w), F32), sds((1, w), F32)],
        scratch_shapes=[pltpu.VMEM((t, LANE), F32)] * 3,
        compiler_params=_params(1),
    )(proj, proj, hs, dlru, cw, cb, gaw, gab, gxw, gxb, lam)


S5_CHUNKS = 8
S5_STATES = 512


def s5_fwd(proj, u_col, bre, bim, are, aim, cre, cim, dsk, name):
    t = proj.shape[0]

    def body(u_ref, bre_ref, bim_ref, are_ref, aim_ref, cre_ref, cim_ref, d_ref, y_ref, sr_ref, si_ref, br_scr, bi_scr):
        u = u_ref[...]
        br_scr[...] = _dot(u, bre_ref[...])
        bi_scr[...] = _dot(u, bim_ref[...])
        _complex_scan(are_ref[...], aim_ref[...], br_scr, bi_scr, sr_ref, si_ref, t)
        y_ref[...] = _dot(sr_ref[...], cre_ref[...]) - _dot(si_ref[...], cim_ref[...]) + d_ref[...] * u

    ucol = pl.BlockSpec((t, LANE), lambda c: (0, u_col + c))
    ycol = pl.BlockSpec((t, LANE), lambda c: (0, c))
    scol = pl.BlockSpec((t, S5_STATES), lambda c: (0, c))
    bsp = pl.BlockSpec((None, LANE, S5_STATES), lambda c: (c, 0, 0))
    csp = pl.BlockSpec((None, S5_STATES, LANE), lambda c: (c, 0, 0))
    asp = pl.BlockSpec((1, S5_STATES), lambda c: (0, c))
    dsp = pl.BlockSpec((1, LANE), lambda c: (0, c))
    sds = jax.ShapeDtypeStruct
    return pl.pallas_call(
        body, name=name, grid=(S5_CHUNKS,),
        in_specs=[ucol, bsp, bsp, asp, asp, csp, csp, dsp],
        out_specs=[ycol, scol, scol],
        out_shape=[sds((t, S5_CHUNKS * LANE), F32), sds((t, S5_CHUNKS * S5_STATES), F32), sds((t, S5_CHUNKS * S5_STATES), F32)],
        scratch_shapes=[pltpu.VMEM((t, S5_STATES), F32)] * 2,
        compiler_params=_params(1),
    )(proj, bre, bim, are, aim, cre, cim, dsk)


def s5_bwd(proj, u_col, dy, sr, si, bre, bim, are, aim, cre, cim, dsk, name):
    t = proj.shape[0]
    half = S5_STATES // 2

    def body(u_ref, dy_ref, sr_ref, si_ref, bre_ref, bim_ref, are_ref, aim_ref, cre_ref, cim_ref, d_ref,
             du_ref, dbre_ref, dbim_ref, dare_ref, daim_ref, dcre_ref, dcim_ref, dd_ref, dsr_scr, dsi_scr, gr_scr, gi_scr):
        hh = pl.program_id(1)
        u = u_ref[...]
        dy = dy_ref[...]
        dsr_scr[...] = _dot(dy, cre_ref[...], 1, 1)
        dsi_scr[...] = -_dot(dy, cim_ref[...], 1, 1)
        _complex_scan(are_ref[...], -aim_ref[...], dsr_scr, dsi_scr, gr_scr, gi_scr, t, reverse=True)
        gr = gr_scr[...]
        gi = gi_scr[...]
        spr = _shift_down(sr_ref[...], 1)
        spi = _shift_down(si_ref[...], 1)
        dare_ref[...] = jnp.sum(gr * spr + gi * spi, axis=0, keepdims=True)
        daim_ref[...] = jnp.sum(gi * spr - gr * spi, axis=0, keepdims=True)
        du = _dot(gr, bre_ref[...], 1, 1) + _dot(gi, bim_ref[...], 1, 1)

        @pl.when(hh == 0)
        def _():
            du_ref[...] = du + d_ref[...] * dy

        @pl.when(hh > 0)
        def _():
            du_ref[...] += du

        dbre_ref[...] = _dot(u, gr, 0, 0)
        dbim_ref[...] = _dot(u, gi, 0, 0)
        dcre_ref[...] = _dot(sr_ref[...], dy, 0, 0)
        dcim_ref[...] = -_dot(si_ref[...], dy, 0, 0)
        dd_ref[...] = jnp.sum(dy * u, axis=0, keepdims=True)

    ucol = pl.BlockSpec((t, LANE), lambda c, h: (0, u_col + c))
    ycol = pl.BlockSpec((t, LANE), lambda c, h: (0, c))
    scol = pl.BlockSpec((t, half), lambda c, h: (0, 2 * c + h))
    bsp = pl.BlockSpec((None, LANE, half), lambda c, h: (c, 0, h))
    csp = pl.BlockSpec((None, half, LANE), lambda c, h: (c, h, 0))
    asp = pl.BlockSpec((1, half), lambda c, h: (0, 2 * c + h))
    dsp = pl.BlockSpec((1, LANE), lambda c, h: (0, c))
    sds = jax.ShapeDtypeStruct
    return pl.pallas_call(
        body, name=name, grid=(S5_CHUNKS, 2),
        in_specs=[ucol, ycol, scol, scol, bsp, bsp, asp, asp, csp, csp, dsp],
        out_specs=[ycol, bsp, bsp, asp, asp, csp, csp, dsp],
        out_shape=[sds((t, S5_CHUNKS * LANE), F32), sds((S5_CHUNKS, LANE, S5_STATES), F32), sds((S5_CHUNKS, LANE, S5_STATES), F32),
                   sds((1, S5_CHUNKS * S5_STATES), F32), sds((1, S5_CHUNKS * S5_STATES), F32),
                   sds((S5_CHUNKS, S5_STATES, LANE), F32), sds((S5_CHUNKS, S5_STATES, LANE), F32), sds((1, S5_CHUNKS * LANE), F32)],
        scratch_shapes=[pltpu.VMEM((t, half), F32)] * 4,
        compiler_params=_params(2),
    )(proj, dy, sr, si, bre, bim, are, aim, cre, cim, dsk)


def s5_prep(a_re, a_im, b_re, b_im, c_re, c_im, log_dt):
    lam = lax.complex(a_re, a_im)
    dt = jnp.exp(log_dt)[:, None]
    a_bar = jnp.exp(lam * dt)
    b_bar = ((a_bar - 1.0) / lam)[..., None] * lax.complex(b_re, b_im)
    g, p, cg = b_re.shape
    eye = jnp.eye(8, dtype=F32)

    def in_map(m):
        m = m.reshape(g // 8, 8, p, cg)
        return jnp.einsum("ab,kapc->kacbp", eye, m).reshape(g // 8, 8 * cg, 8 * p)

    def out_map(m):
        m = m.reshape(g // 8, 8, cg, p)
        return jnp.einsum("ab,kacp->kapbc", eye, m).reshape(g // 8, 8 * p, 8 * cg)

    return (jnp.real(a_bar).reshape(1, g * p), jnp.imag(a_bar).reshape(1, g * p), in_map(jnp.real(b_bar)), in_map(jnp.imag(b_bar)),
            out_map(c_re), out_map(c_im))


def rope_tables(pos, half):
    inv = ROPE_THETA ** (-jnp.arange(half, dtype=F32) / half)
    ang = pos.astype(F32)[:, None] * inv
    cos, sin = jnp.cos(ang), jnp.sin(ang)
    reps = max(LANE // (2 * half), 1)
    return jnp.tile(jnp.concatenate([cos, cos], axis=1), (1, reps)), jnp.tile(jnp.concatenate([-sin, sin], axis=1), (1, reps))


A_W = 1024
ROW_T = 256


def _tiled(a, width, col):
    return (a, (ROW_T, width), lambda i: (i, col))


def _out_tiled(t, width, dtype):
    return ((t, width), dtype, (ROW_T, width), lambda i: (i, 0), False)


def qkv_rope_even(proj, cos, sin, name):
    t = proj.shape[0]

    def fn(q, k, v, cos, sin):
        return _rope(q, cos, sin, 64), _rope(k, cos, sin, 64), v
    ins = [_tiled(proj, A_W, 0), _tiled(proj, A_W, 1), _tiled(proj, A_W, 2), _tiled(cos, LANE, 0), _tiled(sin, LANE, 0)]
    return _tile_call(name, fn, (t // ROW_T,), ins, [_out_tiled(t, A_W, BF16)] * 3)


def merge3(o, lse, name):
    t = o[0].shape[0]

    def fn(o1, o2, o3, l1, l2, l3):
        mx = jnp.maximum(jnp.maximum(l1, l2), l3)
        e1, e2, e3 = jnp.exp(l1 - mx), jnp.exp(l2 - mx), jnp.exp(l3 - mx)
        den = e1 + e2 + e3
        out = (e1 * o1 + e2 * o2 + e3 * o3) / den
        return out, out, mx + jnp.log(den)
    ins = [_tiled(a, A_W, 0) for a in list(o) + list(lse)]
    return _tile_call(name, fn, (t // ROW_T,), ins, [_out_tiled(t, A_W, BF16), _out_tiled(t, A_W, F32), _out_tiled(t, A_W, F32)])


def _segsum_bcast(x, width):
    parts = []
    for h in range(x.shape[1] // width):
        s = jnp.sum(x[:, h * width:(h + 1) * width], axis=1, keepdims=True)
        parts.append(jnp.broadcast_to(s, (x.shape[0], width)))
    return jnp.concatenate(parts, axis=1)


def even_attn_prep(dmix, attn, name):
    t = attn.shape[0]

    def fn(dout, attn):
        return dout, _segsum_bcast(dout * attn, LANE)
    ins = [_tiled(dmix, A_W, 0), _tiled(attn, A_W, 0)]
    return _tile_call(name, fn, (t // ROW_T,), ins, [_out_tiled(t, A_W, BF16), _out_tiled(t, A_W, F32)])


def even_dproj(dq, dk, dv, dxb, dyb, cos, sin, name):
    t = dxb.shape[0]

    def fn(q1, q2, q3, k1, k2, k3, v1, v2, v3, dxb, dyb, cos, sin):
        return jnp.concatenate([_rope_t(q1 + q2 + q3, cos, sin, 64).astype(BF16), _rope_t(k1 + k2 + k3, cos, sin, 64).astype(BF16),
                                (v1 + v2 + v3).astype(BF16), dxb, dyb], axis=1)
    ins = [_tiled(a, A_W, 0) for a in list(dq) + list(dk) + list(dv) + [dxb, dyb]] + [_tiled(cos, LANE, 0), _tiled(sin, LANE, 0)]
    return _tile_call(name, fn, (t // ROW_T,), ins, [_out_tiled(t, 5 * A_W, BF16)])[0]


def perm(x, d):
    t = x.shape[0]
    return x.reshape(t // d, d, 8, LANE).transpose(1, 2, 0, 3).reshape(d * 8, t // d, LANE)


def unperm(xp, d):
    n, l, _ = xp.shape
    return xp.reshape(d, 8, l, LANE).transpose(2, 0, 1, 3).reshape(l * d, 8 * LANE)


def qkv_rope_odd(proj, cos, sin, name):
    t = proj.shape[0]

    def fn(q, k, v, cos, sin):
        return _rope(q, cos, sin, 32), _rope(k, cos, sin, 32), v
    ins = [_tiled(proj, A_W, 0), _tiled(proj, LANE, 8), _tiled(proj, LANE, 9), _tiled(cos, LANE, 0), _tiled(sin, LANE, 0)]
    return _tile_call(name, fn, (t // ROW_T,), ins, [_out_tiled(t, A_W, BF16), _out_tiled(t, LANE, BF16), _out_tiled(t, LANE, BF16)])


HEAD_ROWS = 1024


def _head_blocks(a):
    return (a, (None, HEAD_ROWS, a.shape[2]), lambda h, i: (h, i, 0))


def sink_fwd(o, lse, sink_b, name):
    nh, t, dh = o.shape

    def fn(o, lse, s):
        return o * _sigmoid(lse - s)
    ins = [_head_blocks(o), _head_blocks(lse), (sink_b, (None, 1, dh), lambda h, i: (h, 0, 0))]
    return _tile_call(name, fn, (nh, t // HEAD_ROWS), ins, [((nh, t, dh), BF16, (None, HEAD_ROWS, dh), lambda h, i: (h, i, 0), False)])[0]


def sink_bwd(dof, o, lse, sink_b, name):
    nh, t, dh = o.shape

    def fn(dof, o, lse, s):
        keep = _sigmoid(lse - s)
        dk = jnp.sum(dof * o, axis=1, keepdims=True)
        dlse = dk * keep * (1.0 - keep)
        return dof * keep, dk * keep * keep, -jnp.sum(dlse, axis=0, keepdims=True)
    ins = [_head_blocks(dof), _head_blocks(o), _head_blocks(lse), (sink_b, (None, 1, dh), lambda h, i: (h, 0, 0))]
    outs = [((nh, t, dh), BF16, (None, HEAD_ROWS, dh), lambda h, i: (h, i, 0), False),
            ((nh, t, dh), F32, (None, HEAD_ROWS, dh), lambda h, i: (h, i, 0), False),
            ((nh, 1, dh), F32, (None, 1, dh), lambda h, i: (h, 0, 0), True)]
    return _tile_call(name, fn, (nh, t // HEAD_ROWS), ins, outs, acc_axis=1)


def odd_dproj(dq, dk, dv, du, cos, sin, name):
    t = dq.shape[0]

    def fn(dq, dk, dv, du, cos, sin):
        return jnp.concatenate([_rope_t(dq, cos, sin, 32), _rope_t(dk, cos, sin, 32), dv, du], axis=1)
    ins = [_tiled(dq, A_W, 0), _tiled(dk, LANE, 0), _tiled(dv, LANE, 0), _tiled(du, A_W, 0), _tiled(cos, LANE, 0), _tiled(sin, LANE, 0)]
    return _tile_call(name, fn, (t // ROW_T,), ins, [_out_tiled(t, 2 * A_W + 2 * LANE, BF16)])[0]


def glu_z(y, name):
    return _rows_call(name, _gelu, y.shape[0], [y], [], [(y.shape[1], BF16)])[0]


def glu_out(y, gpre, b, name):
    def fn(y, gpre, b):
        return _gelu(y) * _sigmoid(gpre + b)
    return _rows_call(name, fn, y.shape[0], [y, gpre], [b], [(y.shape[1], BF16)])[0]


def glu_bwd_gate(dmix, y, gpre, b, name):
    t = y.shape[0]

    def fn(dout, y, gpre, b):
        gate = _sigmoid(gpre + b)
        dgp = dout * _gelu(y) * gate * (1.0 - gate)
        return dgp, jnp.sum(dgp, axis=0, keepdims=True)
    ins = [_tiled(dmix, A_W, 1), _tiled(y, A_W, 0), _tiled(gpre, A_W, 0), (b, (1, A_W), lambda i: (0, 0))]
    outs = [_out_tiled(t, A_W, BF16), ((1, A_W), F32, (1, A_W), lambda i: (0, 0), True)]
    return _tile_call(name, fn, (t // ROW_T,), ins, outs, acc_axis=0)


def glu_bwd_y(dmix, y, gpre, b, dz_mm, name):
    t = y.shape[0]

    def fn(dout, y, gpre, b, dz_mm):
        return (dout * _sigmoid(gpre + b) + dz_mm) * _gelu_grad(y)
    ins = [_tiled(dmix, A_W, 1), _tiled(y, A_W, 0), _tiled(gpre, A_W, 0), (b, (1, A_W), lambda i: (0, 0)), _tiled(dz_mm, A_W, 0)]
    return _tile_call(name, fn, (t // ROW_T,), ins, [_out_tiled(t, A_W, F32)])[0]


def adamw(w, g, m, v, name):
    def fn(w, g, m, v):
        m = ADAM_B1 * m + (1.0 - ADAM_B1) * g
        v = ADAM_B2 * v + (1.0 - ADAM_B2) * (g * g)
        m_hat = m / (1.0 - ADAM_B1 ** ADAM_STEP)
        v_hat = v / (1.0 - ADAM_B2 ** ADAM_STEP)
        return -ADAM_LR * (m_hat / (jnp.sqrt(v_hat) + ADAM_EPS) + ADAM_WD * w), m, v
    c = w.shape[1]
    return _rows_call(name, fn, w.shape[0], [w, g, m, v], [], [(c, F32)] * 3)


def _row_block(rows, row_bytes, limit):
    best = 16
    for t in range(16, rows + 1, 16):
        if rows % t == 0 and t * row_bytes <= limit:
            best = t
    return best


def _sum_in_order(v):
    s = v[0].astype(F32)
    for d in range(1, v.shape[0]):
        s = s + v[d].astype(F32)
    return s


def sum_devices(parts, name):
    nd, nl, r, c = parts.shape
    tr = _row_block(r, c * (parts.dtype.itemsize * nd + 4), 18 * 1024 * 1024)
    ins = [(parts, (nd, None, tr, c), lambda l, i: (0, l, i, 0))]
    outs = [((nl, r, c), F32, (None, tr, c), lambda l, i: (l, i, 0), False)]
    return _tile_call(name, _sum_in_order, (nl, r // tr), ins, outs)[0]


def silu_rows(c_all, name):
    def fn(c):
        return c * _sigmoid(c)
    return _rows_call(name, fn, c_all.shape[0], [c_all], [], [(c_all.shape[1], F32)])[0]


def _place():
    x, y, c = lax.axis_index("x"), lax.axis_index("y"), lax.axis_index("c")
    return x, y, c


ANY = pl.BlockSpec(memory_space=pl.ANY)


def all_gather8(v, name):
    r, cdim = v.shape

    def body(x_ref, out_ref, send_sems, recv_sems, local_sem):
        x, y, c = _place()
        me, sibling = (x, y, c), (x, y, 1 - c)
        chips = [(1 - x, y), (x, 1 - y), (1 - x, 1 - y)]

        def rows(px, py, pc):
            return out_ref.at[4 * px + 2 * py + pc]

        def copy(k, block, to, src=None):
            return pltpu.make_async_remote_copy(
                src_ref=rows(*block) if src is None else src, dst_ref=rows(*block),
                send_sem=send_sems.at[k], recv_sem=recv_sems.at[k], device_id=to, device_id_type=MESH)

        mine = pltpu.make_async_copy(x_ref, rows(*me), local_sem)
        mine.start()
        first = [copy(0, me, sibling, src=x_ref)]
        first += [copy(1 + j, me, (*chip, c), src=x_ref) for j, chip in enumerate(chips)]
        for cp in first:
            cp.start()
        passed = [copy(4 + j, (*chip, c), sibling) for j, chip in enumerate(chips)]
        for j, chip in enumerate(chips):
            copy(1 + j, (*chip, c), me).wait_recv()
            passed[j].start()
        copy(0, sibling, me).wait_recv()
        for j, chip in enumerate(chips):
            copy(4 + j, (*chip, 1 - c), me).wait_recv()
        for cp in first + passed:
            cp.wait_send()
        mine.wait()

    return pl.pallas_call(
        body, name=name, out_shape=jax.ShapeDtypeStruct((N_DEV, r, cdim), v.dtype),
        in_specs=[ANY], out_specs=ANY,
        scratch_shapes=[pltpu.SemaphoreType.DMA((7,)), pltpu.SemaphoreType.DMA((7,)), pltpu.SemaphoreType.DMA],
    )(v)


def gather_weights(shards, name):
    n = len(shards)

    def body(*refs):
        ins, outs = refs[:n], refs[n:2 * n]
        send_sems, recv_sems = refs[2 * n:]
        x, y, c = _place()
        sibling = (x, y, 1 - c)
        chips = [(1 - x, y), (x, 1 - y), (1 - x, 1 - y)]
        my_chip = 2 * x + y

        def half(t, chip_slot, start):
            hr = ins[t].shape[1] // 2
            return outs[t].at[chip_slot, :, pl.ds(start, hr), :]

        def copy(t, k, src, dst, to):
            return pltpu.make_async_remote_copy(src_ref=src, dst_ref=dst, send_sem=send_sems.at[6 * t + k],
                                                recv_sem=recv_sems.at[6 * t + k], device_id=to, device_id_type=MESH)

        def lows(t):
            hr = ins[t].shape[1] // 2
            return hr, pl.multiple_of(c * hr, 16), pl.multiple_of((1 - c) * hr, 16)

        started = []
        for t in range(n):
            hr, lo, _ = lows(t)
            for j, chip in enumerate(chips):
                cp = copy(t, j, ins[t].at[:, pl.ds(lo, hr), :], half(t, my_chip, lo), (*chip, c))
                cp.start()
                started.append(cp)
        for t in range(n):
            hr, lo, _ = lows(t)
            for j, (px, py) in enumerate(chips):
                slot = 2 * px + py
                copy(t, j, half(t, slot, lo), half(t, slot, lo), (px, py, c)).wait_recv()
                fwd = copy(t, 3 + j, half(t, slot, lo), half(t, slot, lo), sibling)
                fwd.start()
                started.append(fwd)
        for t in range(n):
            hr, _, lo_sib = lows(t)
            for j, (px, py) in enumerate(chips):
                slot = 2 * px + py
                copy(t, 3 + j, half(t, slot, lo_sib), half(t, slot, lo_sib), sibling).wait_recv()
        for cp in started:
            cp.wait_send()

    got = pl.pallas_call(
        body, name=name,
        out_shape=[jax.ShapeDtypeStruct((N_CHIP,) + s.shape, s.dtype) for s in shards],
        in_specs=[ANY] * n, out_specs=[ANY] * n,
        scratch_shapes=[pltpu.SemaphoreType.DMA((6 * n,)), pltpu.SemaphoreType.DMA((6 * n,))],
    )(*shards)
    my_chip = 2 * lax.axis_index("x") + lax.axis_index("y")
    return [lax.dynamic_update_index_in_dim(g, s, my_chip, 0) for g, s in zip(got, shards)]


def pair_swap(grads, name):
    n = len(grads)

    def body(*refs):
        ins, outs = refs[:n], refs[n:2 * n]
        send_sems, recv_sems = refs[2 * n:]
        x, y, c = _place()
        sibling = (x, y, 1 - c)
        sends = []
        for t in range(n):
            for q in range(N_CHIP):
                cp = pltpu.make_async_remote_copy(src_ref=ins[t].at[1 - c, q], dst_ref=outs[t].at[q], send_sem=send_sems.at[N_CHIP * t + q],
                                                  recv_sem=recv_sems.at[N_CHIP * t + q], device_id=sibling, device_id_type=MESH)
                cp.start()
                sends.append(cp)
        for cp in sends:
            cp.wait_recv()
            cp.wait_send()

    return pl.pallas_call(
        body, name=name,
        out_shape=[jax.ShapeDtypeStruct(g.shape[1:], g.dtype) for g in grads],
        in_specs=[ANY] * n, out_specs=[ANY] * n,
        scratch_shapes=[pltpu.SemaphoreType.DMA((N_CHIP * n,)), pltpu.SemaphoreType.DMA((N_CHIP * n,))],
    )(*grads)


def pair_sum(g5, from_sibling, core, name):
    _, nq, nl, hr, c = g5.shape
    tr = _row_block(hr, c * 2, 3 * 1024 * 1024)
    flag =jnp.broadcast_to(core.astype(F32), (8, LANE))

    def fn(g0, g1, r, flag):
        own = jnp.where(flag[0:1, 0:1] == 0.0, g0.astype(F32), g1.astype(F32))
        return own + r.astype(F32)
    ins = [(g5, (None, None, None, tr, c), lambda q, l, i: (0, q, l, i, 0)), (g5, (None, None, None, tr, c), lambda q, l, i: (1, q, l, i, 0)),
           (from_sibling, (None, None, tr, c), lambda q, l, i: (q, l, i, 0)), (flag, (8, LANE), lambda q, l, i: (0, 0))]
    outs = [((nq, nl, hr, c), BF16, (None, None, tr, c), lambda q, l, i: (q, l, i, 0), False)]
    return _tile_call(name, fn, (nq, nl, hr // tr), ins, outs)[0]


def exchange_chips(pairs, name):
    n = len(pairs)

    def body(*refs):
        ins, outs = refs[:n], refs[n:2 * n]
        send_sems, recv_sems = refs[2 * n:]
        x, y, c = _place()
        chips = [(1 - x, y), (x, 1 - y), (1 - x, 1 - y)]
        my_chip = 2 * x + y
        sends = []
        for t in range(n):
            for j, (px, py) in enumerate(chips):
                cp = pltpu.make_async_remote_copy(src_ref=ins[t].at[2 * px + py], dst_ref=outs[t].at[my_chip], send_sem=send_sems.at[3 * t + j],
                                                  recv_sem=recv_sems.at[3 * t + j], device_id=(px, py, c), device_id_type=MESH)
                cp.start()
                sends.append(cp)
        for t in range(n):
            for j, (px, py) in enumerate(chips):
                slot = outs[t].at[2 * px + py]
                pltpu.make_async_remote_copy(src_ref=slot, dst_ref=slot, send_sem=send_sems.at[3 * t + j], recv_sem=recv_sems.at[3 * t + j],
                                             device_id=(px, py, c), device_id_type=MESH).wait_recv()
        for cp in sends:
            cp.wait_send()

    got = pl.pallas_call(
        body, name=name,
        out_shape=[jax.ShapeDtypeStruct(p.shape, p.dtype) for p in pairs],
        in_specs=[ANY] * n, out_specs=[ANY] * n,
        scratch_shapes=[pltpu.SemaphoreType.DMA((3 * n,)), pltpu.SemaphoreType.DMA((3 * n,))],
    )(*pairs)
    my_chip = 2 * lax.axis_index("x") + lax.axis_index("y")
    return [lax.dynamic_update_index_in_dim(o, lax.dynamic_index_in_dim(p, my_chip, 0, keepdims=False), my_chip, 0) for o, p in zip(got, pairs)]


def join_halves(halves, name):
    n = len(halves)
    chunks = [(t, l, j) for t in range(n) for l in range(halves[t].shape[0]) for j in range(2)]

    def body(*refs):
        ins, outs = refs[:n], refs[n:2 * n]
        send_sems, recv_sems = refs[2 * n:]
        x, y, c = _place()
        sibling = (x, y, 1 - c)
        pending = []
        for k, (t, l, j) in enumerate(chunks):
            h_ref, o_ref = ins[t], outs[t]
            hr = h_ref.shape[1]
            rows = hr // 2
            lo = pl.multiple_of(c * hr + j * rows, 8)
            lo_sib = pl.multiple_of((1 - c) * hr + j * rows, 8)
            src = h_ref.at[l, pl.ds(j * rows, rows), :]
            cp = pltpu.make_async_remote_copy(src_ref=src, dst_ref=o_ref.at[l, pl.ds(lo, rows), :], send_sem=send_sems.at[k],
                                              recv_sem=recv_sems.at[k], device_id=sibling, device_id_type=MESH)
            cp.start()
            got = pltpu.make_async_remote_copy(src_ref=src, dst_ref=o_ref.at[l, pl.ds(lo_sib, rows), :], send_sem=send_sems.at[k],
                                               recv_sem=recv_sems.at[k], device_id=sibling, device_id_type=MESH)
            pending.append((cp, got))
        for cp, got in pending:
            got.wait_recv()
            cp.wait_send()

    got = pl.pallas_call(
        body, name=name,
        out_shape=[jax.ShapeDtypeStruct((h.shape[0], 2 * h.shape[1], h.shape[2]), h.dtype) for h in halves],
        in_specs=[ANY] * n, out_specs=[ANY] * n,
        scratch_shapes=[pltpu.SemaphoreType.DMA((len(chunks),)), pltpu.SemaphoreType.DMA((len(chunks),))],
    )(*halves)
    ci = lax.axis_index("c")
    return [lax.dynamic_update_slice(g, h, (0, ci * h.shape[1], 0)) for g, h in zip(got, halves)]


WEIGHTS = ['ada_w', 'ada_b', 'norm_mix', 'norm_ffn', 'norm_final', 'ev_w_in', 'ev_conv_w', 'ev_conv_b', 'ev_gate_a_w', 'ev_gate_a_b',
           'ev_gate_x_w', 'ev_gate_x_b', 'ev_lambda', 'ev_w_out', 'od_w_in', 'od_sinks', 'od_a_re', 'od_a_im', 'od_b_re', 'od_b_im',
           'od_c_re', 'od_c_im', 'od_d', 'od_log_dt', 'od_glu_w', 'od_glu_b', 'od_w_out', 'ffn_w_in', 'ffn_conv_w', 'ffn_conv_b', 'ffn_w_out']
BIG = ['ev_w_in', 'ev_w_out', 'od_w_in', 'od_glu_w', 'od_w_out', 'ffn_w_in', 'ffn_w_out']
COL_SHARDED = ('ev_w_in', 'od_w_in', 'ffn_w_in')
SMALL_SHARDED = ['ev_conv_w', 'od_d', 'od_glu_b', 'ffn_conv_w']
SMALL = [n for n in WEIGHTS if n not in BIG and n != 'ada_w']


def _pack(arrs):
    flat = jnp.concatenate([a.reshape(-1).astype(F32) for a in arrs])
    rows = -(-flat.shape[0] // (1024 * LANE)) * 1024
    return jnp.pad(flat, (0, rows * LANE - flat.shape[0])).reshape(rows, LANE)


def _unpack(flat, shapes):
    out, off = [], 0
    for s in shapes:
        n = math.prod(s)
        out.append(flat[..., off:off + n].reshape(flat.shape[:-1] + tuple(s)))
        off += n
    return out


def _ffn_fwd(l, h2, wf, cw, cb):
    u = mm(h2, wf['ffn_w_in'], layer=l, tm=2048, tn=256, name=f"ffn_in{l}")
    act = ffn_act(u, cw, cb, f"ffn_act{l}")
    f = mm(act, wf['ffn_w_out'], layer=l, tm=1024, tn=512, name=f"ffn_out{l}")
    return f, dict(u=u, act=act)


def _ffn_bwd(l, df, s, h2, wf, cw, cb):
    dact = mm(df, wf['ffn_w_out'], layer=l, tb=True, tm=512, tn=D_FF, tk=512, name=f"ffn_dact{l}")
    dwo = mm(s['act'], df, ta=True, out_dtype=BF16, tm=D_FF, tn=512, tk=512, name=f"ffn_dwo{l}")
    du, dcw, dcb = ffn_act_bwd(s['u'], dact, cw, cb, f"ffn_act_bwd{l}")
    dh2 = mm(du, wf['ffn_w_in'], layer=l, tb=True, tm=1024, tn=512, tk=D_FF, name=f"ffn_dh{l}")
    dwi = mm(h2, du, ta=True, out_dtype=BF16, tm=2048, tn=256, name=f"ffn_dwi{l}")
    return dh2, dwi, dwo, dcw, dcb


def _even_fwd(e, h1, a, wf, fs, tabs):
    cos, sin = tabs
    proj = mm(h1, wf['ev_w_in'], layer=e, name=f"ev_in{e}")
    q, k, v = qkv_rope_even(proj, cos, sin, f"ev_rope{e}")
    outs, lses = [], []
    for window, d in A_PATTERNS:
        o, lse = attn_fwd(perm(q, d)[:, None], perm(k, d), perm(v, d), window // d, LANE ** -0.5, f"ev_attn{e}_{d}")
        outs.append(unperm(o[:, 0], d))
        lses.append(unperm(lse[:, 0], d))
    attn_bf, attn, lse_tot = merge3(outs, lses, f"ev_merge{e}")
    lru, hs = rglru_fwd(proj, fs['ev_conv_w'][e], a['ev_conv_b'][e][None], a['ev_gate_a_w'][e], a['ev_gate_a_b'][e][None],
                        a['ev_gate_x_w'][e], a['ev_gate_x_b'][e][None], a['ev_lambda'][e][None], 24, 32, f"ev_lru{e}")
    mix = jnp.concatenate([attn_bf, lru], axis=1)
    y = mm(mix, wf['ev_w_out'], layer=e, name=f"ev_out{e}")
    return y, dict(proj=proj, q=q, k=k, v=v, attn=attn, lse=lse_tot, hs=hs, mix=mix)


def _even_bwd(e, dyg, s, h1, a, wf, fs, tabs, gs):
    cos, sin = tabs
    dmix = mm(dyg, wf['ev_w_out'], layer=e, tb=True, name=f"ev_dmix{e}")
    dwo = mm(s['mix'], dyg, ta=True, out_dtype=BF16, name=f"ev_dwo{e}")
    do_bf, dvec = even_attn_prep(dmix, s['attn'], f"ev_prep{e}")
    dqs, dks, dvs = [], [], []
    for window, d in A_PATTERNS:
        dq, dk, dv = attn_bwd(perm(s['q'], d)[:, None], perm(s['k'], d), perm(s['v'], d), perm(do_bf, d)[:, None],
                              perm(s['lse'], d)[:, None], perm(dvec, d)[:, None], window // d, LANE ** -0.5, f"ev_attn_bwd{e}_{d}")
        dqs.append(unperm(dq[:, 0], d))
        dks.append(unperm(dk, d))
        dvs.append(unperm(dv, d))
    dxb, dyb, dcw, dcb, dgaw, dgab, dgxw, dgxb, dlam = rglru_bwd(
        s['proj'], s['hs'], dmix, 8, fs['ev_conv_w'][e], a['ev_conv_b'][e][None], a['ev_gate_a_w'][e], a['ev_gate_a_b'][e][None],
        a['ev_gate_x_w'][e], a['ev_gate_x_b'][e][None], a['ev_lambda'][e][None], 24, 32, f"ev_lru_bwd{e}")
    for n, g in (('ev_conv_w', dcw), ('ev_conv_b', dcb[0]), ('ev_gate_a_w', dgaw), ('ev_gate_a_b', dgab[0]), ('ev_gate_x_w', dgxw),
                 ('ev_gate_x_b', dgxb[0]), ('ev_lambda', dlam[0])):
        gs[n][e] = g
    dproj = even_dproj(dqs, dks, dvs, dxb, dyb, cos, sin, f"ev_dproj{e}")
    dh1 = mm(dproj, wf['ev_w_in'], layer=e, tb=True, tk=2560, name=f"ev_dh{e}")
    dwi = mm(h1, dproj, ta=True, out_dtype=BF16, tm=2048, tn=512, name=f"ev_dwi{e}")
    return dh1, dwi, dwo


def _odd_fwd(o, h1, a, wf, fs, tabs):
    cos, sin = tabs
    t = h1.shape[0]
    proj = mm(h1, wf['od_w_in'], layer=o, name=f"od_in{o}")
    qr, kr, vv = qkv_rope_odd(proj, cos, sin, f"od_rope{o}")
    qh = qr.reshape(t, 2, 8, 64).transpose(1, 2, 0, 3)
    kh = kr.reshape(t, 2, 64).transpose(1, 0, 2)
    vh = vv.reshape(t, 2, 64).transpose(1, 0, 2)
    oh, lse = attn_fwd(qh, kh, vh, 127, 64 ** -0.5, f"od_attn{o}")
    sink_b = jnp.broadcast_to(a['od_sinks'][o].reshape(16, 1, 1), (16, 1, 64))
    oh, lse = oh.reshape(16, t, 64), lse.reshape(16, t, 64)
    attn_hm = sink_fwd(oh, lse, sink_b, f"od_sink{o}")
    attn_tm = attn_hm.transpose(1, 0, 2).reshape(t, A_W)
    prep_in = tuple(a[n][o] for n in ('od_a_re', 'od_a_im', 'od_b_re', 'od_b_im', 'od_c_re', 'od_c_im', 'od_log_dt'))
    (are, aim, bre, bim, cre, cim), prep_vjp = jax.vjp(s5_prep, *prep_in)
    s5w = (bre, bim, are, aim, cre, cim, fs['od_d'][o][None])
    y, sr, si = s5_fwd(proj, 10, *s5w, f"od_s5{o}")
    z = glu_z(y, f"od_glu_z{o}")
    gpre = mm(z, wf['od_glu_w'], layer=o, name=f"od_glu_mm{o}")
    glu_b = fs['od_glu_b'][o][None]
    ssm = glu_out(y, gpre, glu_b, f"od_glu_out{o}")
    mix = jnp.concatenate([attn_tm, ssm], axis=1)
    yo = mm(mix, wf['od_w_out'], layer=o, name=f"od_out{o}")
    return yo, dict(proj=proj, qh=qh, kh=kh, vh=vh, oh=oh, lse=lse, sink_b=sink_b, prep_vjp=prep_vjp, s5w=s5w, y=y, sr=sr, si=si,
                    z=z, gpre=gpre, glu_b=glu_b, mix=mix)


def _odd_bwd(o, dyg, s, h1, a, wf, fs, tabs, gs):
    cos, sin = tabs
    t = h1.shape[0]
    dmix = mm(dyg, wf['od_w_out'], layer=o, tb=True, name=f"od_dmix{o}")
    dwo = mm(s['mix'], dyg, ta=True, out_dtype=BF16, name=f"od_dwo{o}")
    dgp, dglu_b = glu_bwd_gate(dmix, s['y'], s['gpre'], s['glu_b'], f"od_glu_bwd_gate{o}")
    dz_mm = mm(dgp, wf['od_glu_w'], layer=o, tb=True, name=f"od_glu_dz{o}")
    dglu_w = mm(s['z'], dgp, ta=True, out_dtype=BF16, name=f"od_glu_dw{o}")
    dy = glu_bwd_y(dmix, s['y'], s['gpre'], s['glu_b'], dz_mm, f"od_glu_bwd_y{o}")
    du, dbre, dbim, dare, daim, dcre, dcim, dd = s5_bwd(s['proj'], 10, dy, s['sr'], s['si'], *s['s5w'], f"od_s5_bwd{o}")
    ga = s['prep_vjp']((dare, daim, dbre, dbim, dcre, dcim))
    for n, g in zip(('od_a_re', 'od_a_im', 'od_b_re', 'od_b_im', 'od_c_re', 'od_c_im', 'od_log_dt'), ga):
        gs[n][o] = g
    gs['od_d'][o] = dd[0]
    gs['od_glu_b'][o] = dglu_b[0]
    dattn_hm = dmix[:, :A_W].reshape(t, 16, 64).transpose(1, 0, 2)
    do, dvec, dsink = sink_bwd(dattn_hm, s['oh'], s['lse'], s['sink_b'], f"od_sink_bwd{o}")
    gs['od_sinks'][o] = dsink[:, 0, 0]
    dq, dk, dv = attn_bwd(s['qh'], s['kh'], s['vh'], do.reshape(2, 8, t, 64), s['lse'].reshape(2, 8, t, 64), dvec.reshape(2, 8, t, 64),
                          127, 64 ** -0.5, f"od_attn_bwd{o}")
    dq_tm = dq.transpose(2, 0, 1, 3).reshape(t, A_W)
    dk_tm = dk.transpose(1, 0, 2).reshape(t, LANE)
    dv_tm = dv.transpose(1, 0, 2).reshape(t, LANE)
    dproj = odd_dproj(dq_tm, dk_tm, dv_tm, du, cos, sin, f"od_dproj{o}")
    dh1 = mm(dproj, wf['od_w_in'], layer=o, tb=True, name=f"od_dh{o}")
    dwi = mm(h1, dproj, ta=True, out_dtype=BF16, tm=2048, tn=768, name=f"od_dwi{o}")
    return dh1, dwi, dwo, dglu_w


def kernel(x, c, positions, ada_w, ada_b, norm_mix, norm_ffn, norm_final, ev_w_in, ev_conv_w, ev_conv_b, ev_gate_a_w, ev_gate_a_b, ev_gate_x_w, ev_gate_x_b, ev_lambda, ev_w_out, od_w_in, od_sinks, od_a_re, od_a_im, od_b_re, od_b_im, od_c_re, od_c_im, od_d, od_log_dt, od_glu_w, od_glu_b, od_w_out, ffn_w_in, ffn_conv_w, ffn_conv_b, ffn_w_out, loss_target, m_ada_w, m_ada_b, m_norm_mix, m_norm_ffn, m_norm_final, m_ev_w_in, m_ev_conv_w, m_ev_conv_b, m_ev_gate_a_w, m_ev_gate_a_b, m_ev_gate_x_w, m_ev_gate_x_b, m_ev_lambda, m_ev_w_out, m_od_w_in, m_od_sinks, m_od_a_re, m_od_a_im, m_od_b_re, m_od_b_im, m_od_c_re, m_od_c_im, m_od_d, m_od_log_dt, m_od_glu_w, m_od_glu_b, m_od_w_out, m_ffn_w_in, m_ffn_conv_w, m_ffn_conv_b, m_ffn_w_out, v_ada_w, v_ada_b, v_norm_mix, v_norm_ffn, v_norm_final, v_ev_w_in, v_ev_conv_w, v_ev_conv_b, v_ev_gate_a_w, v_ev_gate_a_b, v_ev_gate_x_w, v_ev_gate_x_b, v_ev_lambda, v_ev_w_out, v_od_w_in, v_od_sinks, v_od_a_re, v_od_a_im, v_od_b_re, v_od_b_im, v_od_c_re, v_od_c_im, v_od_d, v_od_log_dt, v_od_glu_w, v_od_glu_b, v_od_w_out, v_ffn_w_in, v_ffn_conv_w, v_ffn_conv_b, v_ffn_w_out):
    a = dict(locals())
    xi, yi, ci = _place()
    chip = 2 * xi + yi
    me = 2 * chip + ci
    x0, target, pos = x[0], loss_target[0], positions[0]
    d = D_MODEL

    g0 = all_gather8(_pack([c] + [a[n] for n in SMALL_SHARDED]), "gather_small").reshape(N_DEV, -1)
    c_all = g0[:, :d]
    fs, off = {}, d
    for n in SMALL_SHARDED:
        sh = a[n].shape
        parts = g0[0::2, off:off + math.prod(sh)].reshape((N_CHIP,) + sh)
        fs[n] = jnp.moveaxis(parts, 0, -2).reshape(sh[:-1] + (N_CHIP * sh[-1],))
        off += math.prod(sh)
    cond_all = silu_rows(c_all, "silu")

    modp = jnp.stack([mm(cond_all, ada_w, layer=l, tm=8, tn=512, name=f"mod{l}") for l in range(DEPTH)])
    mod_all = all_gather8(modp.reshape(-1, LANE), "gather_mod").reshape(N_DEV, DEPTH, N_DEV, 6 * d // N_CHIP)[0::2]
    mod_me = lax.dynamic_index_in_dim(mod_all, me, axis=2, keepdims=False)
    mod = jnp.transpose(mod_me, (1, 0, 2)).reshape(DEPTH, 6 * d) + ada_b
    mods = [[mod[l, i * d:(i + 1) * d][None] for i in range(6)] for l in range(DEPTH)]

    full = gather_weights([a[n].astype(BF16) for n in BIG], "gather_weights")
    wf = {}
    for n, f in zip(BIG, full):
        _, nl, r, cc = f.shape
        if n in COL_SHARDED:
            wf[n] = jnp.transpose(f, (1, 2, 0, 3)).reshape(nl, r, N_CHIP * cc)
        else:
            wf[n] = jnp.transpose(f, (1, 0, 2, 3)).reshape(nl, N_CHIP * r, cc)

    ffn_cw, ffn_cb = fs['ffn_conv_w'], ffn_conv_b

    tabs128 = rope_tables(pos, 64)
    tabs64 = rope_tables(pos, 32)

    saved = []
    xcur = x0
    for l in range(DEPTH):
        sh1, sc1, g1, sh2, sc2, g2 = mods[l]
        s = dict(x=xcur)
        s['h1'] = norm_mod(xcur, norm_mix[l][None], sc1, sh1, f"norm_mix{l}")
        if l % 2 == 0:
            s['y'], s['mixer'] = _even_fwd(l // 2, s['h1'], a, wf, fs, tabs128)
        else:
            s['y'], s['mixer'] = _odd_fwd(l // 2, s['h1'], a, wf, fs, tabs64)
        s['x2'], s['h2'] = resid_norm_mod(xcur, s['y'], g1, norm_ffn[l][None], sc2, sh2, f"norm_ffn{l}")
        s['f'], s['ffn'] = _ffn_fwd(l, s['h2'], wf, ffn_cw[l], ffn_cb[l][None])
        xcur = resid_add(s['x2'], s['f'], g2, f"resid{l}")
        saved.append(s)

    dx, loss_part, dnf = final_loss(xcur, norm_final[None], target, "loss")
    loss = lax.psum(loss_part[0, 0], ("x", "y", "c"))

    gs = {n: {} for n in SMALL}
    gbig = {n: {} for n in BIG}
    dmod = {}
    gs['norm_final'][0] = dnf[0]
    for l in reversed(range(DEPTH)):
        sh1, sc1, g1, sh2, sc2, g2 = mods[l]
        s = saved[l]
        df, dg2 = resid_bwd(dx, s['f'], g2, f"resid_bwd_ffn{l}")
        dh2, dwi, dwo, dcw, dcb = _ffn_bwd(l, df, s['ffn'], s['h2'], wf, ffn_cw[l], ffn_cb[l][None])
        gbig['ffn_w_in'][l], gbig['ffn_w_out'][l], gs['ffn_conv_w'][l], gs['ffn_conv_b'][l] = dwi, dwo, dcw, dcb[0]
        dx2, dsh2, dsc2, dgam2 = norm_mod_bwd(dh2, s['x2'], dx, norm_ffn[l][None], sc2, f"norm_ffn_bwd{l}")
        gs['norm_ffn'][l] = dgam2[0]
        dyg, dg1 = resid_bwd(dx2, s['y'], g1, f"resid_bwd_mix{l}")
        if l % 2 == 0:
            dh1, dwi, dwo = _even_bwd(l // 2, dyg, s['mixer'], s['h1'], a, wf, fs, tabs128, gs)
            gbig['ev_w_in'][l // 2], gbig['ev_w_out'][l // 2] = dwi, dwo
        else:
            dh1, dwi, dwo, dglu_w = _odd_bwd(l // 2, dyg, s['mixer'], s['h1'], a, wf, fs, tabs64, gs)
            gbig['od_w_in'][l // 2], gbig['od_w_out'][l // 2], gbig['od_glu_w'][l // 2] = dwi, dwo, dglu_w
        dx, dsh1, dsc1, dgam1 = norm_mod_bwd(dh1, s['x'], dx2, norm_mix[l][None], sc1, f"norm_mix_bwd{l}")
        gs['norm_mix'][l] = dgam1[0]
        dmod[l] = jnp.concatenate([dsh1, dsc1, dg1, dsh2, dsc2, dg2], axis=1)[0]
    grad_x = dx[None]
    gs['ada_b'] = dmod

    grads = {}
    g5 = []
    for n in BIG:
        g = jnp.stack([gbig[n][i] for i in range(len(gbig[n]))])
        nl = g.shape[0]
        if n in COL_SHARDED:
            hr, ns = g.shape[1] // 2, g.shape[2] // N_CHIP
            g5.append(g.reshape(nl, 2, hr, N_CHIP, ns).transpose(1, 3, 0, 2, 4))
        else:
            hr = g.shape[1] // N_CHIP // 2
            g5.append(g.reshape(nl, N_CHIP, 2, hr, g.shape[2]).transpose(2, 1, 0, 3, 4))
    from_sibling = pair_swap(g5, "pair_swap_grads")
    pair = [pair_sum(g, r, ci, f"pair_sum_{n}") for n, g, r in zip(BIG, g5, from_sibling)]
    pieces = exchange_chips(pair, "exchange_grads")
    halves = [sum_devices(p, f"sum_{n}") for n, p in zip(BIG, pieces)]
    for n, g in zip(BIG, join_halves(halves, "join_grads")):
        grads[n] = g.reshape(a[n].shape)

    small_full = [jnp.stack([gs[n][i] for i in range(len(gs[n]))]) if n != 'norm_final' else gs[n][0] for n in SMALL]
    small_shapes = [g.shape for g in small_full]
    gs_all = all_gather8(_pack(small_full), "gather_small_grads")
    gs_sum = sum_devices(gs_all[:, None], "sum_small").reshape(-1)
    for n, g in zip(SMALL, _unpack(gs_sum, small_shapes)):
        if n in SMALL_SHARDED:
            w = a[n].shape[-1]
            g = lax.dynamic_slice_in_dim(g, chip * w, w, axis=g.ndim - 1)
        grads[n] = g
    assert SMALL[0] == 'ada_b'
    dmod_all = gs_all.reshape(N_DEV, -1)[:, :DEPTH * 6 * d].reshape(N_DEV, DEPTH, 6 * d)
    wcols = 6 * d // N_CHIP
    grads['ada_w'] = jnp.stack([
        mm(cond_all, lax.dynamic_slice_in_dim(dmod_all[:, l], chip * wcols, wcols, axis=1), ta=True, tm=2048, tn=512, name=f"ada_dw{l}")
        for l in range(DEPTH)])

    delta, new_m, new_v = {}, {}, {}
    for n in ['ada_w'] + BIG:
        sh = a[n].shape
        two_d = lambda t: t.reshape(-1, sh[-1])
        dl, nm, nv = adamw(two_d(a[n]), two_d(grads[n]), two_d(a['m_' + n]), two_d(a['v_' + n]), f"adamw_{n}")
        delta[n], new_m[n], new_v[n] = dl.reshape(sh), nm.reshape(sh), nv.reshape(sh)
    shapes = [a[n].shape for n in SMALL]
    dl, nm, nv = adamw(_pack([a[n] for n in SMALL]), _pack([grads[n] for n in SMALL]), _pack([a['m_' + n] for n in SMALL]),
                       _pack([a['v_' + n] for n in SMALL]), "adamw_small")
    for n, t1, t2, t3 in zip(SMALL, _unpack(dl.reshape(-1), shapes), _unpack(nm.reshape(-1), shapes), _unpack(nv.reshape(-1), shapes)):
        delta[n], new_m[n], new_v[n] = t1, t2, t3

    return (loss, grad_x, *[grads[n] for n in WEIGHTS], *[delta[n] for n in WEIGHTS], *[new_m[n] for n in WEIGHTS],
            *[new_v[n] for n in WEIGHTS])
```
